```python
import math
import jax, jax.numpy as jnp
from jax import lax
import numpy as np

D_MODEL = 1024
BATCH = 32
SEQ = 2048
DEPTH = 1
DEC_BATCH = 16
DEC_SEQ = 16
PAST_LEN = 1024

CHUNK = 64
Q_BLOCK = 128
ROPE_THETA = 10000.0
EPS = 1e-6
RET_HEADS = 4
RET_DK = D_MODEL // RET_HEADS
RET_DV = 2 * RET_DK
RET_W_QK = RET_HEADS * RET_DK
RET_W_V = RET_HEADS * RET_DV
DIFF_HEADS = 8
DIFF_DH = D_MODEL // (2 * DIFF_HEADS)
DIFF_DV = 2 * DIFF_DH
DIFF_W_QK = DIFF_HEADS * 2 * DIFF_DH
DIFF_W_V = DIFF_HEADS * DIFF_DV
N_BRANCHES = 2
IN_COLS = 2 * RET_W_QK + 2 * RET_W_V + 2 * DIFF_W_QK + DIFF_W_V + N_BRANCHES * D_MODEL
N_GROUPS = 4
EXPERTS_PER_GROUP = 4
N_EXPERTS = N_GROUPS * EXPERTS_PER_GROUP
TOP_K = 2
D_EXPERT = 512

kernel_name = 'hybrid_retention_diffattn_hmoe_stream_step'


def rms_norm(x, g=None):
    xf = x.astype(jnp.float32)
    y = xf * lax.rsqrt(jnp.mean(xf * xf, axis=-1, keepdims=True) + EPS)
    if g is not None:
        y = y * g.astype(jnp.float32)
    return y.astype(x.dtype)


def rope(x, pos):
    d = x.shape[-1]
    inv = ROPE_THETA ** (-jnp.arange(0, d, 2, dtype=jnp.float32) / d)
    ang = pos.astype(jnp.float32)[:, None] * inv[None, :]
    cos = jnp.cos(ang)[:, None, :]
    sin = jnp.sin(ang)[:, None, :]
    x1, x2 = jnp.split(x.astype(jnp.float32), 2, axis=-1)
    return jnp.concatenate([x1 * cos - x2 * sin, x1 * sin + x2 * cos], axis=-1).astype(x.dtype)


def retention_log_gamma():
    return jnp.log1p(-jnp.exp2(-5.0 - jnp.arange(RET_HEADS, dtype=jnp.float32)))


def retention_block(q, k, v, s, log_gamma):
    L = q.shape[1]
    idx = jnp.arange(L, dtype=jnp.float32)
    rel = idx[:, None] - idx[None, :]
    decay = jnp.where(rel >= 0, jnp.exp(jnp.maximum(rel, 0.0)[None] * log_gamma[:, None, None]), 0.0)
    scores = jnp.einsum('blhd,bmhd->bhlm', q, k) * decay[None]
    intra = jnp.einsum('bhlm,bmhv->blhv', scores, v)
    q_decay = jnp.exp((idx + 1.0)[:, None] * log_gamma[None, :])
    cross = jnp.einsum('blhd,bhdv->blhv', q, s) * q_decay[None, :, :, None]
    k_decay = jnp.exp((L - 1.0 - idx)[:, None] * log_gamma[None, :])
    s_new = s * jnp.exp(L * log_gamma)[None, :, None, None] + jnp.einsum(
        'blhd,blhv->bhdv', k * k_decay[None, :, :, None], v)
    return intra + cross, s_new


def retention(q, k, v, s0, log_gamma):
    B, L = q.shape[0], q.shape[1]
    if L <= CHUNK:
        return retention_block(q, k, v, s0, log_gamma)
    nc = L // CHUNK

    def to_chunks(t):
        return t.reshape(B, nc, CHUNK, *t.shape[2:]).swapaxes(0, 1)

    def step(s, qkv):
        o, s_next = retention_block(qkv[0], qkv[1], qkv[2], s, log_gamma)
        return s_next, o

    s_fin, o = lax.scan(step, s0, (to_chunks(q), to_chunks(k), to_chunks(v)))
    return o.swapaxes(0, 1).reshape(B, L, *o.shape[3:]), s_fin


def diff_attend(q, k, v, q_pos, k_pos, lam):
    s = jnp.einsum('bqhcd,bkhcd->bhcqk', q, k).astype(jnp.float32) * (DIFF_DH ** -0.5)
    visible = (k_pos // CHUNK)[None, :] <= (q_pos // CHUNK)[:, None]
    s = jnp.where(visible, s, -1e30)
    p = jax.nn.softmax(s, axis=-1)
    a = p[:, :, 0] - lam * p[:, :, 1]
    return jnp.einsum('bhqk,bkhv->bqhv', a.astype(v.dtype), v)


def diff_attention(q, k, v, q_pos, k_pos, lam):
    B, L = q.shape[0], q.shape[1]
    if L <= Q_BLOCK:
        return diff_attend(q, k, v, q_pos, k_pos, lam)
    nb = L // Q_BLOCK
    qb = q.reshape(B, nb, Q_BLOCK, *q.shape[2:]).swapaxes(0, 1)
    pb = q_pos.reshape(nb, Q_BLOCK)
    ob = lax.map(lambda a: diff_attend(a[0], k, v, a[1], k_pos, lam), (qb, pb))
    return ob.swapaxes(0, 1).reshape(B, L, *ob.shape[3:])


def hierarchical_moe(h, w_group, b_group, w_expert_router, b_expert, w1, w3, w2):
    B, L, D = h.shape
    hf = h.reshape(B * L, D)
    g_logits = (hf @ w_group).astype(jnp.float32) + b_group.astype(jnp.float32)
    g_idx = jnp.argmax(g_logits, axis=-1)
    g_prob = jnp.take_along_axis(jax.nn.softmax(g_logits, axis=-1), g_idx[:, None], axis=-1)
    e_logits = ((hf @ w_expert_router).astype(jnp.float32) + b_expert.astype(jnp.float32)).reshape(
        -1, N_GROUPS, EXPERTS_PER_GROUP)
    e_in_group = jnp.take_along_axis(e_logits, g_idx[:, None, None], axis=1)[:, 0]
    top_v, top_i = lax.top_k(e_in_group, TOP_K)
    weights = jax.nn.softmax(top_v, axis=-1) * g_prob
    expert_id = g_idx[:, None] * EXPERTS_PER_GROUP + top_i
    combine = jnp.sum(jax.nn.one_hot(expert_id, N_EXPERTS, dtype=jnp.float32) * weights[..., None], axis=1)
    y = jnp.zeros_like(hf)
    for e in range(N_EXPERTS):
        ff = (jax.nn.silu(hf @ w1[e]) * (hf @ w3[e])) @ w2[e]
        y = y + combine[:, e:e + 1].astype(hf.dtype) * ff
    return y.reshape(B, L, D)


def layer(x, c, pos, ret_state, k_past, v_past, lw, li):
    B, L, _ = x.shape
    mod = jax.nn.silu(c) @ lw['w_ada'] + lw['b_ada']
    sh1, sc1, gt1, sh2, sc2, gt2 = jnp.split(mod[:, None, :], 6, axis=-1)
    h = rms_norm(x, lw['norm1_g']) * (1.0 + sc1) + sh1
    proj = h @ lw['w_in']
    sizes = [RET_W_QK, RET_W_QK, RET_W_V, RET_W_V, DIFF_W_QK, DIFF_W_QK, DIFF_W_V]
    split_at = [int(i) for i in np.cumsum(sizes)]
    rq, rk, rv, rg, dq, dk, dv, gates = jnp.split(proj, split_at, axis=-1)

    rq = rope(rq.reshape(B, L, RET_HEADS, RET_DK), pos).astype(jnp.float32)
    rk = rope(rk.reshape(B, L, RET_HEADS, RET_DK), pos).astype(jnp.float32) * (RET_DK ** -0.5)
    rv = rv.reshape(B, L, RET_HEADS, RET_DV).astype(jnp.float32)
    ret_o, ret_state_new = retention(rq, rk, rv, ret_state.astype(jnp.float32), retention_log_gamma())
    ret_o = rms_norm(ret_o).reshape(B, L, RET_W_V).astype(x.dtype) * jax.nn.silu(rg)

    dq = rope(rms_norm(dq.reshape(B, L, 2 * DIFF_HEADS, DIFF_DH), lw['diff_qnorm_g']), pos)
    dk = rope(rms_norm(dk.reshape(B, L, 2 * DIFF_HEADS, DIFF_DH), lw['diff_knorm_g']), pos)
    k_rows = dk.reshape(B, L, DIFF_HEADS, 2 * DIFF_DH)
    v_rows = dv.reshape(B, L, DIFF_HEADS, DIFF_DV)
    if k_past is None:
        k_all, v_all, k_pos = k_rows, v_rows, pos
    else:
        k_all = jnp.concatenate([k_past.astype(k_rows.dtype), k_rows], axis=1)
        v_all = jnp.concatenate([v_past.astype(v_rows.dtype), v_rows], axis=1)
        k_pos = jnp.arange(k_all.shape[1], dtype=jnp.int32)
    lam_init = 0.8 - 0.6 * math.exp(-0.3 * li)
    lam = (jnp.exp(jnp.sum(lw['diff_lambda_q1'] * lw['diff_lambda_k1']))
           - jnp.exp(jnp.sum(lw['diff_lambda_q2'] * lw['diff_lambda_k2'])) + lam_init).astype(jnp.float32)
    diff_o = diff_attention(dq.reshape(B, L, DIFF_HEADS, 2, DIFF_DH),
                            k_all.reshape(B, -1, DIFF_HEADS, 2, DIFF_DH), v_all, pos, k_pos, lam)
    diff_o = (rms_norm(diff_o, lw['diff_subln_g']) * (1.0 - lam_init)).reshape(B, L, DIFF_W_V)

    gate_ret, gate_diff = jnp.split(jax.nn.sigmoid(gates), 2, axis=-1)
    merged = gate_ret * (ret_o @ lw['w_br_ret']) + gate_diff * (diff_o @ lw['w_br_diff'])
    x = x + gt1 * (merged @ lw['w_out'])

    h2 = rms_norm(x, lw['norm2_g']) * (1.0 + sc2) + sh2
    x = x + gt2 * hierarchical_moe(h2, lw['w_group'], lw['b_group'], lw['w_expert_router'],
                                   lw['b_expert'], lw['w1'], lw['w3'], lw['w2'])
    return x, ret_state_new, k_rows, v_rows


def setup_inputs(seed: int = 0) -> dict:
    key = jax.random.key(seed)
    ks = jax.random.split(key, 32)
    f32 = jnp.float32
    D = D_MODEL

    def nrm(k, shape, scale):
        return jax.random.normal(k, shape, f32) * scale

    return {
        'x_prompt': nrm(ks[0], (BATCH, SEQ, D), 1.0),
        'x_sample': nrm(ks[1], (DEC_BATCH, DEC_SEQ, D), 1.0),
        'c_prompt': nrm(ks[2], (BATCH, D), 1.0),
        'c_sample': nrm(ks[3], (DEC_BATCH, D), 1.0),
        'cache_diff_k': nrm(ks[4], (DEPTH, DEC_BATCH, PAST_LEN, DIFF_HEADS, 2 * DIFF_DH), 1.0),
        'cache_diff_v': nrm(ks[5], (DEPTH, DEC_BATCH, PAST_LEN, DIFF_HEADS, DIFF_DV), 1.0),
        'state_ret': nrm(ks[6], (DEPTH, DEC_BATCH, RET_HEADS, RET_DK, RET_DV), 0.5),
        'w_ada': nrm(ks[7], (DEPTH, D, 6 * D), 0.5 * D ** -0.5),
        'b_ada': nrm(ks[8], (DEPTH, 6 * D), 0.01),
        'norm1_g': 1.0 + nrm(ks[9], (DEPTH, D), 0.01),
        'w_in': nrm(ks[10], (DEPTH, D, IN_COLS), D ** -0.5),
        'diff_qnorm_g': 1.0 + nrm(ks[11], (DEPTH, DIFF_DH), 0.01),
        'diff_knorm_g': 1.0 + nrm(ks[12], (DEPTH, DIFF_DH), 0.01),
        'diff_lambda_q1': nrm(ks[13], (DEPTH, DIFF_DH), 0.1),
        'diff_lambda_k1': nrm(ks[14], (DEPTH, DIFF_DH), 0.1),
        'diff_lambda_q2': nrm(ks[15], (DEPTH, DIFF_DH), 0.1),
        'diff_lambda_k2': nrm(ks[16], (DEPTH, DIFF_DH), 0.1),
        'diff_subln_g': 1.0 + nrm(ks[17], (DEPTH, DIFF_DV), 0.01),
        'w_br_ret': nrm(ks[18], (DEPTH, RET_W_V, D), RET_W_V ** -0.5),
        'w_br_diff': nrm(ks[19], (DEPTH, DIFF_W_V, D), DIFF_W_V ** -0.5),
        'w_out': nrm(ks[20], (DEPTH, D, D), D ** -0.5),
        'norm2_g': 1.0 + nrm(ks[21], (DEPTH, D), 0.01),
        'w_group': nrm(ks[22], (DEPTH, D, N_GROUPS), D ** -0.5),
        'b_group': nrm(ks[23], (DEPTH, N_GROUPS), 0.01),
        'w_expert_router': nrm(ks[24], (DEPTH, D, N_EXPERTS), D ** -0.5),
        'b_expert': nrm(ks[25], (DEPTH, N_EXPERTS), 0.01),
        'w1': nrm(ks[26], (DEPTH, N_EXPERTS, D, D_EXPERT), D ** -0.5),
        'w3': nrm(ks[27], (DEPTH, N_EXPERTS, D, D_EXPERT), D ** -0.5),
        'w2': nrm(ks[28], (DEPTH, N_EXPERTS, D_EXPERT, D), D_EXPERT ** -0.5),
    }


def reference(x_prompt, x_sample, c_prompt, c_sample, cache_diff_k, cache_diff_v, state_ret,
              w_ada, b_ada, norm1_g, w_in, diff_qnorm_g, diff_knorm_g, diff_lambda_q1, diff_lambda_k1,
              diff_lambda_q2, diff_lambda_k2, diff_subln_g, w_br_ret, w_br_diff, w_out, norm2_g,
              w_group, b_group, w_expert_router, b_expert, w1, w3, w2):
    B, S, _ = x_prompt.shape
    Ls = x_sample.shape[1]
    past = cache_diff_k.shape[2]
    pos_p = jnp.arange(S, dtype=jnp.int32)
    pos_s = past + jnp.arange(Ls, dtype=jnp.int32)
    xp, xs = x_prompt, x_sample
    kp_l, vp_l, sp_l, ks_l, vs_l, ss_l = [], [], [], [], [], []
    for li in range(DEPTH):
        lw = {
            'w_ada': w_ada[li], 'b_ada': b_ada[li], 'norm1_g': norm1_g[li], 'w_in': w_in[li],
            'diff_qnorm_g': diff_qnorm_g[li], 'diff_knorm_g': diff_knorm_g[li],
            'diff_lambda_q1': diff_lambda_q1[li], 'diff_lambda_k1': diff_lambda_k1[li],
            'diff_lambda_q2': diff_lambda_q2[li], 'diff_lambda_k2': diff_lambda_k2[li],
            'diff_subln_g': diff_subln_g[li], 'w_br_ret': w_br_ret[li], 'w_br_diff': w_br_diff[li],
            'w_out': w_out[li], 'norm2_g': norm2_g[li], 'w_group': w_group[li], 'b_group': b_group[li],
            'w_expert_router': w_expert_router[li], 'b_expert': b_expert[li],
            'w1': w1[li], 'w3': w3[li], 'w2': w2[li],
        }
        s0 = jnp.zeros((B, RET_HEADS, RET_DK, RET_DV), jnp.float32)
        xp, sp, kp, vp = layer(xp, c_prompt, pos_p, s0, None, None, lw, li)
        xs, ss, kss, vss = layer(xs, c_sample, pos_s, state_ret[li], cache_diff_k[li], cache_diff_v[li], lw, li)
        kp_l.append(kp)
        vp_l.append(vp)
        sp_l.append(sp)
        ks_l.append(kss)
        vs_l.append(vss)
        ss_l.append(ss)
    return (xp, xs, jnp.stack(kp_l), jnp.stack(vp_l), jnp.stack(sp_l).astype(x_prompt.dtype),
            jnp.stack(ks_l), jnp.stack(vs_l), jnp.stack(ss_l).astype(state_ret.dtype))
```

```python
import functools
import math

import jax
import jax.numpy as jnp
from jax import lax
from jax.experimental import pallas as pl
from jax.experimental.pallas import tpu as pltpu

F32 = jnp.float32
BF16 = jnp.bfloat16

D_MODEL = 1024
CHUNK = 64
ROPE_THETA = 10000.0
EPS = 1e-6
RET_HEADS = 4
RET_DK = D_MODEL // RET_HEADS
RET_DV = 2 * RET_DK
DIFF_HEADS = 8
DIFF_DH = D_MODEL // (2 * DIFF_HEADS)
DIFF_DV = 2 * DIFF_DH
N_GROUPS = 4
EXPERTS_PER_GROUP = 4
N_EXPERTS = N_GROUPS * EXPERTS_PER_GROUP
D_EXPERT = 512

LANES = 128
COL_BLOCK = 1024
N_COL_BLOCKS = 11
P_COLS = 9 * COL_BLOCK
P_RQ, P_RK, P_RV, P_RG, P_DQ, P_GATE = 0, 1024, 2048, 4096, 6144, 7168
VMEM_LIMIT = 56 * 1024 * 1024


def _cparams(*sem):
    return pltpu.CompilerParams(dimension_semantics=sem, vmem_limit_bytes=VMEM_LIMIT)


def _dot(a, b):
    return jnp.dot(a, b, preferred_element_type=F32)


def _dot_nt(a, b):
    return lax.dot_general(a, b, (((1,), (1,)), ((), ())), preferred_element_type=F32)


def _adaln_kernel(c_ref, w_ref, b_ref, o_ref):
    c = c_ref[...]
    a = c * jax.nn.sigmoid(c)
    o_ref[...] = jnp.dot(a, w_ref[...], preferred_element_type=F32,
                         precision=lax.Precision.HIGHEST) + b_ref[...]


def _adaln(c, w_ada, b_ada):
    rows, d = c.shape
    n = w_ada.shape[1]
    tn = 1536
    return pl.pallas_call(
        _adaln_kernel,
        out_shape=jax.ShapeDtypeStruct((rows, n), F32),
        grid=(n // tn,),
        in_specs=[pl.BlockSpec((rows, d), lambda j: (0, 0)),
                  pl.BlockSpec((d, tn), lambda j: (0, j)),
                  pl.BlockSpec((1, tn), lambda j: (0, j))],
        out_specs=pl.BlockSpec((rows, tn), lambda j: (0, j)),
        compiler_params=_cparams("arbitrary"),
        name="adaln",
    )(c, w_ada, b_ada.reshape(1, n))


def _rms(x):
    return x * lax.rsqrt(jnp.mean(x * x, axis=-1, keepdims=True) + EPS)


def _inproj_kernel(x_ref, sc_ref, sh_ref, g1_ref, w_ref, rc_ref, rs_ref, dc_ref, dsn_ref, dsp_ref,
                   gq_ref, gk_ref, bd_ref, p_ref, k_ref, v_ref, h_scr):
    j = pl.program_id(1)

    @pl.when(j == 0)
    def _():
        h = _rms(x_ref[...]) * g1_ref[...]
        h = h * (1.0 + sc_ref[...]) + sh_ref[...]
        h_scr[...] = h.astype(BF16)

    acc = _dot(h_scr[...], w_ref[...])

    def ret_rope(scale):
        c = rc_ref[...]
        s = rs_ref[...]
        for hd in range(RET_HEADS):
            lo = hd * RET_DK
            x1 = acc[:, lo:lo + LANES]
            x2 = acc[:, lo + LANES:lo + 2 * LANES]
            p_ref[:, lo:lo + LANES] = ((x1 * c - x2 * s) * scale).astype(BF16)
            p_ref[:, lo + LANES:lo + 2 * LANES] = ((x1 * s + x2 * c) * scale).astype(BF16)

    def diff_norm_rope(g_ref, scale, out_ref):
        c = dc_ref[...]
        sn = dsn_ref[...]
        sp = dsp_ref[...]
        g = g_ref[...]
        for grp in range(COL_BLOCK // LANES):
            xg = acc[:, grp * LANES:(grp + 1) * LANES]
            ms = _dot((xg * xg).astype(BF16), bd_ref[...])
            y = xg * lax.rsqrt(ms + EPS) * g
            o = y * c + pltpu.roll(y, LANES - DIFF_DH // 2, 1) * sn + pltpu.roll(y, DIFF_DH // 2, 1) * sp
            out_ref[:, grp * LANES:(grp + 1) * LANES] = (o * scale).astype(out_ref.dtype)

    @pl.when(j == 0)
    def _():
        ret_rope(1.0)

    @pl.when(j == 1)
    def _():
        ret_rope(RET_DK ** -0.5)

    @pl.when((j == 2) | (j == 3))
    def _():
        p_ref[...] = acc.astype(BF16)

    @pl.when((j == 4) | (j == 5))
    def _():
        p_ref[...] = (acc * jax.nn.sigmoid(acc)).astype(BF16)

    @pl.when(j == 6)
    def _():
        diff_norm_rope(gq_ref, DIFF_DH ** -0.5, p_ref)

    @pl.when((j == 7) | (j == 8))
    def _():
        p_ref[...] = jax.nn.sigmoid(acc).astype(BF16)

    @pl.when(j == 9)
    def _():
        diff_norm_rope(gk_ref, 1.0, k_ref)

    @pl.when(j == 10)
    def _():
        v_ref[...] = acc


def _mod_spec(arr, tm, rows_per_group, ngrid):
    d = arr.shape[-1]
    if arr.ndim == 3:
        tiles_per_group = rows_per_group // tm
        if ngrid == 2:
            return pl.BlockSpec((None, 1, d), lambda i, j: (i // tiles_per_group, 0, 0))
        return pl.BlockSpec((None, 1, d), lambda i: (i // tiles_per_group, 0, 0))
    if ngrid == 2:
        return pl.BlockSpec((tm, d), lambda i, j: (i, 0))
    return pl.BlockSpec((tm, d), lambda i: (i, 0))


def _inproj(x2d, sc1, sh1, g1, w_bf, tabs, gq, gk, bd, tm, rows_per_group):
    t, d = x2d.shape
    rc, rs, dc, dsn, dsp = tabs
    nrep = rc.shape[0] // tm

    def w_map(i, j):
        return (0, jnp.where(j < 7, j, jnp.where(j < 9, j + 2, j - 2)))

    tab_spec = pl.BlockSpec((tm, LANES), lambda i, j: (i % nrep, 0))
    vec_spec = pl.BlockSpec((1, LANES), lambda i, j: (0, 0))
    return pl.pallas_call(
        _inproj_kernel,
        out_shape=(jax.ShapeDtypeStruct((t, P_COLS), BF16),
                   jax.ShapeDtypeStruct((t, d), F32),
                   jax.ShapeDtypeStruct((t, d), F32)),
        grid=(t // tm, N_COL_BLOCKS),
        in_specs=[pl.BlockSpec((tm, d), lambda i, j: (i, 0)),
                  _mod_spec(sc1, tm, rows_per_group, 2),
                  _mod_spec(sh1, tm, rows_per_group, 2),
                  pl.BlockSpec((1, d), lambda i, j: (0, 0)),
                  pl.BlockSpec((d, COL_BLOCK), w_map),
                  tab_spec, tab_spec, tab_spec, tab_spec, tab_spec,
                  vec_spec, vec_spec,
                  pl.BlockSpec((LANES, LANES), lambda i, j: (0, 0))],
        out_specs=(pl.BlockSpec((tm, COL_BLOCK), lambda i, j: (i, jnp.minimum(j, 8))),
                   pl.BlockSpec((tm, d), lambda i, j: (i, 0)),
                   pl.BlockSpec((tm, d), lambda i, j: (i, 0))),
        scratch_shapes=[pltpu.VMEM((tm, d), BF16)],
        compiler_params=_cparams("arbitrary", "arbitrary"),
        name="inproj",
    )(x2d, sc1, sh1, g1, w_bf, rc, rs, dc, dsn, dsp, gq, gk, bd)


def _retention_kernel(*refs, has_state):
    if has_state:
        q_ref, k_ref, v_ref, rg_ref, dm_ref, dv_ref, s0_ref, o_ref, sout_ref, s_scr = refs
    else:
        q_ref, k_ref, v_ref, rg_ref, dm_ref, dv_ref, o_ref, sout_ref, s_scr = refs
        s0_ref = None
    c = pl.program_id(2)

    @pl.when(c == 0)
    def _():
        if has_state:
            s_scr[...] = s0_ref[...]
        else:
            s_scr[...] = jnp.zeros_like(s_scr)

    q = q_ref[...]
    k = k_ref[...]
    v = v_ref[...]
    s = s_scr[...]
    dvec = dv_ref[...]
    q_decay = dvec[:, 0:1]
    k_decay = dvec[:, 1:2]
    chunk_decay = dvec[0:1, 2:3]

    scores = (_dot_nt(q, k) * dm_ref[...]).astype(BF16)
    o = _dot(scores, v) + _dot(q, s.astype(BF16)) * q_decay
    o = _rms(o) * rg_ref[...].astype(F32)
    o_ref[...] = o.astype(BF16)

    kd_t = (k.astype(F32) * k_decay).T.astype(BF16)
    s_new = s * chunk_decay + _dot(kd_t, v)
    s_scr[...] = s_new

    @pl.when(c == pl.num_programs(2) - 1)
    def _():
        sout_ref[...] = s_new


def _retention(p, s0, nb, seq, chunk):
    t = p.shape[0]
    nc = seq // chunk
    idx = jnp.arange(chunk, dtype=F32)
    log_gamma = jnp.log1p(-jnp.exp2(-5.0 - jnp.arange(RET_HEADS, dtype=F32)))
    rel = idx[:, None] - idx[None, :]
    dmat = jnp.where(rel >= 0, jnp.exp(jnp.maximum(rel, 0.0)[None] * log_gamma[:, None, None]), 0.0)
    q_decay = jnp.exp((idx + 1.0)[None, :] * log_gamma[:, None])
    k_decay = jnp.exp((chunk - 1.0 - idx)[None, :] * log_gamma[:, None])
    c_decay = jnp.broadcast_to(jnp.exp(chunk * log_gamma)[:, None], (RET_HEADS, chunk))
    dvec = jnp.zeros((RET_HEADS, chunk, LANES), F32)
    dvec = dvec.at[:, :, 0].set(q_decay).at[:, :, 1].set(k_decay).at[:, :, 2].set(c_decay)

    qb, vb = P_RQ // RET_DK, P_RV // RET_DV
    kb, gb = P_RK // RET_DK, P_RG // RET_DV
    in_specs = [pl.BlockSpec((chunk, RET_DK), lambda b, h, c: (b * nc + c, qb + h)),
                pl.BlockSpec((chunk, RET_DK), lambda b, h, c: (b * nc + c, kb + h)),
                pl.BlockSpec((chunk, RET_DV), lambda b, h, c: (b * nc + c, vb + h)),
                pl.BlockSpec((chunk, RET_DV), lambda b, h, c: (b * nc + c, gb + h)),
                pl.BlockSpec((None, chunk, chunk), lambda b, h, c: (h, 0, 0)),
                pl.BlockSpec((None, chunk, LANES), lambda b, h, c: (h, 0, 0))]
    args = [p, p, p, p, dmat, dvec]
    state_spec = pl.BlockSpec((None, None, RET_DK, RET_DV), lambda b, h, c: (b, h, 0, 0))
    if s0 is not None:
        in_specs.append(state_spec)
        args.append(s0)
    return pl.pallas_call(
        functools.partial(_retention_kernel, has_state=s0 is not None),
        out_shape=(jax.ShapeDtypeStruct((t, RET_HEADS * RET_DV), BF16),
                   jax.ShapeDtypeStruct((nb, RET_HEADS, RET_DK, RET_DV), F32)),
        grid=(nb, RET_HEADS, nc),
        in_specs=in_specs,
        out_specs=(pl.BlockSpec((chunk, RET_DV), lambda b, h, c: (b * nc + c, h)), state_spec),
        scratch_shapes=[pltpu.VMEM((RET_DK, RET_DV), F32)],
        compiler_params=_cparams("arbitrary", "arbitrary", "arbitrary"),
        name="retention",
    )(*args)


def _lambda(lam_ref, lam_init):
    l1 = jnp.sum(lam_ref[0:1, :] * lam_ref[1:2, :], axis=-1, keepdims=True)
    l2 = jnp.sum(lam_ref[2:3, :] * lam_ref[3:4, :], axis=-1, keepdims=True)
    return jnp.exp(l1) - jnp.exp(l2) + lam_init


def _split_maps(q):
    lane = lax.broadcasted_iota(jnp.int32, q.shape, 1)
    zero = jnp.zeros_like(q)
    return jnp.where(lane < DIFF_DH, q, zero), jnp.where(lane >= DIFF_DH, q, zero)


def _diffattn_kernel(q_ref, k_ref, v_ref, lam_ref, g_ref, o_ref, kb_scr, vb_scr, *, tq, lam_init):
    qi = pl.program_id(2)

    @pl.when(qi == 0)
    def _():
        kb_scr[...] = k_ref[...].astype(BF16)
        vb_scr[...] = v_ref[...].astype(BF16)

    qa, qb = _split_maps(q_ref[...])
    row = lax.broadcasted_iota(jnp.int32, (tq, tq), 0)
    col = lax.broadcasted_iota(jnp.int32, (tq, tq), 1)
    visible = (col // CHUNK) <= (row // CHUNK)

    def online(sc, vblk, m, l, a):
        m_new = jnp.maximum(m, jnp.max(sc, axis=-1, keepdims=True))
        p = jnp.exp(sc - m_new)
        alpha = jnp.exp(m - m_new)
        l = alpha * l + jnp.sum(p, axis=-1, keepdims=True)
        a = alpha * a + _dot(p.astype(BF16), vblk)
        return m_new, l, a

    def tile(kt, carry, masked):
        m1, l1, a1, m2, l2, a2 = carry
        off = pl.multiple_of(kt * tq, tq)
        kblk = kb_scr[pl.ds(off, tq), :]
        vblk = vb_scr[pl.ds(off, tq), :]
        s1 = _dot_nt(qa, kblk)
        s2 = _dot_nt(qb, kblk)
        if masked:
            s1 = jnp.where(visible, s1, -1e30)
            s2 = jnp.where(visible, s2, -1e30)
        m1, l1, a1 = online(s1, vblk, m1, l1, a1)
        m2, l2, a2 = online(s2, vblk, m2, l2, a2)
        return m1, l1, a1, m2, l2, a2

    neg = jnp.full((tq, 1), -1e30, F32)
    zl = jnp.zeros((tq, 1), F32)
    za = jnp.zeros((tq, DIFF_DV), F32)
    carry = lax.fori_loop(0, qi, lambda kt, cr: tile(kt, cr, False), (neg, zl, za, neg, zl, za))
    m1, l1, a1, m2, l2, a2 = tile(qi, carry, True)

    o = a1 / l1 - _lambda(lam_ref, lam_init) * (a2 / l2)
    o = _rms(o) * g_ref[...] * (1.0 - lam_init)
    o_ref[...] = o.astype(BF16)


def _diffattn(p, k2d, v2d, lam_vecs, subln_g, nb, seq, lam_init, tq):
    t = p.shape[0]
    nq = seq // tq
    qcol = P_DQ // DIFF_DV
    return pl.pallas_call(
        functools.partial(_diffattn_kernel, tq=tq, lam_init=lam_init),
        out_shape=jax.ShapeDtypeStruct((t, DIFF_HEADS * DIFF_DV), BF16),
        grid=(nb, DIFF_HEADS, nq),
        in_specs=[pl.BlockSpec((tq, DIFF_DV), lambda b, h, i: (b * nq + i, qcol + h)),
                  pl.BlockSpec((seq, DIFF_DV), lambda b, h, i: (b, h)),
                  pl.BlockSpec((seq, DIFF_DV), lambda b, h, i: (b, h)),
                  pl.BlockSpec((4, DIFF_DH), lambda b, h, i: (0, 0)),
                  pl.BlockSpec((1, DIFF_DV), lambda b, h, i: (0, 0))],
        out_specs=pl.BlockSpec((tq, DIFF_DV), lambda b, h, i: (b * nq + i, h)),
        scratch_shapes=[pltpu.VMEM((seq, DIFF_DV), BF16), pltpu.VMEM((seq, DIFF_DV), BF16)],
        compiler_params=_cparams("arbitrary", "arbitrary", "arbitrary"),
        name="diffattn",
    )(p, k2d, v2d, lam_vecs, subln_g)


def _diffattn_cached_kernel(q_ref, kc_ref, vc_ref, kn_ref, vn_ref, lam_ref, g_ref, o_ref, k_scr, v_scr,
                            *, past, new, lam_init):
    total = k_scr.shape[0]
    k_scr[0:past, :] = kc_ref[...].astype(BF16)
    v_scr[0:past, :] = vc_ref[...].astype(BF16)
    pad = jnp.zeros((total - past, DIFF_DV), BF16)
    k_scr[past:total, :] = pad
    v_scr[past:total, :] = pad
    k_scr[past:past + new, :] = kn_ref[...].astype(BF16)
    v_scr[past:past + new, :] = vn_ref[...].astype(BF16)

    qa, qb = _split_maps(q_ref[...])
    qpos = past + lax.broadcasted_iota(jnp.int32, (new, total), 0)
    kpos = lax.broadcasted_iota(jnp.int32, (new, total), 1)
    visible = ((kpos // CHUNK) <= (qpos // CHUNK)) & (kpos < past + new)
    kall = k_scr[...]
    vall = v_scr[...]

    def softmax_av(qm):
        sc = jnp.where(visible, _dot_nt(qm, kall), -1e30)
        m = jnp.max(sc, axis=-1, keepdims=True)
        p = jnp.exp(sc - m)
        return _dot(p.astype(BF16), vall) / jnp.sum(p, axis=-1, keepdims=True)

    o = softmax_av(qa) - _lambda(lam_ref, lam_init) * softmax_av(qb)
    o = _rms(o) * g_ref[...] * (1.0 - lam_init)
    o_ref[...] = o.astype(BF16)


def _diffattn_cached(p, k2d, v2d, kc, vc, lam_vecs, subln_g, nb, new, lam_init):
    t = p.shape[0]
    past = kc.shape[1]
    total = past + LANES * pl.cdiv(new, LANES)
    qcol = P_DQ // DIFF_DV
    return pl.pallas_call(
        functools.partial(_diffattn_cached_kernel, past=past, new=new, lam_init=lam_init),
        out_shape=jax.ShapeDtypeStruct((t, DIFF_HEADS * DIFF_DV), BF16),
        grid=(nb, DIFF_HEADS),
        in_specs=[pl.BlockSpec((new, DIFF_DV), lambda b, h: (b, qcol + h)),
                  pl.BlockSpec((None, past, DIFF_DV), lambda b, h: (b, 0, h)),
                  pl.BlockSpec((None, past, DIFF_DV), lambda b, h: (b, 0, h)),
                  pl.BlockSpec((new, DIFF_DV), lambda b, h: (b, h)),
                  pl.BlockSpec((new, DIFF_DV), lambda b, h: (b, h)),
                  pl.BlockSpec((4, DIFF_DH), lambda b, h: (0, 0)),
                  pl.BlockSpec((1, DIFF_DV), lambda b, h: (0, 0))],
        out_specs=pl.BlockSpec((new, DIFF_DV), lambda b, h: (b, h)),
        scratch_shapes=[pltpu.VMEM((total, DIFF_DV), BF16), pltpu.VMEM((total, DIFF_DV), BF16)],
        compiler_params=_cparams("arbitrary", "arbitrary"),
        name="diffattn_cached",
    )(p, kc, vc, k2d, v2d, lam_vecs, subln_g)


def _merge_kernel(ret_ref, dif_ref, gr_ref, gd_ref, x_ref, gt1_ref, sc2_ref, sh2_ref, g2_ref,
                  wr_ref, wd_ref, wo_ref, rwh_ref, rwl_ref, rb_ref, x1_ref, h2_ref, cmb_ref):
    a = _dot(ret_ref[...], wr_ref[...])
    b = _dot(dif_ref[...], wd_ref[...])
    merged = gr_ref[...].astype(F32) * a + gd_ref[...].astype(F32) * b
    x1 = x_ref[...] + gt1_ref[...] * _dot(merged.astype(BF16), wo_ref[...])
    x1_ref[...] = x1
    h2 = _rms(x1) * g2_ref[...]
    h2 = h2 * (1.0 + sc2_ref[...]) + sh2_ref[...]
    h2_ref[...] = h2.astype(BF16)

    hi = h2.astype(BF16)
    lo = (h2 - hi.astype(F32)).astype(BF16)
    logits = _dot(hi, rwh_ref[...]) + _dot(lo, rwh_ref[...]) + _dot(hi, rwl_ref[...]) + rb_ref[...]

    lane = lax.broadcasted_iota(jnp.int32, logits.shape, 1)
    big = jnp.int32(4 * LANES)

    def masked_argmax(mask):
        vmax = jnp.max(jnp.where(mask, logits, -jnp.inf), axis=-1, keepdims=True)
        idx = jnp.min(jnp.where(mask & (logits == vmax), lane, big), axis=-1, keepdims=True)
        return vmax, idx

    is_group = lane < N_GROUPS
    gmax, gidx = masked_argmax(is_group)
    gprob = 1.0 / jnp.sum(jnp.where(is_group, jnp.exp(logits - gmax), 0.0), axis=-1, keepdims=True)
    first = N_GROUPS + gidx * EXPERTS_PER_GROUP
    in_group = (lane >= first) & (lane < first + EXPERTS_PER_GROUP)
    v1, i1 = masked_argmax(in_group)
    v2, i2 = masked_argmax(in_group & (lane != i1))
    e2 = jnp.exp(v2 - v1)
    w1 = gprob / (1.0 + e2)
    w2 = gprob * e2 / (1.0 + e2)
    cmb_ref[...] = (jnp.where(lane + N_GROUPS == i1, w1, 0.0)
                    + jnp.where(lane + N_GROUPS == i2, w2, 0.0))


def _merge(ret_o, dif_o, p, x2d, gt1, sc2, sh2, g2, wr, wd, wo, rwh, rwl, rb, tm, rows_per_group):
    t, d = x2d.shape
    gcol = P_GATE // d
    const = lambda shape: pl.BlockSpec(shape, lambda i: (0, 0), pipeline_mode=pl.Buffered(1))
    return pl.pallas_call(
        _merge_kernel,
        out_shape=(jax.ShapeDtypeStruct((t, d), F32),
                   jax.ShapeDtypeStruct((t, d), BF16),
                   jax.ShapeDtypeStruct((t, LANES), F32)),
        grid=(t // tm,),
        in_specs=[pl.BlockSpec((tm, ret_o.shape[1]), lambda i: (i, 0)),
                  pl.BlockSpec((tm, d), lambda i: (i, 0)),
                  pl.BlockSpec((tm, d), lambda i: (i, gcol)),
                  pl.BlockSpec((tm, d), lambda i: (i, gcol + 1)),
                  pl.BlockSpec((tm, d), lambda i: (i, 0)),
                  _mod_spec(gt1, tm, rows_per_group, 1),
                  _mod_spec(sc2, tm, rows_per_group, 1),
                  _mod_spec(sh2, tm, rows_per_group, 1),
                  const((1, d)),
                  const(wr.shape), const(wd.shape), const(wo.shape),
                  const(rwh.shape), const(rwl.shape), const(rb.shape)],
        out_specs=(pl.BlockSpec((tm, d), lambda i: (i, 0)),
                   pl.BlockSpec((tm, d), lambda i: (i, 0)),
                   pl.BlockSpec((tm, LANES), lambda i: (i, 0))),
        compiler_params=_cparams("arbitrary"),
        name="merge",
    )(ret_o, dif_o, p, p, x2d, gt1, sc2, sh2, g2, wr, wd, wo, rwh, rwl, rb)


def _moe_kernel(h_ref, cmb_ref, x1_ref, gt2_ref, w1_ref, w3_ref, w2_ref, y_ref, acc_scr):
    e = pl.program_id(1)

    @pl.when(e == 0)
    def _():
        acc_scr[...] = jnp.zeros_like(acc_scr)

    h = h_ref[...]
    a = _dot(h, w1_ref[...])
    g = (a * jax.nn.sigmoid(a)) * _dot(h, w3_ref[...])
    ff = _dot(g.astype(BF16), w2_ref[...])
    cmb = cmb_ref[...]
    lane = lax.broadcasted_iota(jnp.int32, cmb.shape, 1)
    w = jnp.sum(jnp.where(lane == e, cmb, 0.0), axis=-1, keepdims=True)
    acc_scr[...] += w * ff

    @pl.when(e == pl.num_programs(1) - 1)
    def _():
        y_ref[...] = x1_ref[...] + gt2_ref[...] * acc_scr[...]


def _moe(h2, cmb, x1, gt2, w1, w3, w2, tm, rows_per_group):
    t, d = x1.shape
    ne, _, de = w1.shape
    gt2_spec = (pl.BlockSpec((None, 1, d), lambda i, e: (i // (rows_per_group // tm), 0, 0))
                if gt2.ndim == 3 else pl.BlockSpec((tm, d), lambda i, e: (i, 0)))
    return pl.pallas_call(
        _moe_kernel,
        out_shape=jax.ShapeDtypeStruct((t, d), F32),
        grid=(t // tm, ne),
        in_specs=[pl.BlockSpec((tm, d), lambda i, e: (i, 0)),
                  pl.BlockSpec((tm, LANES), lambda i, e: (i, 0)),
                  pl.BlockSpec((tm, d), lambda i, e: (i, 0)),
                  gt2_spec,
                  pl.BlockSpec((None, d, de), lambda i, e: (e, 0, 0)),
                  pl.BlockSpec((None, d, de), lambda i, e: (e, 0, 0)),
                  pl.BlockSpec((None, de, d), lambda i, e: (e, 0, 0))],
        out_specs=pl.BlockSpec((tm, d), lambda i, e: (i, 0)),
        scratch_shapes=[pltpu.VMEM((tm, d), F32)],
        compiler_params=_cparams("arbitrary", "arbitrary"),
        name="moe",
    )(h2, cmb, x1, gt2, w1, w3, w2)


def _rope_tables(pos):
    posf = pos.astype(F32)[:, None]
    inv_r = ROPE_THETA ** (-jnp.arange(0, RET_DK, 2, dtype=F32) / RET_DK)
    ang_r = posf * inv_r[None, :]
    inv_d = ROPE_THETA ** (-jnp.arange(0, DIFF_DH, 2, dtype=F32) / DIFF_DH)
    lane = jnp.arange(LANES)
    ang_d = posf * inv_d[lane % (DIFF_DH // 2)][None, :]
    first_half = ((lane % DIFF_DH) < DIFF_DH // 2)[None, :]
    sin_d = jnp.sin(ang_d)
    return (jnp.cos(ang_r), jnp.sin(ang_r), jnp.cos(ang_d),
            jnp.where(first_half, -sin_d, 0.0), jnp.where(first_half, 0.0, sin_d))


def _tile_rows(n, target):
    tm = min(n, target)
    assert n % tm == 0
    return tm


def _layer(x, mod, pos, s0, kc, vc, lw, li, ret_chunk):
    nb, seq, d = x.shape
    t = nb * seq
    x2d = x.reshape(t, d)
    per_token = seq < 128
    if per_token:
        mods = [jnp.repeat(mod[:, k, :], seq, axis=0) for k in range(6)]
        tabs = tuple(jnp.tile(tb, (nb, 1)) for tb in _rope_tables(pos))
        rows_per_group = t
    else:
        mods = [mod[:, k:k + 1, :] for k in range(6)]
        tabs = _rope_tables(pos)
        rows_per_group = seq
    sh1, sc1, gt1, sh2, sc2, gt2 = mods
    tm = _tile_rows(rows_per_group, 512)

    p, k2d, v2d = _inproj(x2d, sc1, sh1, lw['norm1_g'], lw['w_in'], tabs, lw['gq'], lw['gk'], lw['bd'],
                          tm, rows_per_group)
    ret_o, s_new = _retention(p, s0, nb, seq, ret_chunk)
    lam_init = 0.8 - 0.6 * math.exp(-0.3 * li)
    if kc is None:
        dif_o = _diffattn(p, k2d, v2d, lw['lam'], lw['subln_g'], nb, seq, lam_init, tq=256)
    else:
        dif_o = _diffattn_cached(p, k2d, v2d, kc, vc, lw['lam'], lw['subln_g'], nb, seq, lam_init)
    x1, h2, cmb = _merge(ret_o, dif_o, p, x2d, gt1, sc2, sh2, lw['norm2_g'], lw['w_br_ret'], lw['w_br_diff'],
                         lw['w_out'], lw['rw_hi'], lw['rw_lo'], lw['rb'], tm, rows_per_group)
    y = _moe(h2, cmb, x1, gt2, lw['w1'], lw['w3'], lw['w2'], tm, rows_per_group)
    return (y.reshape(nb, seq, d), s_new, k2d.reshape(nb, seq, DIFF_HEADS, 2 * DIFF_DH),
            v2d.reshape(nb, seq, DIFF_HEADS, DIFF_DV))


def kernel(x_prompt, x_sample, c_prompt, c_sample, cache_diff_k, cache_diff_v, state_ret, w_ada, b_ada, norm1_g, w_in, diff_qnorm_g, diff_knorm_g, diff_lambda_q1, diff_lambda_k1, diff_lambda_q2, diff_lambda_k2, diff_subln_g, w_br_ret, w_br_diff, w_out, norm2_g, w_group, b_group, w_expert_router, b_expert, w1, w3, w2):
    nb, seq, d = x_prompt.shape
    nbs, seqs, _ = x_sample.shape
    depth = w_in.shape[0]
    past = cache_diff_k.shape[2]
    pos_p = jnp.arange(seq, dtype=jnp.int32)
    pos_s = past + jnp.arange(seqs, dtype=jnp.int32)
    c_all = jnp.concatenate([c_prompt, c_sample], axis=0)
    lane = jnp.arange(LANES)
    bd = jnp.where((lane[:, None] // DIFF_DH) == (lane[None, :] // DIFF_DH), 1.0 / DIFF_DH, 0.0).astype(BF16)

    xp, xs = x_prompt, x_sample
    outs = [[] for _ in range(6)]
    for li in range(depth):
        rw = jnp.zeros((d, LANES), F32)
        rw = rw.at[:, :N_GROUPS].set(w_group[li]).at[:, N_GROUPS:N_GROUPS + N_EXPERTS].set(w_expert_router[li])
        rw_hi = rw.astype(BF16)
        rb = jnp.zeros((1, LANES), F32)
        rb = rb.at[0, :N_GROUPS].set(b_group[li]).at[0, N_GROUPS:N_GROUPS + N_EXPERTS].set(b_expert[li])
        lw = {
            'norm1_g': norm1_g[li].reshape(1, d), 'norm2_g': norm2_g[li].reshape(1, d),
            'w_in': w_in[li].astype(BF16),
            'gq': jnp.tile(diff_qnorm_g[li], LANES // DIFF_DH).reshape(1, LANES),
            'gk': jnp.tile(diff_knorm_g[li], LANES // DIFF_DH).reshape(1, LANES),
            'bd': bd,
            'lam': jnp.stack([diff_lambda_q1[li], diff_lambda_k1[li], diff_lambda_q2[li], diff_lambda_k2[li]]),
            'subln_g': diff_subln_g[li].reshape(1, DIFF_DV),
            'w_br_ret': w_br_ret[li].astype(BF16), 'w_br_diff': w_br_diff[li].astype(BF16),
            'w_out': w_out[li].astype(BF16),
            'rw_hi': rw_hi, 'rw_lo': (rw - rw_hi.astype(F32)).astype(BF16), 'rb': rb,
            'w1': w1[li].astype(BF16), 'w3': w3[li].astype(BF16), 'w2': w2[li].astype(BF16),
        }
        mod = _adaln(c_all, w_ada[li], b_ada[li]).reshape(nb + nbs, 6, d)
        xp, sp, kp, vp = _layer(xp, mod[:nb], pos_p, None, None, None, lw, li, ret_chunk=min(seq, 256))
        xs, ss, kss, vss = _layer(xs, mod[nb:], pos_s, state_ret[li],
                                  cache_diff_k[li].reshape(nbs, past, DIFF_HEADS * 2 * DIFF_DH),
                                  cache_diff_v[li].reshape(nbs, past, DIFF_HEADS * DIFF_DV),
                                  lw, li, ret_chunk=seqs)
        for lst, val in zip(outs, (kp, vp, sp, kss, vss, ss)):
            lst.append(val)
    kp, vp, sp, kss, vss, ss = (jnp.stack(lst) for lst in outs)
    return (xp, xs, kp, vp, sp.astype(x_prompt.dtype), kss, vss, ss.astype(state_ret.dtype))
```

```python
import functools
import math

import jax
import jax.numpy as jnp
from jax import lax
from jax.experimental import pallas as pl
from jax.experimental.pallas import tpu as pltpu

F32 = jnp.float32
BF16 = jnp.bfloat16

D_MODEL = 1024
CHUNK = 64
ROPE_THETA = 10000.0
EPS = 1e-6
RET_HEADS = 4
RET_DK = D_MODEL // RET_HEADS
RET_DV = 2 * RET_DK
DIFF_HEADS = 8
DIFF_DH = D_MODEL // (2 * DIFF_HEADS)
DIFF_DV = 2 * DIFF_DH
N_GROUPS = 4
EXPERTS_PER_GROUP = 4
N_EXPERTS = N_GROUPS * EXPERTS_PER_GROUP
D_EXPERT = 512
N_PAIRS = EXPERTS_PER_GROUP * (EXPERTS_PER_GROUP - 1) // 2
N_CLASSES = N_GROUPS * N_PAIRS
MOE_TILE_ROWS = 512

LANES = 128
SUBLANES = 8
MXU_COLS = 256
COL_BLOCK = 1024
N_COL_BLOCKS = 11
P_COLS = 9 * COL_BLOCK
P_RQ, P_RK, P_RV, P_RG, P_DQ, P_GATE = 0, 1024, 2048, 4096, 6144, 7168
VMEM_LIMIT = 56 * 1024 * 1024
LOG2E = math.log2(math.e)


def _cparams(*sem):
    return pltpu.CompilerParams(dimension_semantics=sem, vmem_limit_bytes=VMEM_LIMIT)


def _dot(a, b):
    return jnp.dot(a, b, preferred_element_type=F32)


def _dot_nt(a, b):
    return lax.dot_general(a, b, (((1,), (1,)), ((), ())), preferred_element_type=F32)


def _adaln_kernel(c_ref, w_ref, b_ref, o_ref):
    c = c_ref[...]
    a = c * jax.nn.sigmoid(c)
    o_ref[...] = jnp.dot(a, w_ref[...], preferred_element_type=F32,
                         precision=lax.Precision.HIGHEST) + b_ref[...]


def _adaln(c, w_ada, b_ada):
    rows, d = c.shape
    n = w_ada.shape[1]
    tn = 1536
    return pl.pallas_call(
        _adaln_kernel,
        out_shape=jax.ShapeDtypeStruct((rows, n), F32),
        grid=(n // tn,),
        in_specs=[pl.BlockSpec((rows, d), lambda j: (0, 0)),
                  pl.BlockSpec((d, tn), lambda j: (0, j)),
                  pl.BlockSpec((1, tn), lambda j: (0, j))],
        out_specs=pl.BlockSpec((rows, tn), lambda j: (0, j)),
        compiler_params=_cparams("arbitrary"),
        name="adaln",
    )(c, w_ada, b_ada.reshape(1, n))


def _rms(x):
    return x * lax.rsqrt(jnp.mean(x * x, axis=-1, keepdims=True) + EPS)


def _inproj_kernel(x_ref, sc_ref, sh_ref, g1_ref, w_ref, rc_ref, rs_ref, qc_ref, qsn_ref, qsp_ref,
                   kc_ref, ksn_ref, ksp_ref, bd_ref, p_ref, k_ref, v_ref, h_scr):
    j = pl.program_id(1)

    @pl.when(j == 0)
    def _():
        h = _rms(x_ref[...]) * g1_ref[...]
        h = h * (1.0 + sc_ref[...]) + sh_ref[...]
        h_scr[...] = h.astype(BF16)

    def strips():
        h = h_scr[...]
        for lo in range(0, COL_BLOCK, MXU_COLS):
            yield lo, _dot(h, w_ref[:, lo:lo + MXU_COLS])

    def ret_rope(scale):
        c = rc_ref[...]
        s = rs_ref[...]
        for lo, acc in strips():
            x1 = acc[:, 0:LANES]
            x2 = acc[:, LANES:2 * LANES]
            p_ref[:, lo:lo + LANES] = ((x1 * c - x2 * s) * scale).astype(BF16)
            p_ref[:, lo + LANES:lo + 2 * LANES] = ((x1 * s + x2 * c) * scale).astype(BF16)

    def diff_norm_rope(c_ref, sn_ref, sp_ref, out_ref):
        c = c_ref[...]
        sn = sn_ref[...]
        sp = sp_ref[...]
        for lo, acc in list(strips()):
            y = acc * lax.rsqrt(_dot((acc * acc).astype(BF16), bd_ref[...]) + EPS)
            for off in range(0, MXU_COLS, LANES):
                yg = y[:, off:off + LANES]
                o = yg * c + pltpu.roll(yg, LANES - DIFF_DH // 2, 1) * sn + pltpu.roll(yg, DIFF_DH // 2, 1) * sp
                out_ref[:, lo + off:lo + off + LANES] = o.astype(out_ref.dtype)

    def elementwise(fn, out_ref):
        for lo, acc in strips():
            out_ref[:, lo:lo + MXU_COLS] = fn(acc).astype(out_ref.dtype)

    @pl.when(j == 0)
    def _():
        ret_rope(1.0)

    @pl.when(j == 1)
    def _():
        ret_rope(RET_DK ** -0.5)

    @pl.when((j == 2) | (j == 3))
    def _():
        elementwise(lambda a: a, p_ref)

    @pl.when((j == 4) | (j == 5))
    def _():
        elementwise(lambda a: a * jax.nn.sigmoid(a), p_ref)

    @pl.when(j == 6)
    def _():
        diff_norm_rope(qc_ref, qsn_ref, qsp_ref, p_ref)

    @pl.when((j == 7) | (j == 8))
    def _():
        elementwise(jax.nn.sigmoid, p_ref)

    @pl.when(j == 9)
    def _():
        diff_norm_rope(kc_ref, ksn_ref, ksp_ref, k_ref)

    @pl.when(j == 10)
    def _():
        elementwise(lambda a: a, v_ref)


def _mod_spec(arr, tm, rows_per_group, ngrid):
    d = arr.shape[-1]
    if arr.ndim == 3:
        tiles_per_group = rows_per_group // tm
        if ngrid == 2:
            return pl.BlockSpec((None, 1, d), lambda i, j: (i // tiles_per_group, 0, 0))
        return pl.BlockSpec((None, 1, d), lambda i: (i // tiles_per_group, 0, 0))
    if ngrid == 2:
        return pl.BlockSpec((tm, d), lambda i, j: (i, 0))
    return pl.BlockSpec((tm, d), lambda i: (i, 0))


def _inproj(x2d, sc1, sh1, g1, w_bf, tabs, bd, tm, rows_per_group):
    t, d = x2d.shape
    nrep = tabs[0].shape[0] // tm

    def w_map(i, j):
        return (0, jnp.where(j < 7, j, jnp.where(j < 9, j + 2, j - 2)))

    tab_spec = pl.BlockSpec((tm, LANES), lambda i, j: (i % nrep, 0))
    return pl.pallas_call(
        _inproj_kernel,
        out_shape=(jax.ShapeDtypeStruct((t, P_COLS), BF16),
                   jax.ShapeDtypeStruct((t, d), F32),
                   jax.ShapeDtypeStruct((t, d), F32)),
        grid=(t // tm, N_COL_BLOCKS),
        in_specs=[pl.BlockSpec((tm, d), lambda i, j: (i, 0)),
                  _mod_spec(sc1, tm, rows_per_group, 2),
                  _mod_spec(sh1, tm, rows_per_group, 2),
                  pl.BlockSpec((1, d), lambda i, j: (0, 0)),
                  pl.BlockSpec((d, COL_BLOCK), w_map),
                  *([tab_spec] * len(tabs)),
                  pl.BlockSpec((MXU_COLS, MXU_COLS), lambda i, j: (0, 0))],
        out_specs=(pl.BlockSpec((tm, COL_BLOCK), lambda i, j: (i, jnp.minimum(j, 8))),
                   pl.BlockSpec((tm, d), lambda i, j: (i, 0)),
                   pl.BlockSpec((tm, d), lambda i, j: (i, 0))),
        scratch_shapes=[pltpu.VMEM((tm, d), BF16)],
        compiler_params=_cparams("arbitrary", "arbitrary"),
        name="inproj",
    )(x2d, sc1, sh1, g1, w_bf, *tabs, bd)


def _retention_kernel(*refs, has_state):
    if has_state:
        q_ref, k_ref, v_ref, rg_ref, dm_ref, dv_ref, s0_ref, o_ref, sout_ref, s_scr = refs
    else:
        q_ref, k_ref, v_ref, rg_ref, dm_ref, dv_ref, o_ref, sout_ref, s_scr = refs
        s0_ref = None
    c = pl.program_id(2)

    @pl.when(c == 0)
    def _():
        if has_state:
            s_scr[...] = s0_ref[...]
        else:
            s_scr[...] = jnp.zeros_like(s_scr)

    q = q_ref[...]
    k = k_ref[...]
    v = v_ref[...]
    s = s_scr[...]
    dvec = dv_ref[...]
    q_decay = dvec[:, 0:1]
    k_decay = dvec[:, 1:2]
    chunk_decay = dvec[0:1, 2:3]

    scores = (_dot_nt(q, k) * dm_ref[...]).astype(BF16)
    o = _dot(scores, v) + _dot(q, s.astype(BF16)) * q_decay
    o = _rms(o) * rg_ref[...].astype(F32)
    o_ref[...] = o.astype(BF16)

    kd_t = (k.astype(F32) * k_decay).T.astype(BF16)
    s_new = s * chunk_decay + _dot(kd_t, v)
    s_scr[...] = s_new

    @pl.when(c == pl.num_programs(2) - 1)
    def _():
        sout_ref[...] = s_new


def _retention(p, s0, nb, seq, chunk):
    t = p.shape[0]
    nc = seq // chunk
    idx = jnp.arange(chunk, dtype=F32)
    log_gamma = jnp.log1p(-jnp.exp2(-5.0 - jnp.arange(RET_HEADS, dtype=F32)))
    rel = idx[:, None] - idx[None, :]
    dmat = jnp.where(rel >= 0, jnp.exp(jnp.maximum(rel, 0.0)[None] * log_gamma[:, None, None]), 0.0)
    q_decay = jnp.exp((idx + 1.0)[None, :] * log_gamma[:, None])
    k_decay = jnp.exp((chunk - 1.0 - idx)[None, :] * log_gamma[:, None])
    c_decay = jnp.broadcast_to(jnp.exp(chunk * log_gamma)[:, None], (RET_HEADS, chunk))
    dvec = jnp.zeros((RET_HEADS, chunk, LANES), F32)
    dvec = dvec.at[:, :, 0].set(q_decay).at[:, :, 1].set(k_decay).at[:, :, 2].set(c_decay)

    qb, vb = P_RQ // RET_DK, P_RV // RET_DV
    kb, gb = P_RK // RET_DK, P_RG // RET_DV
    in_specs = [pl.BlockSpec((chunk, RET_DK), lambda b, h, c: (b * nc + c, qb + h)),
                pl.BlockSpec((chunk, RET_DK), lambda b, h, c: (b * nc + c, kb + h)),
                pl.BlockSpec((chunk, RET_DV), lambda b, h, c: (b * nc + c, vb + h)),
                pl.BlockSpec((chunk, RET_DV), lambda b, h, c: (b * nc + c, gb + h)),
                pl.BlockSpec((None, chunk, chunk), lambda b, h, c: (h, 0, 0)),
                pl.BlockSpec((None, chunk, LANES), lambda b, h, c: (h, 0, 0))]
    args = [p, p, p, p, dmat, dvec]
    state_spec = pl.BlockSpec((None, None, RET_DK, RET_DV), lambda b, h, c: (b, h, 0, 0))
    if s0 is not None:
        in_specs.append(state_spec)
        args.append(s0)
    return pl.pallas_call(
        functools.partial(_retention_kernel, has_state=s0 is not None),
        out_shape=(jax.ShapeDtypeStruct((t, RET_HEADS * RET_DV), BF16),
                   jax.ShapeDtypeStruct((nb, RET_HEADS, RET_DK, RET_DV), F32)),
        grid=(nb, RET_HEADS, nc),
        in_specs=in_specs,
        out_specs=(pl.BlockSpec((chunk, RET_DV), lambda b, h, c: (b * nc + c, h)), state_spec),
        scratch_shapes=[pltpu.VMEM((RET_DK, RET_DV), F32)],
        compiler_params=_cparams("arbitrary", "arbitrary", "arbitrary"),
        name="retention",
    )(*args)


def _lambda(lam_ref, lam_init):
    l1 = jnp.sum(lam_ref[0:1, :] * lam_ref[1:2, :], axis=-1, keepdims=True)
    l2 = jnp.sum(lam_ref[2:3, :] * lam_ref[3:4, :], axis=-1, keepdims=True)
    return jnp.exp(l1) - jnp.exp(l2) + lam_init


def _split_maps(q):
    lane = lax.broadcasted_iota(jnp.int32, q.shape, 1)
    zero = jnp.zeros_like(q)
    return jnp.where(lane < DIFF_DH, q, zero), jnp.where(lane >= DIFF_DH, q, zero)


def _diffattn_kernel(q_ref, k_ref, v_ref, lam_ref, g_ref, o_ref, kb_scr, vb_scr, m0_scr, m1_scr, acc0_scr, acc1_scr,
                     *, tq, lam_init):
    qi = pl.program_id(2)

    @pl.when(qi == 0)
    def _():
        kb_scr[...] = k_ref[...].astype(BF16)
        vb_scr[:, 0:DIFF_DV] = v_ref[...].astype(BF16)
        vb_scr[:, DIFF_DV:2 * DIFF_DV] = jnp.ones((vb_scr.shape[0], DIFF_DV), BF16)

    qmaps = _split_maps(q_ref[...])
    m_refs = (m0_scr, m1_scr)
    acc_refs = (acc0_scr, acc1_scr)
    for mp in range(2):
        m_refs[mp][...] = jnp.full(m_refs[mp].shape, -1e30, F32)
        acc_refs[mp][...] = jnp.zeros(acc_refs[mp].shape, F32)

    def tile(kt, masked):
        off = pl.multiple_of(kt * tq, tq)
        kblk = kb_scr[pl.ds(off, tq), :]
        vblk = vb_scr[pl.ds(off, tq), :]
        for mp in range(2):
            sc = _dot_nt(qmaps[mp], kblk)
            if masked:
                row = lax.broadcasted_iota(jnp.int32, (tq, tq), 0)
                col = lax.broadcasted_iota(jnp.int32, (tq, tq), 1)
                sc = jnp.where((col // CHUNK) <= (row // CHUNK), sc, -1e30)
            m_old = m_refs[mp][...]
            m_new = jnp.maximum(m_old, jnp.max(sc, axis=-1, keepdims=True))
            p = jnp.exp2(sc - m_new)
            acc_refs[mp][...] = jnp.exp2(m_old - m_new) * acc_refs[mp][...] + _dot(p.astype(BF16), vblk)
            m_refs[mp][...] = m_new

    def body(kt, carry):
        tile(kt, False)
        return carry

    lax.fori_loop(0, qi, body, 0)
    tile(qi, True)

    a1 = acc0_scr[...]
    a2 = acc1_scr[...]
    o = (a1[:, 0:DIFF_DV] / a1[:, DIFF_DV:2 * DIFF_DV]
         - _lambda(lam_ref, lam_init) * (a2[:, 0:DIFF_DV] / a2[:, DIFF_DV:2 * DIFF_DV]))
    o = _rms(o) * g_ref[...] * (1.0 - lam_init)
    o_ref[...] = o.astype(BF16)


def _diffattn(p, k2d, v2d, lam_vecs, subln_g, nb, seq, lam_init, tq):
    t = p.shape[0]
    nq = seq // tq
    qcol = P_DQ // DIFF_DV
    return pl.pallas_call(
        functools.partial(_diffattn_kernel, tq=tq, lam_init=lam_init),
        out_shape=jax.ShapeDtypeStruct((t, DIFF_HEADS * DIFF_DV), BF16),
        grid=(nb, DIFF_HEADS, nq),
        in_specs=[pl.BlockSpec((tq, DIFF_DV), lambda b, h, i: (b * nq + i, qcol + h)),
                  pl.BlockSpec((seq, DIFF_DV), lambda b, h, i: (b, h)),
                  pl.BlockSpec((seq, DIFF_DV), lambda b, h, i: (b, h)),
                  pl.BlockSpec((4, DIFF_DH), lambda b, h, i: (0, 0)),
                  pl.BlockSpec((1, DIFF_DV), lambda b, h, i: (0, 0))],
        out_specs=pl.BlockSpec((tq, DIFF_DV), lambda b, h, i: (b * nq + i, h)),
        scratch_shapes=[pltpu.VMEM((seq, DIFF_DV), BF16), pltpu.VMEM((seq, 2 * DIFF_DV), BF16),
                        pltpu.VMEM((tq, 1), F32), pltpu.VMEM((tq, 1), F32),
                        pltpu.VMEM((tq, 2 * DIFF_DV), F32), pltpu.VMEM((tq, 2 * DIFF_DV), F32)],
        compiler_params=_cparams("arbitrary", "arbitrary", "arbitrary"),
        name="diffattn",
    )(p, k2d, v2d, lam_vecs, subln_g)


def _diffattn_cached_kernel(q_ref, kc_ref, vc_ref, kn_ref, vn_ref, lam_ref, g_ref, o_ref, k_scr, v_scr,
                            *, past, new, lam_init):
    total = k_scr.shape[0]
    pad = jnp.zeros((total - past, DIFF_DV), BF16)
    qpos = past + lax.broadcasted_iota(jnp.int32, (new, total), 0)
    kpos = lax.broadcasted_iota(jnp.int32, (new, total), 1)
    visible = ((kpos // CHUNK) <= (qpos // CHUNK)) & (kpos < past + new)
    lam = _lambda(lam_ref, lam_init)

    for h in range(DIFF_HEADS):
        cols = slice(h * DIFF_DV, (h + 1) * DIFF_DV)
        k_scr[0:past, :] = kc_ref[:, h, :].astype(BF16)
        v_scr[0:past, :] = vc_ref[:, h, :].astype(BF16)
        k_scr[past:total, :] = pad
        v_scr[past:total, :] = pad
        k_scr[past:past + new, :] = kn_ref[:, cols].astype(BF16)
        v_scr[past:past + new, :] = vn_ref[:, cols].astype(BF16)
        kall = k_scr[...]
        vall = v_scr[...]

        def softmax_av(qm):
            sc = jnp.where(visible, _dot_nt(qm, kall), -1e30)
            m = jnp.max(sc, axis=-1, keepdims=True)
            p = jnp.exp2(sc - m)
            return _dot(p.astype(BF16), vall) / jnp.sum(p, axis=-1, keepdims=True)

        qa, qb = _split_maps(q_ref[:, cols])
        o = softmax_av(qa) - lam * softmax_av(qb)
        o = _rms(o) * g_ref[...] * (1.0 - lam_init)
        o_ref[:, cols] = o.astype(BF16)


def _diffattn_cached(p, k2d, v2d, kc, vc, li, lam_vecs, subln_g, nb, new, lam_init):
    t = p.shape[0]
    past = kc.shape[2]
    width = DIFF_HEADS * DIFF_DV
    total = past + LANES * pl.cdiv(new, LANES)
    cache_spec = pl.BlockSpec((None, None, past, DIFF_HEADS, DIFF_DV), lambda b: (li, b, 0, 0, 0))
    return pl.pallas_call(
        functools.partial(_diffattn_cached_kernel, past=past, new=new, lam_init=lam_init),
        out_shape=jax.ShapeDtypeStruct((t, width), BF16),
        grid=(nb,),
        in_specs=[pl.BlockSpec((new, width), lambda b: (b, P_DQ // width)),
                  cache_spec, cache_spec,
                  pl.BlockSpec((new, width), lambda b: (b, 0)),
                  pl.BlockSpec((new, width), lambda b: (b, 0)),
                  pl.BlockSpec((4, DIFF_DH), lambda b: (0, 0)),
                  pl.BlockSpec((1, DIFF_DV), lambda b: (0, 0))],
        out_specs=pl.BlockSpec((new, width), lambda b: (b, 0)),
        scratch_shapes=[pltpu.VMEM((total, DIFF_DV), BF16), pltpu.VMEM((total, DIFF_DV), BF16)],
        compiler_params=_cparams("arbitrary"),
        name="diffattn_cached",
    )(p, kc, vc, k2d, v2d, lam_vecs, subln_g)


def _merge_kernel(ret_ref, dif_ref, gr_ref, gd_ref, x_ref, gt1_ref, sc2_ref, sh2_ref, g2_ref,
                  wr_ref, wd_ref, wo_ref, rwh_ref, rwl_ref, rb_ref, x1_ref, *out_refs, routed):
    a = _dot(ret_ref[...], wr_ref[...])
    b = _dot(dif_ref[...], wd_ref[...])
    merged = gr_ref[...].astype(F32) * a + gd_ref[...].astype(F32) * b
    x1 = x_ref[...] + gt1_ref[...] * _dot(merged.astype(BF16), wo_ref[...])
    x1_ref[...] = x1
    h2 = _rms(x1) * g2_ref[...]
    h2 = h2 * (1.0 + sc2_ref[...]) + sh2_ref[...]

    hi = h2.astype(BF16)
    lo = (h2 - hi.astype(F32)).astype(BF16)
    logits = _dot(hi, rwh_ref[...]) + _dot(lo, rwh_ref[...]) + _dot(hi, rwl_ref[...]) + rb_ref[...]

    lane = lax.broadcasted_iota(jnp.int32, logits.shape, 1)
    big = jnp.int32(4 * LANES)

    def masked_argmax(mask):
        vmax = jnp.max(jnp.where(mask, logits, -jnp.inf), axis=-1, keepdims=True)
        idx = jnp.min(jnp.where(mask & (logits == vmax), lane, big), axis=-1, keepdims=True)
        return vmax, idx

    is_group = lane < N_GROUPS
    gmax, gidx = masked_argmax(is_group)
    gprob = 1.0 / jnp.sum(jnp.where(is_group, jnp.exp(logits - gmax), 0.0), axis=-1, keepdims=True)
    first = N_GROUPS + gidx * EXPERTS_PER_GROUP
    in_group = (lane >= first) & (lane < first + EXPERTS_PER_GROUP)
    v1, i1 = masked_argmax(in_group)
    v2, i2 = masked_argmax(in_group & (lane != i1))
    e2 = jnp.exp(v2 - v1)
    w1 = gprob / (1.0 + e2)
    w2 = gprob * e2 / (1.0 + e2)
    if not routed:
        h2_ref, cmb_ref = out_refs
        h2_ref[...] = h2.astype(BF16)
        cmb_ref[...] = (jnp.where(lane + N_GROUPS == i1, w1, 0.0)
                        + jnp.where(lane + N_GROUPS == i2, w2, 0.0))
        return

    hx_ref, cls_ref, hist_ref = out_refs
    d = h2.shape[1]
    first_is_lo = i1 < i2
    lo = jnp.minimum(i1, i2) - first
    hi = jnp.maximum(i1, i2) - first
    cls = gidx * N_PAIRS + jnp.right_shift(lo * (7 - lo), 1) + hi - lo - 1
    hx_ref[:, 0:d] = h2
    hx_ref[:, d:d + LANES] = jnp.where(lane == 0, jnp.where(first_is_lo, w1, w2),
                                       jnp.where(lane == 1, jnp.where(first_is_lo, w2, w1), 0.0))
    cls_ref[...] = cls

    @pl.when(pl.program_id(0) == 0)
    def _():
        hist_ref[...] = jnp.zeros_like(hist_ref)

    hist_ref[...] += jnp.sum(jnp.where(lane == cls, 1.0, 0.0), axis=0, keepdims=True)


def _merge(ret_o, dif_o, p, x2d, gt1, sc2, sh2, g2, wr, wd, wo, rwh, rwl, rb, tm, rows_per_group, routed):
    t, d = x2d.shape
    gcol = P_GATE // d
    const = lambda shape: pl.BlockSpec(shape, lambda i: (0, 0), pipeline_mode=pl.Buffered(1))
    if routed:
        out_shape = (jax.ShapeDtypeStruct((t, d), F32),
                     jax.ShapeDtypeStruct((t, d + LANES), F32),
                     jax.ShapeDtypeStruct((t, 1), jnp.int32),
                     jax.ShapeDtypeStruct((1, LANES), F32))
        out_specs = (pl.BlockSpec((tm, d), lambda i: (i, 0)),
                     pl.BlockSpec((tm, d + LANES), lambda i: (i, 0)),
                     pl.BlockSpec((tm, 1), lambda i: (i, 0)),
                     pl.BlockSpec((1, LANES), lambda i: (0, 0)))
    else:
        out_shape = (jax.ShapeDtypeStruct((t, d), F32),
                     jax.ShapeDtypeStruct((t, d), BF16),
                     jax.ShapeDtypeStruct((t, LANES), F32))
        out_specs = (pl.BlockSpec((tm, d), lambda i: (i, 0)),
                     pl.BlockSpec((tm, d), lambda i: (i, 0)),
                     pl.BlockSpec((tm, LANES), lambda i: (i, 0)))
    return pl.pallas_call(
        functools.partial(_merge_kernel, routed=routed),
        out_shape=out_shape,
        grid=(t // tm,),
        in_specs=[pl.BlockSpec((tm, ret_o.shape[1]), lambda i: (i, 0)),
                  pl.BlockSpec((tm, d), lambda i: (i, 0)),
                  pl.BlockSpec((tm, d), lambda i: (i, gcol)),
                  pl.BlockSpec((tm, d), lambda i: (i, gcol + 1)),
                  pl.BlockSpec((tm, d), lambda i: (i, 0)),
                  _mod_spec(gt1, tm, rows_per_group, 1),
                  _mod_spec(sc2, tm, rows_per_group, 1),
                  _mod_spec(sh2, tm, rows_per_group, 1),
                  const((1, d)),
                  const(wr.shape), const(wd.shape), const(wo.shape),
                  const(rwh.shape), const(rwl.shape), const(rb.shape)],
        out_specs=out_specs,
        compiler_params=_cparams("arbitrary"),
        name="merge",
    )(ret_o, dif_o, p, p, x2d, gt1, sc2, sh2, g2, wr, wd, wo, rwh, rwl, rb)


def _route_kernel(cls_ref, hist_ref, tri_ref, pos_ref, tinfo_ref, carry_scr, off_scr, *, tile_rows):
    i = pl.program_id(0)

    def excl_cumsum(v):
        lane8 = lax.broadcasted_iota(jnp.int32, v.shape, 1)
        inc = v
        for sh in (1, 2, 4, 8, 16, 32, 64):
            inc = inc + jnp.where(lane8 >= sh, pltpu.roll(inc, sh, 1), 0.0)
        return inc - v

    @pl.when(i == 0)
    def _():
        cnt = hist_ref[...]
        ntile = jnp.floor((cnt + (tile_rows - 1)) * (1.0 / tile_rows))
        start = excl_cumsum(jnp.broadcast_to(ntile, (SUBLANES, LANES)))[0:1, :]
        off_scr[...] = start * tile_rows
        carry_scr[...] = jnp.zeros_like(carry_scr)
        nt = tinfo_ref.shape[0]
        jrow = lax.broadcasted_iota(jnp.int32, (nt, LANES), 0).astype(F32)
        lane = lax.broadcasted_iota(jnp.int32, (nt, LANES), 1)
        is_cls = lane < N_CLASSES
        tcls = jnp.sum(jnp.where(is_cls & (start <= jrow), 1.0, 0.0), axis=-1, keepdims=True) - 1.0
        pick = lane == tcls.astype(jnp.int32)
        cnt_j = jnp.sum(jnp.where(pick, cnt, 0.0), axis=-1, keepdims=True)
        start_j = jnp.sum(jnp.where(pick, start, 0.0), axis=-1, keepdims=True)
        valid = jnp.clip(cnt_j - (jrow[:, 0:1] - start_j) * tile_rows, 0.0, float(tile_rows)).astype(jnp.int32)
        tc = tcls.astype(jnp.int32)
        grp = sum((tc >= g * N_PAIRS).astype(jnp.int32) for g in range(1, N_GROUPS))
        pair = tc - grp * N_PAIRS
        lo = (pair >= 3).astype(jnp.int32) + (pair >= 5).astype(jnp.int32)
        hi = pair - jnp.right_shift(lo * (7 - lo), 1) + lo + 1
        ea = grp * EXPERTS_PER_GROUP + lo
        eb = grp * EXPERTS_PER_GROUP + hi
        tinfo_ref[...] = jnp.where(lane == 0, ea, jnp.where(lane == 1, eb, jnp.where(lane == 2, valid, 0)))

    cls = cls_ref[...]
    lane = lax.broadcasted_iota(jnp.int32, (cls.shape[0], LANES), 1)
    onehot = lane == cls
    earlier = _dot(tri_ref[...], jnp.where(onehot, 1.0, 0.0).astype(BF16))
    base = carry_scr[...] + off_scr[...]
    pos = jnp.sum(jnp.where(onehot, earlier + base, 0.0), axis=-1, keepdims=True)
    pos_ref[...] = pos.astype(jnp.int32)
    carry_scr[...] += jnp.sum(jnp.where(onehot, 1.0, 0.0), axis=0, keepdims=True)


def _route(cls, hist, tile_rows, n_tiles, tm):
    t = cls.shape[0]
    tri = jnp.tril(jnp.ones((tm, tm), BF16), -1)
    return pl.pallas_call(
        functools.partial(_route_kernel, tile_rows=tile_rows),
        out_shape=(jax.ShapeDtypeStruct((t, 1), jnp.int32),
                   jax.ShapeDtypeStruct((n_tiles, LANES), jnp.int32)),
        grid=(t // tm,),
        in_specs=[pl.BlockSpec((tm, 1), lambda i: (i, 0)),
                  pl.BlockSpec((1, LANES), lambda i: (0, 0)),
                  pl.BlockSpec((tm, tm), lambda i: (0, 0))],
        out_specs=(pl.BlockSpec((tm, 1), lambda i: (i, 0)),
                   pl.BlockSpec((n_tiles, LANES), lambda i: (0, 0))),
        scratch_shapes=[pltpu.VMEM((1, LANES), F32), pltpu.VMEM((1, LANES), F32)],
        compiler_params=_cparams("arbitrary"),
        name="route",
    )(cls, hist, tri)


def _dispatch_kernel(pos_ref, hx_ref, sorted_ref, sem):
    tm = hx_ref.shape[0]

    def row_copy(r):
        return pltpu.make_async_copy(hx_ref.at[pl.ds(r, 1)], sorted_ref.at[pl.ds(pos_ref[0, r], 1)], sem)

    def start(r, carry):
        row_copy(r).start()
        return carry

    def wait(r, carry):
        row_copy(r).wait()
        return carry

    lax.fori_loop(0, tm, start, 0, unroll=8)
    lax.fori_loop(0, tm, wait, 0, unroll=8)


def _dispatch(pos, hx, n_slots, tm):
    t, w = hx.shape
    return pl.pallas_call(
        _dispatch_kernel,
        out_shape=jax.ShapeDtypeStruct((n_slots, w), F32),
        grid=(t // tm,),
        in_specs=[pl.BlockSpec((None, 1, tm), lambda i: (i, 0, 0), memory_space=pltpu.SMEM),
                  pl.BlockSpec((tm, w), lambda i: (i, 0))],
        out_specs=pl.BlockSpec(memory_space=pl.ANY),
        scratch_shapes=[pltpu.SemaphoreType.DMA],
        compiler_params=pltpu.CompilerParams(dimension_semantics=("arbitrary",), vmem_limit_bytes=VMEM_LIMIT,
                                             has_side_effects=True),
        name="dispatch",
    )(pos.reshape(t // tm, 1, tm), hx)


def _moe_routed_kernel(ea_ref, eb_ref, valid_ref, hx_ref, w1a_ref, w3a_ref, w2a_ref, w1b_ref, w3b_ref, w2b_ref, y_ref):
    j = pl.program_id(0)
    nvalid = valid_ref[j]

    @pl.when(nvalid == 0)
    def _():
        y_ref[...] = jnp.zeros_like(y_ref)

    @pl.when(nvalid > 0)
    def _():
        tm, d = y_ref.shape
        row = lax.broadcasted_iota(jnp.int32, (tm, 1), 0)
        real = row < nvalid
        h = jnp.where(real, hx_ref[:, 0:d], 0.0).astype(BF16)
        wts = jnp.where(real, hx_ref[:, d:d + LANES], 0.0)

        def expert(w1_ref, w3_ref, w2_ref):
            a = _dot(h, w1_ref[...])
            g = (a * jax.nn.sigmoid(a)) * _dot(h, w3_ref[...])
            return _dot(g.astype(BF16), w2_ref[...])

        y_ref[...] = (wts[:, 0:1] * expert(w1a_ref, w3a_ref, w2a_ref)
                      + wts[:, 1:2] * expert(w1b_ref, w3b_ref, w2b_ref))


def _moe_routed(hx_sorted, ea, eb, valid, w1, w3, w2, tile_rows):
    n_slots, w = hx_sorted.shape
    ne, d, de = w1.shape
    up = lambda sel: pl.BlockSpec((None, d, de), lambda j, ea, eb, valid: ((ea, eb)[sel][j], 0, 0))
    down = lambda sel: pl.BlockSpec((None, de, d), lambda j, ea, eb, valid: ((ea, eb)[sel][j], 0, 0))
    grid_spec = pltpu.PrefetchScalarGridSpec(
        num_scalar_prefetch=3,
        grid=(n_slots // tile_rows,),
        in_specs=[pl.BlockSpec((tile_rows, w), lambda j, ea, eb, valid: (j, 0)),
                  up(0), up(0), down(0), up(1), up(1), down(1)],
        out_specs=pl.BlockSpec((tile_rows, d), lambda j, ea, eb, valid: (j, 0)),
    )
    return pl.pallas_call(
        _moe_routed_kernel,
        out_shape=jax.ShapeDtypeStruct((n_slots, d), F32),
        grid_spec=grid_spec,
        compiler_params=_cparams("arbitrary"),
        name="moe_routed",
    )(ea, eb, valid, hx_sorted, w1, w3, w2, w1, w3, w2)


def _combine_kernel(pos_ref, ys_ref, x1_ref, gt2_ref, y_ref, buf, sem):
    tm = buf.shape[0]

    def row_copy(r):
        return pltpu.make_async_copy(ys_ref.at[pl.ds(pos_ref[0, r], 1)], buf.at[pl.ds(r, 1)], sem)

    def start(r, carry):
        row_copy(r).start()
        return carry

    def wait(r, carry):
        row_copy(r).wait()
        return carry

    lax.fori_loop(0, tm, start, 0, unroll=8)
    lax.fori_loop(0, tm, wait, 0, unroll=8)
    y_ref[...] = x1_ref[...] + gt2_ref[...] * buf[...]


def _combine(pos, ys, x1, gt2, tm, rows_per_group):
    t, d = x1.shape
    return pl.pallas_call(
        _combine_kernel,
        out_shape=jax.ShapeDtypeStruct((t, d), F32),
        grid=(t // tm,),
        in_specs=[pl.BlockSpec((None, 1, tm), lambda i: (i, 0, 0), memory_space=pltpu.SMEM),
                  pl.BlockSpec(memory_space=pl.ANY),
                  pl.BlockSpec((tm, d), lambda i: (i, 0)),
                  _mod_spec(gt2, tm, rows_per_group, 1)],
        out_specs=pl.BlockSpec((tm, d), lambda i: (i, 0)),
        scratch_shapes=[pltpu.VMEM((tm, d), F32), pltpu.SemaphoreType.DMA],
        compiler_params=_cparams("arbitrary"),
        name="combine",
    )(pos.reshape(t // tm, 1, tm), ys, x1, gt2)


def _moe_kernel(h_ref, cmb_ref, x1_ref, gt2_ref, w1_ref, w3_ref, w2_ref, y_ref, acc_scr):
    e = pl.program_id(1)

    @pl.when(e == 0)
    def _():
        acc_scr[...] = jnp.zeros_like(acc_scr)

    h = h_ref[...]
    a = _dot(h, w1_ref[...])
    g = (a * jax.nn.sigmoid(a)) * _dot(h, w3_ref[...])
    ff = _dot(g.astype(BF16), w2_ref[...])
    cmb = cmb_ref[...]
    lane = lax.broadcasted_iota(jnp.int32, cmb.shape, 1)
    w = jnp.sum(jnp.where(lane == e, cmb, 0.0), axis=-1, keepdims=True)
    acc_scr[...] += w * ff

    @pl.when(e == pl.num_programs(1) - 1)
    def _():
        y_ref[...] = x1_ref[...] + gt2_ref[...] * acc_scr[...]


def _moe(h2, cmb, x1, gt2, w1, w3, w2, tm, rows_per_group):
    t, d = x1.shape
    ne, _, de = w1.shape
    gt2_spec = (pl.BlockSpec((None, 1, d), lambda i, e: (i // (rows_per_group // tm), 0, 0))
                if gt2.ndim == 3 else pl.BlockSpec((tm, d), lambda i, e: (i, 0)))
    return pl.pallas_call(
        _moe_kernel,
        out_shape=jax.ShapeDtypeStruct((t, d), F32),
        grid=(t // tm, ne),
        in_specs=[pl.BlockSpec((tm, d), lambda i, e: (i, 0)),
                  pl.BlockSpec((tm, LANES), lambda i, e: (i, 0)),
                  pl.BlockSpec((tm, d), lambda i, e: (i, 0)),
                  gt2_spec,
                  pl.BlockSpec((None, d, de), lambda i, e: (e, 0, 0)),
                  pl.BlockSpec((None, d, de), lambda i, e: (e, 0, 0)),
                  pl.BlockSpec((None, de, d), lambda i, e: (e, 0, 0))],
        out_specs=pl.BlockSpec((tm, d), lambda i, e: (i, 0)),
        scratch_shapes=[pltpu.VMEM((tm, d), F32)],
        compiler_params=_cparams("arbitrary", "arbitrary"),
        name="moe",
    )(h2, cmb, x1, gt2, w1, w3, w2)


def _rope_tables(pos, gq, gk):
    posf = pos.astype(F32)[:, None]
    inv_r = ROPE_THETA ** (-jnp.arange(0, RET_DK, 2, dtype=F32) / RET_DK)
    ang_r = posf * inv_r[None, :]
    inv_d = ROPE_THETA ** (-jnp.arange(0, DIFF_DH, 2, dtype=F32) / DIFF_DH)
    lane = jnp.arange(LANES)
    half = DIFF_DH // 2
    ang_d = posf * inv_d[lane % half][None, :]
    first_half = ((lane % DIFF_DH) < half)[None, :]
    cos_d, sin_d = jnp.cos(ang_d), jnp.sin(ang_d)

    def diff_tables(g, scale):
        g128 = jnp.tile(g, LANES // DIFF_DH)
        return (cos_d * (g128 * scale)[None, :],
                jnp.where(first_half, -sin_d, 0.0) * (jnp.roll(g128, -half) * scale)[None, :],
                jnp.where(first_half, 0.0, sin_d) * (jnp.roll(g128, half) * scale)[None, :])

    return (jnp.cos(ang_r), jnp.sin(ang_r)) + diff_tables(gq, DIFF_DH ** -0.5 * LOG2E) + diff_tables(gk, 1.0)


def _tile_rows(n, target):
    tm = min(n, target)
    assert n % tm == 0
    return tm


def _layer(x, mod, pos, s0, kc, vc, lw, li, ret_chunk):
    nb, seq, d = x.shape
    t = nb * seq
    x2d = x.reshape(t, d)
    per_token = seq < 128
    if per_token:
        mods = [jnp.repeat(mod[:, k, :], seq, axis=0) for k in range(6)]
        tabs = tuple(jnp.tile(tb, (nb, 1)) for tb in _rope_tables(pos, lw['gq'], lw['gk']))
        rows_per_group = t
    else:
        mods = [mod[:, k:k + 1, :] for k in range(6)]
        tabs = _rope_tables(pos, lw['gq'], lw['gk'])
        rows_per_group = seq
    sh1, sc1, gt1, sh2, sc2, gt2 = mods
    tm = _tile_rows(rows_per_group, 512)

    p, k2d, v2d = _inproj(x2d, sc1, sh1, lw['norm1_g'], lw['w_in'], tabs, lw['bd'],
                          _tile_rows(rows_per_group, 1024), rows_per_group)
    ret_o, s_new = _retention(p, s0, nb, seq, ret_chunk)
    lam_init = 0.8 - 0.6 * math.exp(-0.3 * li)
    if kc is None:
        dif_o = _diffattn(p, k2d, v2d, lw['lam'], lw['subln_g'], nb, seq, lam_init, tq=min(seq, 512))
    else:
        dif_o = _diffattn_cached(p, k2d, v2d, kc, vc, li, lw['lam'], lw['subln_g'], nb, seq, lam_init)
    merge_args = (ret_o, dif_o, p, x2d, gt1, sc2, sh2, lw['norm2_g'], lw['w_br_ret'], lw['w_br_diff'],
                  lw['w_out'], lw['rw_hi'], lw['rw_lo'], lw['rb'], tm, rows_per_group)
    if per_token:
        x1, h2, cmb = _merge(*merge_args, routed=False)
        y = _moe(h2, cmb, x1, gt2, lw['w1'], lw['w3'], lw['w2'], tm, rows_per_group)
    else:
        x1, hx, cls, hist = _merge(*merge_args, routed=True)
        n_tiles = SUBLANES * pl.cdiv(t // MOE_TILE_ROWS + N_CLASSES, SUBLANES)
        pos, tinfo = _route(cls, hist, MOE_TILE_ROWS, n_tiles, tm)
        hx_sorted = _dispatch(pos, hx, n_tiles * MOE_TILE_ROWS, tm)
        y_sorted = _moe_routed(hx_sorted, tinfo[:, 0], tinfo[:, 1], tinfo[:, 2],
                               lw['w1'], lw['w3'], lw['w2'], MOE_TILE_ROWS)
        y = _combine(pos, y_sorted, x1, gt2, tm, rows_per_group)
    return (y.reshape(nb, seq, d), s_new, k2d.reshape(nb, seq, DIFF_HEADS, 2 * DIFF_DH),
            v2d.reshape(nb, seq, DIFF_HEADS, DIFF_DV))


def kernel(x_prompt, x_sample, c_prompt, c_sample, cache_diff_k, cache_diff_v, state_ret, w_ada, b_ada, norm1_g, w_in, diff_qnorm_g, diff_knorm_g, diff_lambda_q1, diff_lambda_k1, diff_lambda_q2, diff_lambda_k2, diff_subln_g, w_br_ret, w_br_diff, w_out, norm2_g, w_group, b_group, w_expert_router, b_expert, w1, w3, w2):
    nb, seq, d = x_prompt.shape
    nbs, seqs, _ = x_sample.shape
    depth = w_in.shape[0]
    past = cache_diff_k.shape[2]
    pos_p = jnp.arange(seq, dtype=jnp.int32)
    pos_s = past + jnp.arange(seqs, dtype=jnp.int32)
    c_all = jnp.concatenate([c_prompt, c_sample], axis=0)
    col = jnp.arange(MXU_COLS)
    bd = jnp.where((col[:, None] // DIFF_DH) == (col[None, :] // DIFF_DH), 1.0 / DIFF_DH, 0.0).astype(BF16)

    xp, xs = x_prompt, x_sample
    outs = [[] for _ in range(6)]
    for li in range(depth):
        rw = jnp.zeros((d, LANES), F32)
        rw = rw.at[:, :N_GROUPS].set(w_group[li]).at[:, N_GROUPS:N_GROUPS + N_EXPERTS].set(w_expert_router[li])
        rw_hi = rw.astype(BF16)
        rb = jnp.zeros((1, LANES), F32)
        rb = rb.at[0, :N_GROUPS].set(b_group[li]).at[0, N_GROUPS:N_GROUPS + N_EXPERTS].set(b_expert[li])
        lw = {
            'norm1_g': norm1_g[li].reshape(1, d), 'norm2_g': norm2_g[li].reshape(1, d),
            'w_in': w_in[li].astype(BF16),
            'gq': diff_qnorm_g[li], 'gk': diff_knorm_g[li],
            'bd': bd,
            'lam': jnp.stack([diff_lambda_q1[li], diff_lambda_k1[li], diff_lambda_q2[li], diff_lambda_k2[li]]),
            'subln_g': diff_subln_g[li].reshape(1, DIFF_DV),
            'w_br_ret': w_br_ret[li].astype(BF16), 'w_br_diff': w_br_diff[li].astype(BF16),
            'w_out': w_out[li].astype(BF16),
            'rw_hi': rw_hi, 'rw_lo': (rw - rw_hi.astype(F32)).astype(BF16), 'rb': rb,
            'w1': w1[li].astype(BF16), 'w3': w3[li].astype(BF16), 'w2': w2[li].astype(BF16),
        }
        mod = _adaln(c_all, w_ada[li], b_ada[li]).reshape(nb + nbs, 6, d)
        xp, sp, kp, vp = _layer(xp, mod[:nb], pos_p, None, None, None, lw, li, ret_chunk=min(seq, 256))
        xs, ss, kss, vss = _layer(xs, mod[nb:], pos_s, state_ret[li], cache_diff_k, cache_diff_v,
                                  lw, li, ret_chunk=seqs)
        for lst, val in zip(outs, (kp, vp, sp, kss, vss, ss)):
            lst.append(val)
    kp, vp, sp, kss, vss, ss = (jnp.stack(lst) for lst in outs)
    return (xp, xs, kp, vp, sp.astype(x_prompt.dtype), kss, vss, ss.astype(state_ret.dtype))
```

```python
import functools
import math

import jax
import jax.numpy as jnp
from jax import lax
from jax.experimental import pallas as pl
from jax.experimental.pallas import tpu as pltpu

F32 = jnp.float32
BF16 = jnp.bfloat16

D_MODEL = 1024
CHUNK = 64
ROPE_THETA = 10000.0
EPS = 1e-6
RET_HEADS = 4
RET_DK = D_MODEL // RET_HEADS
RET_DV = 2 * RET_DK
DIFF_HEADS = 8
DIFF_DH = D_MODEL // (2 * DIFF_HEADS)
DIFF_DV = 2 * DIFF_DH
N_GROUPS = 4
EXPERTS_PER_GROUP = 4
N_EXPERTS = N_GROUPS * EXPERTS_PER_GROUP
D_EXPERT = 512
N_PAIRS = EXPERTS_PER_GROUP * (EXPERTS_PER_GROUP - 1) // 2
N_CLASSES = N_GROUPS * N_PAIRS
MOE_TILE_ROWS = 512

LANES = 128
SUBLANES = 8
MXU_COLS = 256
COL_BLOCK = 1024
N_COL_BLOCKS = 11
P_COLS = 9 * COL_BLOCK
P_RQ, P_RK, P_RV, P_RG, P_DQ, P_GATE = 0, 1024, 2048, 4096, 6144, 7168
VMEM_LIMIT = 56 * 1024 * 1024
LOG2E = math.log2(math.e)


def _cparams(*sem):
    return pltpu.CompilerParams(dimension_semantics=sem, vmem_limit_bytes=VMEM_LIMIT)


def _dot(a, b):
    return jnp.dot(a, b, preferred_element_type=F32)


def _dot_nt(a, b):
    return lax.dot_general(a, b, (((1,), (1,)), ((), ())), preferred_element_type=F32)


def _adaln_kernel(c_ref, w_ref, b_ref, o_ref):
    c = c_ref[...]
    a = c * jax.nn.sigmoid(c)
    o_ref[...] = jnp.dot(a, w_ref[...], preferred_element_type=F32,
                         precision=lax.Precision.HIGHEST) + b_ref[...]


def _adaln(c, w_ada, b_ada):
    rows, d = c.shape
    n = w_ada.shape[1]
    tn = 1536
    return pl.pallas_call(
        _adaln_kernel,
        out_shape=jax.ShapeDtypeStruct((rows, n), F32),
        grid=(n // tn,),
        in_specs=[pl.BlockSpec((rows, d), lambda j: (0, 0)),
                  pl.BlockSpec((d, tn), lambda j: (0, j)),
                  pl.BlockSpec((1, tn), lambda j: (0, j))],
        out_specs=pl.BlockSpec((rows, tn), lambda j: (0, j)),
        compiler_params=_cparams("arbitrary"),
        name="adaln",
    )(c, w_ada, b_ada.reshape(1, n))


def _rms(x):
    return x * lax.rsqrt(jnp.mean(x * x, axis=-1, keepdims=True) + EPS)


def _inproj_kernel(x_ref, sc_ref, sh_ref, g1_ref, w_ref, rc_ref, rs_ref, qc_ref, qsn_ref, qsp_ref,
                   kc_ref, ksn_ref, ksp_ref, bd_ref, p_ref, k_ref, v_ref, h_scr):
    j = pl.program_id(1)

    @pl.when(j == 0)
    def _():
        h = _rms(x_ref[...]) * g1_ref[...]
        h = h * (1.0 + sc_ref[...]) + sh_ref[...]
        h_scr[...] = h.astype(BF16)

    def strips():
        h = h_scr[...]
        for lo in range(0, COL_BLOCK, MXU_COLS):
            yield lo, _dot(h, w_ref[:, lo:lo + MXU_COLS])

    def ret_rope(scale):
        c = rc_ref[...]
        s = rs_ref[...]
        for lo, acc in strips():
            x1 = acc[:, 0:LANES]
            x2 = acc[:, LANES:2 * LANES]
            p_ref[:, lo:lo + LANES] = ((x1 * c - x2 * s) * scale).astype(BF16)
            p_ref[:, lo + LANES:lo + 2 * LANES] = ((x1 * s + x2 * c) * scale).astype(BF16)

    def diff_norm_rope(c_ref, sn_ref, sp_ref, out_ref):
        c = c_ref[...]
        sn = sn_ref[...]
        sp = sp_ref[...]
        for lo, acc in list(strips()):
            y = acc * lax.rsqrt(_dot((acc * acc).astype(BF16), bd_ref[...]) + EPS)
            for off in range(0, MXU_COLS, LANES):
                yg = y[:, off:off + LANES]
                o = yg * c + pltpu.roll(yg, LANES - DIFF_DH // 2, 1) * sn + pltpu.roll(yg, DIFF_DH // 2, 1) * sp
                out_ref[:, lo + off:lo + off + LANES] = o.astype(out_ref.dtype)

    def elementwise(fn, out_ref):
        for lo, acc in strips():
            out_ref[:, lo:lo + MXU_COLS] = fn(acc).astype(out_ref.dtype)

    @pl.when(j == 0)
    def _():
        ret_rope(1.0)

    @pl.when(j == 1)
    def _():
        ret_rope(RET_DK ** -0.5)

    @pl.when((j == 2) | (j == 3))
    def _():
        elementwise(lambda a: a, p_ref)

    @pl.when((j == 4) | (j == 5))
    def _():
        elementwise(lambda a: a * jax.nn.sigmoid(a), p_ref)

    @pl.when(j == 6)
    def _():
        diff_norm_rope(qc_ref, qsn_ref, qsp_ref, p_ref)

    @pl.when((j == 7) | (j == 8))
    def _():
        elementwise(jax.nn.sigmoid, p_ref)

    @pl.when(j == 9)
    def _():
        diff_norm_rope(kc_ref, ksn_ref, ksp_ref, k_ref)

    @pl.when(j == 10)
    def _():
        elementwise(lambda a: a, v_ref)


def _mod_spec(arr, tm, rows_per_group, ngrid):
    d = arr.shape[-1]
    if arr.ndim == 3:
        tiles_per_group = rows_per_group // tm
        if ngrid == 2:
            return pl.BlockSpec((None, 1, d), lambda i, j: (i // tiles_per_group, 0, 0))
        return pl.BlockSpec((None, 1, d), lambda i: (i // tiles_per_group, 0, 0))
    if ngrid == 2:
        return pl.BlockSpec((tm, d), lambda i, j: (i, 0))
    return pl.BlockSpec((tm, d), lambda i: (i, 0))


def _inproj(x2d, sc1, sh1, g1, w_bf, tabs, bd, tm, rows_per_group):
    t, d = x2d.shape
    nrep = tabs[0].shape[0] // tm

    def w_map(i, j):
        return (0, jnp.where(j < 7, j, jnp.where(j < 9, j + 2, j - 2)))

    tab_spec = pl.BlockSpec((tm, LANES), lambda i, j: (i % nrep, 0))
    return pl.pallas_call(
        _inproj_kernel,
        out_shape=(jax.ShapeDtypeStruct((t, P_COLS), BF16),
                   jax.ShapeDtypeStruct((t, d), F32),
                   jax.ShapeDtypeStruct((t, d), F32)),
        grid=(t // tm, N_COL_BLOCKS),
        in_specs=[pl.BlockSpec((tm, d), lambda i, j: (i, 0)),
                  _mod_spec(sc1, tm, rows_per_group, 2),
                  _mod_spec(sh1, tm, rows_per_group, 2),
                  pl.BlockSpec((1, d), lambda i, j: (0, 0)),
                  pl.BlockSpec((d, COL_BLOCK), w_map),
                  *([tab_spec] * len(tabs)),
                  pl.BlockSpec((MXU_COLS, MXU_COLS), lambda i, j: (0, 0))],
        out_specs=(pl.BlockSpec((tm, COL_BLOCK), lambda i, j: (i, jnp.minimum(j, 8))),
                   pl.BlockSpec((tm, d), lambda i, j: (i, 0)),
                   pl.BlockSpec((tm, d), lambda i, j: (i, 0))),
        scratch_shapes=[pltpu.VMEM((tm, d), BF16)],
        compiler_params=_cparams("arbitrary", "arbitrary"),
        name="inproj",
    )(x2d, sc1, sh1, g1, w_bf, *tabs, bd)


def _retention_kernel(*refs, has_state, chunk):
    if has_state:
        q_ref, k_ref, v_ref, rg_ref, dm_ref, dv_ref, s0_ref, o_ref, sout_ref = refs
        s = s0_ref[...]
    else:
        q_ref, k_ref, v_ref, rg_ref, dm_ref, dv_ref, o_ref, sout_ref = refs
        s = jnp.zeros(sout_ref.shape, F32)
    dvec = dv_ref[...]
    q_decay = dvec[:, 0:1]
    k_decay = dvec[:, 1:2]
    chunk_decay = dvec[0:1, 2:3]

    for c in range(q_ref.shape[0] // chunk):
        rows = slice(c * chunk, (c + 1) * chunk)
        q = q_ref[rows, :]
        k = k_ref[rows, :]
        v = v_ref[rows, :]
        scores = (_dot_nt(q, k) * dm_ref[...]).astype(BF16)
        o = _dot(scores, v) + _dot(q, s.astype(BF16)) * q_decay
        o = _rms(o) * rg_ref[rows, :].astype(F32)
        o_ref[rows, :] = o.astype(BF16)
        kd_t = (k.astype(F32) * k_decay).T.astype(BF16)
        s = s * chunk_decay + _dot(kd_t, v)
    sout_ref[...] = s


def _retention(p, s0, nb, seq, chunk):
    t = p.shape[0]
    idx = jnp.arange(chunk, dtype=F32)
    log_gamma = jnp.log1p(-jnp.exp2(-5.0 - jnp.arange(RET_HEADS, dtype=F32)))
    rel = idx[:, None] - idx[None, :]
    dmat = jnp.where(rel >= 0, jnp.exp(jnp.maximum(rel, 0.0)[None] * log_gamma[:, None, None]), 0.0)
    q_decay = jnp.exp((idx + 1.0)[None, :] * log_gamma[:, None])
    k_decay = jnp.exp((chunk - 1.0 - idx)[None, :] * log_gamma[:, None])
    c_decay = jnp.broadcast_to(jnp.exp(chunk * log_gamma)[:, None], (RET_HEADS, chunk))
    dvec = jnp.zeros((RET_HEADS, chunk, LANES), F32)
    dvec = dvec.at[:, :, 0].set(q_decay).at[:, :, 1].set(k_decay).at[:, :, 2].set(c_decay)

    qb, vb = P_RQ // RET_DK, P_RV // RET_DV
    kb, gb = P_RK // RET_DK, P_RG // RET_DV
    in_specs = [pl.BlockSpec((seq, RET_DK), lambda b, h: (b, qb + h)),
                pl.BlockSpec((seq, RET_DK), lambda b, h: (b, kb + h)),
                pl.BlockSpec((seq, RET_DV), lambda b, h: (b, vb + h)),
                pl.BlockSpec((seq, RET_DV), lambda b, h: (b, gb + h)),
                pl.BlockSpec((None, chunk, chunk), lambda b, h: (h, 0, 0)),
                pl.BlockSpec((None, chunk, LANES), lambda b, h: (h, 0, 0))]
    args = [p, p, p, p, dmat, dvec]
    state_spec = pl.BlockSpec((None, None, RET_DK, RET_DV), lambda b, h: (b, h, 0, 0))
    if s0 is not None:
        in_specs.append(state_spec)
        args.append(s0)
    return pl.pallas_call(
        functools.partial(_retention_kernel, has_state=s0 is not None, chunk=chunk),
        out_shape=(jax.ShapeDtypeStruct((t, RET_HEADS * RET_DV), BF16),
                   jax.ShapeDtypeStruct((nb, RET_HEADS, RET_DK, RET_DV), F32)),
        grid=(nb, RET_HEADS),
        in_specs=in_specs,
        out_specs=(pl.BlockSpec((seq, RET_DV), lambda b, h: (b, h)), state_spec),
        compiler_params=_cparams("arbitrary", "arbitrary"),
        name="retention",
    )(*args)


def _lambda(lam_ref, lam_init):
    l1 = jnp.sum(lam_ref[0:1, :] * lam_ref[1:2, :], axis=-1, keepdims=True)
    l2 = jnp.sum(lam_ref[2:3, :] * lam_ref[3:4, :], axis=-1, keepdims=True)
    return jnp.exp(l1) - jnp.exp(l2) + lam_init


def _split_maps(q):
    lane = lax.broadcasted_iota(jnp.int32, q.shape, 1)
    zero = jnp.zeros_like(q)
    return jnp.where(lane < DIFF_DH, q, zero), jnp.where(lane >= DIFF_DH, q, zero)


def _diffattn_kernel(q_ref, k_ref, v_ref, lam_ref, g_ref, o_ref, kb_scr, vb_scr, m0_scr, m1_scr, acc0_scr, acc1_scr,
                     *, tq, lam_init):
    seq = k_ref.shape[0]
    nq = seq // tq
    kb_scr[...] = k_ref[...].astype(BF16)
    vb_scr[:, 0:DIFF_DV] = v_ref[...].astype(BF16)
    vb_scr[:, DIFF_DV:2 * DIFF_DV] = jnp.ones((seq, DIFF_DV), BF16)
    lam = _lambda(lam_ref, lam_init)
    row = lax.broadcasted_iota(jnp.int32, (tq, tq), 0)
    col = lax.broadcasted_iota(jnp.int32, (tq, tq), 1)
    visible = (col // CHUNK) <= (row // CHUNK)
    m_refs = (m0_scr, m1_scr)
    acc_refs = (acc0_scr, acc1_scr)

    def scores(qi, kt):
        qmaps = _split_maps(q_ref[qi * tq:(qi + 1) * tq, :])
        kblk = kb_scr[kt * tq:(kt + 1) * tq, :]
        out = []
        for mp in range(2):
            sc = _dot_nt(qmaps[mp], kblk)
            out.append(jnp.where(visible, sc, -1e30) if kt == qi else sc)
        return out

    pairs = [(qi, kt) for qi in range(nq) for kt in range(qi + 1)]
    pending = scores(*pairs[0])
    for idx, (qi, kt) in enumerate(pairs):
        scs = pending
        if idx + 1 < len(pairs):
            pending = scores(*pairs[idx + 1])
        vblk = vb_scr[kt * tq:(kt + 1) * tq, :]
        for mp in range(2):
            sc = scs[mp]
            if kt == 0:
                m_new = jnp.max(sc, axis=-1, keepdims=True)
                acc_refs[mp][...] = _dot(jnp.exp2(sc - m_new).astype(BF16), vblk)
            else:
                m_old = m_refs[mp][...]
                m_new = jnp.maximum(m_old, jnp.max(sc, axis=-1, keepdims=True))
                p = jnp.exp2(sc - m_new)
                acc_refs[mp][...] = jnp.exp2(m_old - m_new) * acc_refs[mp][...] + _dot(p.astype(BF16), vblk)
            m_refs[mp][...] = m_new
        if kt == qi:
            a1 = acc0_scr[...]
            a2 = acc1_scr[...]
            o = (a1[:, 0:DIFF_DV] / a1[:, DIFF_DV:2 * DIFF_DV]
                 - lam * (a2[:, 0:DIFF_DV] / a2[:, DIFF_DV:2 * DIFF_DV]))
            o = _rms(o) * g_ref[...] * (1.0 - lam_init)
            o_ref[qi * tq:(qi + 1) * tq, :] = o.astype(BF16)


def _diffattn(p, k2d, v2d, lam_vecs, subln_g, nb, seq, lam_init, tq):
    t = p.shape[0]
    qcol = P_DQ // DIFF_DV
    return pl.pallas_call(
        functools.partial(_diffattn_kernel, tq=tq, lam_init=lam_init),
        out_shape=jax.ShapeDtypeStruct((t, DIFF_HEADS * DIFF_DV), BF16),
        grid=(nb, DIFF_HEADS),
        in_specs=[pl.BlockSpec((seq, DIFF_DV), lambda b, h: (b, qcol + h)),
                  pl.BlockSpec((seq, DIFF_DV), lambda b, h: (b, h)),
                  pl.BlockSpec((seq, DIFF_DV), lambda b, h: (b, h)),
                  pl.BlockSpec((4, DIFF_DH), lambda b, h: (0, 0)),
                  pl.BlockSpec((1, DIFF_DV), lambda b, h: (0, 0))],
        out_specs=pl.BlockSpec((seq, DIFF_DV), lambda b, h: (b, h)),
        scratch_shapes=[pltpu.VMEM((seq, DIFF_DV), BF16), pltpu.VMEM((seq, 2 * DIFF_DV), BF16),
                        pltpu.VMEM((tq, 1), F32), pltpu.VMEM((tq, 1), F32),
                        pltpu.VMEM((tq, 2 * DIFF_DV), F32), pltpu.VMEM((tq, 2 * DIFF_DV), F32)],
        compiler_params=_cparams("arbitrary", "arbitrary"),
        name="diffattn",
    )(p, k2d, v2d, lam_vecs, subln_g)


def _diffattn_cached_kernel(q_ref, kc_ref, vc_ref, kn_ref, vn_ref, lam_ref, g_ref, o_ref, k_scr, v_scr,
                            *, past, new, lam_init):
    total = k_scr.shape[0]
    pad = jnp.zeros((total - past, DIFF_DV), BF16)
    qpos = past + lax.broadcasted_iota(jnp.int32, (new, total), 0)
    kpos = lax.broadcasted_iota(jnp.int32, (new, total), 1)
    visible = ((kpos // CHUNK) <= (qpos // CHUNK)) & (kpos < past + new)
    lam = _lambda(lam_ref, lam_init)

    for h in range(DIFF_HEADS):
        cols = slice(h * DIFF_DV, (h + 1) * DIFF_DV)
        k_scr[0:past, :] = kc_ref[:, h, :].astype(BF16)
        v_scr[0:past, :] = vc_ref[:, h, :].astype(BF16)
        k_scr[past:total, :] = pad
        v_scr[past:total, :] = pad
        k_scr[past:past + new, :] = kn_ref[:, cols].astype(BF16)
        v_scr[past:past + new, :] = vn_ref[:, cols].astype(BF16)
        kall = k_scr[...]
        vall = v_scr[...]

        def softmax_av(qm):
            sc = jnp.where(visible, _dot_nt(qm, kall), -1e30)
            m = jnp.max(sc, axis=-1, keepdims=True)
            p = jnp.exp2(sc - m)
            return _dot(p.astype(BF16), vall) / jnp.sum(p, axis=-1, keepdims=True)

        qa, qb = _split_maps(q_ref[:, cols])
        o = softmax_av(qa) - lam * softmax_av(qb)
        o = _rms(o) * g_ref[...] * (1.0 - lam_init)
        o_ref[:, cols] = o.astype(BF16)


def _diffattn_cached(p, k2d, v2d, kc, vc, li, lam_vecs, subln_g, nb, new, lam_init):
    t = p.shape[0]
    past = kc.shape[2]
    width = DIFF_HEADS * DIFF_DV
    total = past + LANES * pl.cdiv(new, LANES)
    cache_spec = pl.BlockSpec((None, None, past, DIFF_HEADS, DIFF_DV), lambda b: (li, b, 0, 0, 0))
    return pl.pallas_call(
        functools.partial(_diffattn_cached_kernel, past=past, new=new, lam_init=lam_init),
        out_shape=jax.ShapeDtypeStruct((t, width), BF16),
        grid=(nb,),
        in_specs=[pl.BlockSpec((new, width), lambda b: (b, P_DQ // width)),
                  cache_spec, cache_spec,
                  pl.BlockSpec((new, width), lambda b: (b, 0)),
                  pl.BlockSpec((new, width), lambda b: (b, 0)),
                  pl.BlockSpec((4, DIFF_DH), lambda b: (0, 0)),
                  pl.BlockSpec((1, DIFF_DV), lambda b: (0, 0))],
        out_specs=pl.BlockSpec((new, width), lambda b: (b, 0)),
        scratch_shapes=[pltpu.VMEM((total, DIFF_DV), BF16), pltpu.VMEM((total, DIFF_DV), BF16)],
        compiler_params=_cparams("arbitrary"),
        name="diffattn_cached",
    )(p, kc, vc, k2d, v2d, lam_vecs, subln_g)


def _merge_kernel(ret_ref, dif_ref, gr_ref, gd_ref, x_ref, gt1_ref, sc2_ref, sh2_ref, g2_ref,
                  wr_ref, wd_ref, wo_ref, rwh_ref, rwl_ref, rb_ref, x1_ref, *out_refs, routed):
    a = _dot(ret_ref[...], wr_ref[...])
    b = _dot(dif_ref[...], wd_ref[...])
    merged = gr_ref[...].astype(F32) * a + gd_ref[...].astype(F32) * b
    x1 = x_ref[...] + gt1_ref[...] * _dot(merged.astype(BF16), wo_ref[...])
    x1_ref[...] = x1
    h2 = _rms(x1) * g2_ref[...]
    h2 = h2 * (1.0 + sc2_ref[...]) + sh2_ref[...]

    hi = h2.astype(BF16)
    lo = (h2 - hi.astype(F32)).astype(BF16)
    logits = _dot(hi, rwh_ref[...]) + _dot(lo, rwh_ref[...]) + _dot(hi, rwl_ref[...]) + rb_ref[...]

    lane = lax.broadcasted_iota(jnp.int32, logits.shape, 1)
    big = jnp.int32(4 * LANES)

    def masked_argmax(mask):
        vmax = jnp.max(jnp.where(mask, logits, -jnp.inf), axis=-1, keepdims=True)
        idx = jnp.min(jnp.where(mask & (logits == vmax), lane, big), axis=-1, keepdims=True)
        return vmax, idx

    is_group = lane < N_GROUPS
    gmax, gidx = masked_argmax(is_group)
    gprob = 1.0 / jnp.sum(jnp.where(is_group, jnp.exp(logits - gmax), 0.0), axis=-1, keepdims=True)
    first = N_GROUPS + gidx * EXPERTS_PER_GROUP
    in_group = (lane >= first) & (lane < first + EXPERTS_PER_GROUP)
    v1, i1 = masked_argmax(in_group)
    v2, i2 = masked_argmax(in_group & (lane != i1))
    e2 = jnp.exp(v2 - v1)
    w1 = gprob / (1.0 + e2)
    w2 = gprob * e2 / (1.0 + e2)
    if not routed:
        h2_ref, cmb_ref = out_refs
        h2_ref[...] = h2.astype(BF16)
        cmb_ref[...] = (jnp.where(lane + N_GROUPS == i1, w1, 0.0)
                        + jnp.where(lane + N_GROUPS == i2, w2, 0.0))
        return

    hx_ref, cls_ref, hist_ref = out_refs
    d = h2.shape[1]
    first_is_lo = i1 < i2
    lo = jnp.minimum(i1, i2) - first
    hi = jnp.maximum(i1, i2) - first
    cls = gidx * N_PAIRS + jnp.right_shift(lo * (7 - lo), 1) + hi - lo - 1
    hx_ref[:, 0:d] = h2
    hx_ref[:, d:d + LANES] = jnp.where(lane == 0, jnp.where(first_is_lo, w1, w2),
                                       jnp.where(lane == 1, jnp.where(first_is_lo, w2, w1), 0.0))
    cls_ref[...] = cls

    @pl.when(pl.program_id(0) == 0)
    def _():
        hist_ref[...] = jnp.zeros_like(hist_ref)

    hist_ref[...] += jnp.sum(jnp.where(lane == cls, 1.0, 0.0), axis=0, keepdims=True)


def _merge(ret_o, dif_o, p, x2d, gt1, sc2, sh2, g2, wr, wd, wo, rwh, rwl, rb, tm, rows_per_group, routed):
    t, d = x2d.shape
    gcol = P_GATE // d
    const = lambda shape: pl.BlockSpec(shape, lambda i: (0, 0), pipeline_mode=pl.Buffered(1))
    if routed:
        out_shape = (jax.ShapeDtypeStruct((t, d), F32),
                     jax.ShapeDtypeStruct((t, d + LANES), F32),
                     jax.ShapeDtypeStruct((t, 1), jnp.int32),
                     jax.ShapeDtypeStruct((1, LANES), F32))
        out_specs = (pl.BlockSpec((tm, d), lambda i: (i, 0)),
                     pl.BlockSpec((tm, d + LANES), lambda i: (i, 0)),
                     pl.BlockSpec((tm, 1), lambda i: (i, 0)),
                     pl.BlockSpec((1, LANES), lambda i: (0, 0)))
    else:
        out_shape = (jax.ShapeDtypeStruct((t, d), F32),
                     jax.ShapeDtypeStruct((t, d), BF16),
                     jax.ShapeDtypeStruct((t, LANES), F32))
        out_specs = (pl.BlockSpec((tm, d), lambda i: (i, 0)),
                     pl.BlockSpec((tm, d), lambda i: (i, 0)),
                     pl.BlockSpec((tm, LANES), lambda i: (i, 0)))
    return pl.pallas_call(
        functools.partial(_merge_kernel, routed=routed),
        out_shape=out_shape,
        grid=(t // tm,),
        in_specs=[pl.BlockSpec((tm, ret_o.shape[1]), lambda i: (i, 0)),
                  pl.BlockSpec((tm, d), lambda i: (i, 0)),
                  pl.BlockSpec((tm, d), lambda i: (i, gcol)),
                  pl.BlockSpec((tm, d), lambda i: (i, gcol + 1)),
                  pl.BlockSpec((tm, d), lambda i: (i, 0)),
                  _mod_spec(gt1, tm, rows_per_group, 1),
                  _mod_spec(sc2, tm, rows_per_group, 1),
                  _mod_spec(sh2, tm, rows_per_group, 1),
                  const((1, d)),
                  const(wr.shape), const(wd.shape), const(wo.shape),
                  const(rwh.shape), const(rwl.shape), const(rb.shape)],
        out_specs=out_specs,
        compiler_params=_cparams("arbitrary"),
        name="merge",
    )(ret_o, dif_o, p, p, x2d, gt1, sc2, sh2, g2, wr, wd, wo, rwh, rwl, rb)


def _route_kernel(cls_ref, hist_ref, tri_ref, pos_ref, tinfo_ref, carry_scr, off_scr, *, tile_rows):
    i = pl.program_id(0)

    def excl_cumsum(v):
        lane8 = lax.broadcasted_iota(jnp.int32, v.shape, 1)
        inc = v
        for sh in (1, 2, 4, 8, 16, 32, 64):
            inc = inc + jnp.where(lane8 >= sh, pltpu.roll(inc, sh, 1), 0.0)
        return inc - v

    @pl.when(i == 0)
    def _():
        cnt = hist_ref[...]
        ntile = jnp.floor((cnt + (tile_rows - 1)) * (1.0 / tile_rows))
        start = excl_cumsum(jnp.broadcast_to(ntile, (SUBLANES, LANES)))[0:1, :]
        off_scr[...] = start * tile_rows
        carry_scr[...] = jnp.zeros_like(carry_scr)
        nt = tinfo_ref.shape[0]
        jrow = lax.broadcasted_iota(jnp.int32, (nt, LANES), 0).astype(F32)
        lane = lax.broadcasted_iota(jnp.int32, (nt, LANES), 1)
        is_cls = lane < N_CLASSES
        tcls = jnp.sum(jnp.where(is_cls & (start <= jrow), 1.0, 0.0), axis=-1, keepdims=True) - 1.0
        pick = lane == tcls.astype(jnp.int32)
        cnt_j = jnp.sum(jnp.where(pick, cnt, 0.0), axis=-1, keepdims=True)
        start_j = jnp.sum(jnp.where(pick, start, 0.0), axis=-1, keepdims=True)
        valid = jnp.clip(cnt_j - (jrow[:, 0:1] - start_j) * tile_rows, 0.0, float(tile_rows)).astype(jnp.int32)
        tc = tcls.astype(jnp.int32)
        grp = sum((tc >= g * N_PAIRS).astype(jnp.int32) for g in range(1, N_GROUPS))
        pair = tc - grp * N_PAIRS
        lo = (pair >= 3).astype(jnp.int32) + (pair >= 5).astype(jnp.int32)
        hi = pair - jnp.right_shift(lo * (7 - lo), 1) + lo + 1
        ea = grp * EXPERTS_PER_GROUP + lo
        eb = grp * EXPERTS_PER_GROUP + hi
        tinfo_ref[...] = jnp.where(lane == 0, ea, jnp.where(lane == 1, eb, jnp.where(lane == 2, valid, 0)))

    cls = cls_ref[...]
    lane = lax.broadcasted_iota(jnp.int32, (cls.shape[0], LANES), 1)
    onehot = lane == cls
    earlier = _dot(tri_ref[...], jnp.where(onehot, 1.0, 0.0).astype(BF16))
    base = carry_scr[...] + off_scr[...]
    pos = jnp.sum(jnp.where(onehot, earlier + base, 0.0), axis=-1, keepdims=True)
    pos_ref[...] = pos.astype(jnp.int32)
    carry_scr[...] += jnp.sum(jnp.where(onehot, 1.0, 0.0), axis=0, keepdims=True)


def _route(cls, hist, tile_rows, n_tiles, tm):
    t = cls.shape[0]
    tri = jnp.tril(jnp.ones((tm, tm), BF16), -1)
    return pl.pallas_call(
        functools.partial(_route_kernel, tile_rows=tile_rows),
        out_shape=(jax.ShapeDtypeStruct((t, 1), jnp.int32),
                   jax.ShapeDtypeStruct((n_tiles, LANES), jnp.int32)),
        grid=(t // tm,),
        in_specs=[pl.BlockSpec((tm, 1), lambda i: (i, 0)),
                  pl.BlockSpec((1, LANES), lambda i: (0, 0)),
                  pl.BlockSpec((tm, tm), lambda i: (0, 0))],
        out_specs=(pl.BlockSpec((tm, 1), lambda i: (i, 0)),
                   pl.BlockSpec((n_tiles, LANES), lambda i: (0, 0))),
        scratch_shapes=[pltpu.VMEM((1, LANES), F32), pltpu.VMEM((1, LANES), F32)],
        compiler_params=_cparams("arbitrary"),
        name="route",
    )(cls, hist, tri)


def _dispatch_kernel(pos_ref, hx_ref, sorted_ref, sem):
    tm = hx_ref.shape[0]

    def row_copy(r):
        return pltpu.make_async_copy(hx_ref.at[pl.ds(r, 1)], sorted_ref.at[pl.ds(pos_ref[0, r], 1)], sem)

    def start(r, carry):
        row_copy(r).start()
        return carry

    def wait(r, carry):
        row_copy(r).wait()
        return carry

    lax.fori_loop(0, tm, start, 0, unroll=8)
    lax.fori_loop(0, tm, wait, 0, unroll=8)


def _dispatch(pos, hx, n_slots, tm):
    t, w = hx.shape
    return pl.pallas_call(
        _dispatch_kernel,
        out_shape=jax.ShapeDtypeStruct((n_slots, w), F32),
        grid=(t // tm,),
        in_specs=[pl.BlockSpec((None, 1, tm), lambda i: (i, 0, 0), memory_space=pltpu.SMEM),
                  pl.BlockSpec((tm, w), lambda i: (i, 0))],
        out_specs=pl.BlockSpec(memory_space=pl.ANY),
        scratch_shapes=[pltpu.SemaphoreType.DMA],
        compiler_params=pltpu.CompilerParams(dimension_semantics=("arbitrary",), vmem_limit_bytes=VMEM_LIMIT,
                                             has_side_effects=True),
        name="dispatch",
    )(pos.reshape(t // tm, 1, tm), hx)


def _moe_routed_kernel(ea_ref, eb_ref, valid_ref, hx_ref, w1a_ref, w3a_ref, w2a_ref, w1b_ref, w3b_ref, w2b_ref, y_ref):
    j = pl.program_id(0)
    nvalid = valid_ref[j]

    @pl.when(nvalid == 0)
    def _():
        y_ref[...] = jnp.zeros_like(y_ref)

    @pl.when(nvalid > 0)
    def _():
        tm, d = y_ref.shape
        row = lax.broadcasted_iota(jnp.int32, (tm, 1), 0)
        real = row < nvalid
        h = jnp.where(real, hx_ref[:, 0:d], 0.0).astype(BF16)
        wts = jnp.where(real, hx_ref[:, d:d + LANES], 0.0)

        def expert(w1_ref, w3_ref, w2_ref):
            a = _dot(h, w1_ref[...])
            g = (a * jax.nn.sigmoid(a)) * _dot(h, w3_ref[...])
            return _dot(g.astype(BF16), w2_ref[...])

        y_ref[...] = (wts[:, 0:1] * expert(w1a_ref, w3a_ref, w2a_ref)
                      + wts[:, 1:2] * expert(w1b_ref, w3b_ref, w2b_ref))


def _moe_routed(hx_sorted, ea, eb, valid, w1, w3, w2, tile_rows):
    n_slots, w = hx_sorted.shape
    ne, d, de = w1.shape
    up = lambda sel: pl.BlockSpec((None, d, de), lambda j, ea, eb, valid: ((ea, eb)[sel][j], 0, 0))
    down = lambda sel: pl.BlockSpec((None, de, d), lambda j, ea, eb, valid: ((ea, eb)[sel][j], 0, 0))
    grid_spec = pltpu.PrefetchScalarGridSpec(
        num_scalar_prefetch=3,
        grid=(n_slots // tile_rows,),
        in_specs=[pl.BlockSpec((tile_rows, w), lambda j, ea, eb, valid: (j, 0)),
                  up(0), up(0), down(0), up(1), up(1), down(1)],
        out_specs=pl.BlockSpec((tile_rows, d), lambda j, ea, eb, valid: (j, 0)),
    )
    return pl.pallas_call(
        _moe_routed_kernel,
        out_shape=jax.ShapeDtypeStruct((n_slots, d), F32),
        grid_spec=grid_spec,
        compiler_params=_cparams("arbitrary"),
        name="moe_routed",
    )(ea, eb, valid, hx_sorted, w1, w3, w2, w1, w3, w2)


def _combine_kernel(pos_ref, ys_ref, x1_ref, gt2_ref, y_ref, buf, sem):
    tm = buf.shape[0]

    def row_copy(r):
        return pltpu.make_async_copy(ys_ref.at[pl.ds(pos_ref[0, r], 1)], buf.at[pl.ds(r, 1)], sem)

    def start(r, carry):
        row_copy(r).start()
        return carry

    def wait(r, carry):
        row_copy(r).wait()
        return carry

    lax.fori_loop(0, tm, start, 0, unroll=8)
    lax.fori_loop(0, tm, wait, 0, unroll=8)
    y_ref[...] = x1_ref[...] + gt2_ref[...] * buf[...]


def _combine(pos, ys, x1, gt2, tm, rows_per_group):
    t, d = x1.shape
    return pl.pallas_call(
        _combine_kernel,
        out_shape=jax.ShapeDtypeStruct((t, d), F32),
        grid=(t // tm,),
        in_specs=[pl.BlockSpec((None, 1, tm), lambda i: (i, 0, 0), memory_space=pltpu.SMEM),
                  pl.BlockSpec(memory_space=pl.ANY),
                  pl.BlockSpec((tm, d), lambda i: (i, 0)),
                  _mod_spec(gt2, tm, rows_per_group, 1)],
        out_specs=pl.BlockSpec((tm, d), lambda i: (i, 0)),
        scratch_shapes=[pltpu.VMEM((tm, d), F32), pltpu.SemaphoreType.DMA],
        compiler_params=_cparams("arbitrary"),
        name="combine",
    )(pos.reshape(t // tm, 1, tm), ys, x1, gt2)


def _moe_kernel(h_ref, cmb_ref, x1_ref, gt2_ref, w1_ref, w3_ref, w2_ref, y_ref, acc_scr):
    e = pl.program_id(1)

    @pl.when(e == 0)
    def _():
        acc_scr[...] = jnp.zeros_like(acc_scr)

    h = h_ref[...]
    a = _dot(h, w1_ref[...])
    g = (a * jax.nn.sigmoid(a)) * _dot(h, w3_ref[...])
    ff = _dot(g.astype(BF16), w2_ref[...])
    cmb = cmb_ref[...]
    lane = lax.broadcasted_iota(jnp.int32, cmb.shape, 1)
    w = jnp.sum(jnp.where(lane == e, cmb, 0.0), axis=-1, keepdims=True)
    acc_scr[...] += w * ff

    @pl.when(e == pl.num_programs(1) - 1)
    def _():
        y_ref[...] = x1_ref[...] + gt2_ref[...] * acc_scr[...]


def _moe(h2, cmb, x1, gt2, w1, w3, w2, tm, rows_per_group):
    t, d = x1.shape
    ne, _, de = w1.shape
    gt2_spec = (pl.BlockSpec((None, 1, d), lambda i, e: (i // (rows_per_group // tm), 0, 0))
                if gt2.ndim == 3 else pl.BlockSpec((tm, d), lambda i, e: (i, 0)))
    return pl.pallas_call(
        _moe_kernel,
        out_shape=jax.ShapeDtypeStruct((t, d), F32),
        grid=(t // tm, ne),
        in_specs=[pl.BlockSpec((tm, d), lambda i, e: (i, 0)),
                  pl.BlockSpec((tm, LANES), lambda i, e: (i, 0)),
                  pl.BlockSpec((tm, d), lambda i, e: (i, 0)),
                  gt2_spec,
                  pl.BlockSpec((None, d, de), lambda i, e: (e, 0, 0)),
                  pl.BlockSpec((None, d, de), lambda i, e: (e, 0, 0)),
                  pl.BlockSpec((None, de, d), lambda i, e: (e, 0, 0))],
        out_specs=pl.BlockSpec((tm, d), lambda i, e: (i, 0)),
        scratch_shapes=[pltpu.VMEM((tm, d), F32)],
        compiler_params=_cparams("arbitrary", "arbitrary"),
        name="moe",
    )(h2, cmb, x1, gt2, w1, w3, w2)


def _rope_tables(pos, gq, gk):
    posf = pos.astype(F32)[:, None]
    inv_r = ROPE_THETA ** (-jnp.arange(0, RET_DK, 2, dtype=F32) / RET_DK)
    ang_r = posf * inv_r[None, :]
    inv_d = ROPE_THETA ** (-jnp.arange(0, DIFF_DH, 2, dtype=F32) / DIFF_DH)
    lane = jnp.arange(LANES)
    half = DIFF_DH // 2
    ang_d = posf * inv_d[lane % half][None, :]
    first_half = ((lane % DIFF_DH) < half)[None, :]
    cos_d, sin_d = jnp.cos(ang_d), jnp.sin(ang_d)

    def diff_tables(g, scale):
        g128 = jnp.tile(g, LANES // DIFF_DH)
        return (cos_d * (g128 * scale)[None, :],
                jnp.where(first_half, -sin_d, 0.0) * (jnp.roll(g128, -half) * scale)[None, :],
                jnp.where(first_half, 0.0, sin_d) * (jnp.roll(g128, half) * scale)[None, :])

    return (jnp.cos(ang_r), jnp.sin(ang_r)) + diff_tables(gq, DIFF_DH ** -0.5 * LOG2E) + diff_tables(gk, 1.0)


def _tile_rows(n, target):
    tm = min(n, target)
    assert n % tm == 0
    return tm


def _layer(x, mod, pos, s0, kc, vc, lw, li, ret_chunk):
    nb, seq, d = x.shape
    t = nb * seq
    x2d = x.reshape(t, d)
    per_token = seq < 128
    if per_token:
        mods = [jnp.repeat(mod[:, k, :], seq, axis=0) for k in range(6)]
        tabs = tuple(jnp.tile(tb, (nb, 1)) for tb in _rope_tables(pos, lw['gq'], lw['gk']))
        rows_per_group = t
    else:
        mods = [mod[:, k:k + 1, :] for k in range(6)]
        tabs = _rope_tables(pos, lw['gq'], lw['gk'])
        rows_per_group = seq
    sh1, sc1, gt1, sh2, sc2, gt2 = mods
    tm = _tile_rows(rows_per_group, 512)

    p, k2d, v2d = _inproj(x2d, sc1, sh1, lw['norm1_g'], lw['w_in'], tabs, lw['bd'],
                          _tile_rows(rows_per_group, 1024), rows_per_group)
    ret_o, s_new = _retention(p, s0, nb, seq, ret_chunk)
    lam_init = 0.8 - 0.6 * math.exp(-0.3 * li)
    if kc is None:
        dif_o = _diffattn(p, k2d, v2d, lw['lam'], lw['subln_g'], nb, seq, lam_init, tq=min(seq, 512))
    else:
        dif_o = _diffattn_cached(p, k2d, v2d, kc, vc, li, lw['lam'], lw['subln_g'], nb, seq, lam_init)
    merge_args = (ret_o, dif_o, p, x2d, gt1, sc2, sh2, lw['norm2_g'], lw['w_br_ret'], lw['w_br_diff'],
                  lw['w_out'], lw['rw_hi'], lw['rw_lo'], lw['rb'], tm, rows_per_group)
    if per_token:
        x1, h2, cmb = _merge(*merge_args, routed=False)
        y = _moe(h2, cmb, x1, gt2, lw['w1'], lw['w3'], lw['w2'], tm, rows_per_group)
    else:
        x1, hx, cls, hist = _merge(*merge_args, routed=True)
        n_tiles = SUBLANES * pl.cdiv(t // MOE_TILE_ROWS + N_CLASSES, SUBLANES)
        pos, tinfo = _route(cls, hist, MOE_TILE_ROWS, n_tiles, tm)
        hx_sorted = _dispatch(pos, hx, n_tiles * MOE_TILE_ROWS, tm)
        y_sorted = _moe_routed(hx_sorted, tinfo[:, 0], tinfo[:, 1], tinfo[:, 2],
                               lw['w1'], lw['w3'], lw['w2'], MOE_TILE_ROWS)
        y = _combine(pos, y_sorted, x1, gt2, tm, rows_per_group)
    return (y.reshape(nb, seq, d), s_new, k2d.reshape(nb, seq, DIFF_HEADS, 2 * DIFF_DH),
            v2d.reshape(nb, seq, DIFF_HEADS, DIFF_DV))


def kernel(x_prompt, x_sample, c_prompt, c_sample, cache_diff_k, cache_diff_v, state_ret, w_ada, b_ada, norm1_g, w_in, diff_qnorm_g, diff_knorm_g, diff_lambda_q1, diff_lambda_k1, diff_lambda_q2, diff_lambda_k2, diff_subln_g, w_br_ret, w_br_diff, w_out, norm2_g, w_group, b_group, w_expert_router, b_expert, w1, w3, w2):
    nb, seq, d = x_prompt.shape
    nbs, seqs, _ = x_sample.shape
    depth = w_in.shape[0]
    past = cache_diff_k.shape[2]
    pos_p = jnp.arange(seq, dtype=jnp.int32)
    pos_s = past + jnp.arange(seqs, dtype=jnp.int32)
    c_all = jnp.concatenate([c_prompt, c_sample], axis=0)
    col = jnp.arange(MXU_COLS)
    bd = jnp.where((col[:, None] // DIFF_DH) == (col[None, :] // DIFF_DH), 1.0 / DIFF_DH, 0.0).astype(BF16)

    xp, xs = x_prompt, x_sample
    outs = [[] for _ in range(6)]
    for li in range(depth):
        rw = jnp.zeros((d, LANES), F32)
        rw = rw.at[:, :N_GROUPS].set(w_group[li]).at[:, N_GROUPS:N_GROUPS + N_EXPERTS].set(w_expert_router[li])
        rw_hi = rw.astype(BF16)
        rb = jnp.zeros((1, LANES), F32)
        rb = rb.at[0, :N_GROUPS].set(b_group[li]).at[0, N_GROUPS:N_GROUPS + N_EXPERTS].set(b_expert[li])
        lw = {
            'norm1_g': norm1_g[li].reshape(1, d), 'norm2_g': norm2_g[li].reshape(1, d),
            'w_in': w_in[li].astype(BF16),
            'gq': diff_qnorm_g[li], 'gk': diff_knorm_g[li],
            'bd': bd,
            'lam': jnp.stack([diff_lambda_q1[li], diff_lambda_k1[li], diff_lambda_q2[li], diff_lambda_k2[li]]),
            'subln_g': diff_subln_g[li].reshape(1, DIFF_DV),
            'w_br_ret': w_br_ret[li].astype(BF16), 'w_br_diff': w_br_diff[li].astype(BF16),
            'w_out': w_out[li].astype(BF16),
            'rw_hi': rw_hi, 'rw_lo': (rw - rw_hi.astype(F32)).astype(BF16), 'rb': rb,
            'w1': w1[li].astype(BF16), 'w3': w3[li].astype(BF16), 'w2': w2[li].astype(BF16),
        }
        mod = _adaln(c_all, w_ada[li], b_ada[li]).reshape(nb + nbs, 6, d)
        xp, sp, kp, vp = _layer(xp, mod[:nb], pos_p, None, None, None, lw, li, ret_chunk=min(seq, 256))
        xs, ss, kss, vss = _layer(xs, mod[nb:], pos_s, state_ret[li], cache_diff_k, cache_diff_v,
                                  lw, li, ret_chunk=seqs)
        for lst, val in zip(outs, (kp, vp, sp, kss, vss, ss)):
            lst.append(val)
    kp, vp, sp, kss, vss, ss = (jnp.stack(lst) for lst in outs)
    return (xp, xs, kp, vp, sp.astype(x_prompt.dtype), kss, vss, ss.astype(state_ret.dtype))
```

```python
import functools
import math

import jax
import jax.numpy as jnp
from jax import lax
from jax.experimental import pallas as pl
from jax.experimental.pallas import tpu as pltpu

F32 = jnp.float32
BF16 = jnp.bfloat16

D_MODEL = 1024
CHUNK = 64
ROPE_THETA = 10000.0
EPS = 1e-6
RET_HEADS = 4
RET_DK = D_MODEL // RET_HEADS
RET_DV = 2 * RET_DK
DIFF_HEADS = 8
DIFF_DH = D_MODEL // (2 * DIFF_HEADS)
DIFF_DV = 2 * DIFF_DH
N_GROUPS = 4
EXPERTS_PER_GROUP = 4
N_EXPERTS = N_GROUPS * EXPERTS_PER_GROUP
D_EXPERT = 512
N_PAIRS = EXPERTS_PER_GROUP * (EXPERTS_PER_GROUP - 1) // 2
N_CLASSES = N_GROUPS * N_PAIRS
MOE_TILE_ROWS = 512

LANES = 128
SUBLANES = 8
MXU_COLS = 256
COL_BLOCK = 1024
IN_BLOCK_ORDER = (0, 1, 2, 3, 4, 5, 6, 9, 10, 7, 8)
N_SLAB_BLOCKS = 8
SLAB_RQ, SLAB_RK, SLAB_RV, SLAB_RG, SLAB_GATE = 0, 1, 2, 4, 6
VMEM_LIMIT = 56 * 1024 * 1024
LOG2E = math.log2(math.e)


def _cparams(*sem):
    return pltpu.CompilerParams(dimension_semantics=sem, vmem_limit_bytes=VMEM_LIMIT)


def _dot(a, b):
    return jnp.dot(a, b, preferred_element_type=F32)


def _dot_nt(a, b):
    return lax.dot_general(a, b, (((1,), (1,)), ((), ())), preferred_element_type=F32)


def _adaln_kernel(c_ref, w_ref, b_ref, o_ref):
    c = c_ref[...]
    a = c * jax.nn.sigmoid(c)
    o_ref[...] = jnp.dot(a, w_ref[...], preferred_element_type=F32,
                         precision=lax.Precision.HIGHEST) + b_ref[...]


def _adaln(c, w_ada, b_ada):
    rows, d = c.shape
    n = w_ada.shape[1]
    tn = 1536
    return pl.pallas_call(
        _adaln_kernel,
        out_shape=jax.ShapeDtypeStruct((rows, n), F32),
        grid=(n // tn,),
        in_specs=[pl.BlockSpec((rows, d), lambda j: (0, 0)),
                  pl.BlockSpec((d, tn), lambda j: (0, j)),
                  pl.BlockSpec((1, tn), lambda j: (0, j))],
        out_specs=pl.BlockSpec((rows, tn), lambda j: (0, j)),
        compiler_params=_cparams("arbitrary"),
        name="adaln",
    )(c, w_ada, b_ada.reshape(1, n))


def _rms(x):
    return x * lax.rsqrt(jnp.mean(x * x, axis=-1, keepdims=True) + EPS)


def _inproj_kernel(x_ref, sc_ref, sh_ref, g1_ref, w_ref, rc_ref, rs_ref, qc_ref, qsn_ref, qsp_ref,
                   kc_ref, ksn_ref, ksp_ref, bd_ref, p_ref, qd_ref, k_ref, v_ref, h_scr):
    j = pl.program_id(1)

    @pl.when(j == 0)
    def _():
        h = _rms(x_ref[...]) * g1_ref[...]
        h = h * (1.0 + sc_ref[...]) + sh_ref[...]
        h_scr[...] = h.astype(BF16)

    def strips():
        h = h_scr[...]
        for lo in range(0, COL_BLOCK, MXU_COLS):
            yield lo, _dot(h, w_ref[:, lo:lo + MXU_COLS])

    def ret_rope(scale):
        c = rc_ref[...]
        s = rs_ref[...]
        for lo, acc in strips():
            x1 = acc[:, 0:LANES]
            x2 = acc[:, LANES:2 * LANES]
            p_ref[:, lo:lo + LANES] = ((x1 * c - x2 * s) * scale).astype(BF16)
            p_ref[:, lo + LANES:lo + 2 * LANES] = ((x1 * s + x2 * c) * scale).astype(BF16)

    def diff_norm():
        for lo, acc in list(strips()):
            k_ref[:, lo:lo + MXU_COLS] = acc * lax.rsqrt(_dot((acc * acc).astype(BF16), bd_ref[...]) + EPS)

    def diff_rope(lo, c_ref, sn_ref, sp_ref, out_ref):
        for off in range(lo, lo + MXU_COLS, LANES):
            yg = k_ref[:, off:off + LANES]
            o = (yg * c_ref[...] + pltpu.roll(yg, LANES - DIFF_DH // 2, 1) * sn_ref[...]
                 + pltpu.roll(yg, DIFF_DH // 2, 1) * sp_ref[...])
            out_ref[:, off:off + LANES] = o.astype(out_ref.dtype)

    def elementwise(fn, out_ref, after_strip=None):
        for lo, acc in strips():
            out_ref[:, lo:lo + MXU_COLS] = fn(acc).astype(out_ref.dtype)
            if after_strip is not None:
                after_strip(lo)

    @pl.when(j == 0)
    def _():
        ret_rope(1.0)

    @pl.when(j == 1)
    def _():
        ret_rope(RET_DK ** -0.5)

    @pl.when((j == 2) | (j == 3))
    def _():
        elementwise(lambda a: a, p_ref)

    @pl.when((j == 4) | (j == 5))
    def _():
        elementwise(lambda a: a * jax.nn.sigmoid(a), p_ref)

    @pl.when((j == 6) | (j == 9))
    def _():
        diff_norm()

    @pl.when(j == 7)
    def _():
        elementwise(jax.nn.sigmoid, p_ref, lambda lo: diff_rope(lo, qc_ref, qsn_ref, qsp_ref, qd_ref))

    @pl.when(j == 8)
    def _():
        elementwise(jax.nn.sigmoid, p_ref)

    @pl.when(j == 10)
    def _():
        elementwise(lambda a: a, v_ref, lambda lo: diff_rope(lo, kc_ref, ksn_ref, ksp_ref, k_ref))


def _mod_spec(arr, tm, rows_per_group, ngrid):
    d = arr.shape[-1]
    if arr.ndim == 3:
        tiles_per_group = rows_per_group // tm
        if ngrid == 2:
            return pl.BlockSpec((None, 1, d), lambda i, j: (i // tiles_per_group, 0, 0))
        return pl.BlockSpec((None, 1, d), lambda i: (i // tiles_per_group, 0, 0))
    if ngrid == 2:
        return pl.BlockSpec((tm, d), lambda i, j: (i, 0))
    return pl.BlockSpec((tm, d), lambda i: (i, 0))


def _inproj(x2d, sc1, sh1, g1, w_blocks, tabs, bd, tm, rows_per_group):
    t, d = x2d.shape
    nrep = tabs[0].shape[0] // tm

    def slab_map(i, j):
        return (jnp.where(j < 6, j, jnp.where(j < 9, jnp.maximum(j - 1, 5), N_SLAB_BLOCKS - 1)), i, 0)

    tab_spec = pl.BlockSpec((tm, LANES), lambda i, j: (i % nrep, 0))
    row_spec = pl.BlockSpec((tm, d), lambda i, j: (i, 0))
    return pl.pallas_call(
        _inproj_kernel,
        out_shape=(jax.ShapeDtypeStruct((N_SLAB_BLOCKS, t, COL_BLOCK), BF16),
                   jax.ShapeDtypeStruct((t, d), BF16),
                   jax.ShapeDtypeStruct((t, d), F32),
                   jax.ShapeDtypeStruct((t, d), F32)),
        grid=(t // tm, len(IN_BLOCK_ORDER)),
        in_specs=[row_spec,
                  _mod_spec(sc1, tm, rows_per_group, 2),
                  _mod_spec(sh1, tm, rows_per_group, 2),
                  pl.BlockSpec((1, d), lambda i, j: (0, 0)),
                  pl.BlockSpec((None, d, COL_BLOCK), lambda i, j: (j, 0, 0)),
                  *([tab_spec] * len(tabs)),
                  pl.BlockSpec((MXU_COLS, MXU_COLS), lambda i, j: (0, 0))],
        out_specs=(pl.BlockSpec((None, tm, COL_BLOCK), slab_map), row_spec, row_spec, row_spec),
        scratch_shapes=[pltpu.VMEM((tm, d), BF16)],
        compiler_params=_cparams("arbitrary", "arbitrary"),
        name="inproj",
    )(x2d, sc1, sh1, g1, w_blocks, *tabs, bd)


def _retention_kernel(*refs, has_state, chunk):
    if has_state:
        q_ref, k_ref, v_ref, rg_ref, dm_ref, dv_ref, s0_ref, o_ref, sout_ref = refs
        s = s0_ref[...]
    else:
        q_ref, k_ref, v_ref, rg_ref, dm_ref, dv_ref, o_ref, sout_ref = refs
        s = jnp.zeros(sout_ref.shape, F32)
    dvec = dv_ref[...]
    q_decay = dvec[:, 0:1]
    k_decay = dvec[:, 1:2]
    chunk_decay = dvec[0:1, 2:3]

    for c in range(q_ref.shape[0] // chunk):
        rows = slice(c * chunk, (c + 1) * chunk)
        q = q_ref[rows, :]
        k = k_ref[rows, :]
        v = v_ref[rows, :]
        scores = (_dot_nt(q, k) * dm_ref[...]).astype(BF16)
        o = _dot(scores, v) + _dot(q, s.astype(BF16)) * q_decay
        o = _rms(o) * rg_ref[rows, :].astype(F32)
        o_ref[rows, :] = o.astype(BF16)
        kd_t = (k.astype(F32) * k_decay).T.astype(BF16)
        s = s * chunk_decay + _dot(kd_t, v)
    sout_ref[...] = s


def _retention(p, s0, nb, seq, chunk):
    t = p.shape[1]
    idx = jnp.arange(chunk, dtype=F32)
    log_gamma = jnp.log1p(-jnp.exp2(-5.0 - jnp.arange(RET_HEADS, dtype=F32)))
    rel = idx[:, None] - idx[None, :]
    dmat = jnp.where(rel >= 0, jnp.exp(jnp.maximum(rel, 0.0)[None] * log_gamma[:, None, None]), 0.0)
    q_decay = jnp.exp((idx + 1.0)[None, :] * log_gamma[:, None])
    k_decay = jnp.exp((chunk - 1.0 - idx)[None, :] * log_gamma[:, None])
    c_decay = jnp.broadcast_to(jnp.exp(chunk * log_gamma)[:, None], (RET_HEADS, chunk))
    dvec = jnp.zeros((RET_HEADS, chunk, LANES), F32)
    dvec = dvec.at[:, :, 0].set(q_decay).at[:, :, 1].set(k_decay).at[:, :, 2].set(c_decay)

    wide = COL_BLOCK // RET_DV
    in_specs = [pl.BlockSpec((None, seq, RET_DK), lambda b, h: (SLAB_RQ, b, h)),
                pl.BlockSpec((None, seq, RET_DK), lambda b, h: (SLAB_RK, b, h)),
                pl.BlockSpec((None, seq, RET_DV), lambda b, h: (SLAB_RV + h // wide, b, h % wide)),
                pl.BlockSpec((None, seq, RET_DV), lambda b, h: (SLAB_RG + h // wide, b, h % wide)),
                pl.BlockSpec((None, chunk, chunk), lambda b, h: (h, 0, 0)),
                pl.BlockSpec((None, chunk, LANES), lambda b, h: (h, 0, 0))]
    args = [p, p, p, p, dmat, dvec]
    state_spec = pl.BlockSpec((None, None, RET_DK, RET_DV), lambda b, h: (b, h, 0, 0))
    if s0 is not None:
        in_specs.append(state_spec)
        args.append(s0)
    return pl.pallas_call(
        functools.partial(_retention_kernel, has_state=s0 is not None, chunk=chunk),
        out_shape=(jax.ShapeDtypeStruct((t, RET_HEADS * RET_DV), BF16),
                   jax.ShapeDtypeStruct((nb, RET_HEADS, RET_DK, RET_DV), F32)),
        grid=(nb, RET_HEADS),
        in_specs=in_specs,
        out_specs=(pl.BlockSpec((seq, RET_DV), lambda b, h: (b, h)), state_spec),
        compiler_params=_cparams("arbitrary", "arbitrary"),
        name="retention",
    )(*args)


def _lambda(lam_ref, lam_init):
    l1 = jnp.sum(lam_ref[0:1, :] * lam_ref[1:2, :], axis=-1, keepdims=True)
    l2 = jnp.sum(lam_ref[2:3, :] * lam_ref[3:4, :], axis=-1, keepdims=True)
    return jnp.exp(l1) - jnp.exp(l2) + lam_init


def _split_maps(q):
    lane = lax.broadcasted_iota(jnp.int32, q.shape, 1)
    zero = jnp.zeros_like(q)
    return jnp.where(lane < DIFF_DH, q, zero), jnp.where(lane >= DIFF_DH, q, zero)


def _diffattn_kernel(q_ref, k_ref, v_ref, lam_ref, g_ref, o_ref, kb_scr, vb_scr, m0_scr, m1_scr, acc0_scr, acc1_scr,
                     *, tq, lam_init):
    seq = k_ref.shape[0]
    nq = seq // tq
    kb_scr[...] = k_ref[...].astype(BF16)
    vb_scr[:, 0:DIFF_DV] = v_ref[...].astype(BF16)
    vb_scr[:, DIFF_DV:2 * DIFF_DV] = jnp.ones((seq, DIFF_DV), BF16)
    lam = _lambda(lam_ref, lam_init)
    m_refs = (m0_scr, m1_scr)
    acc_refs = (acc0_scr, acc1_scr)
    half = tq // 2

    items = []
    for qi in range(nq):
        base = qi * tq
        for kt in range(qi):
            items.append((base, tq, kt * tq, tq, kt == 0, False))
        items.append((base, half, base, half, qi == 0, False))
        items.append((base + half, half, base, tq, qi == 0, True))

    def scores(q0, qn, k0, kn):
        qmaps = _split_maps(q_ref[q0:q0 + qn, :])
        kblk = kb_scr[k0:k0 + kn, :]
        needs_mask = (k0 + kn - 1) // CHUNK > q0 // CHUNK
        if needs_mask:
            row = q0 + lax.broadcasted_iota(jnp.int32, (qn, kn), 0)
            col = k0 + lax.broadcasted_iota(jnp.int32, (qn, kn), 1)
            visible = (col // CHUNK) <= (row // CHUNK)
        out = []
        for mp in range(2):
            sc = _dot_nt(qmaps[mp], kblk)
            out.append(jnp.where(visible, sc, -1e30) if needs_mask else sc)
        return out

    pending = scores(*items[0][:4])
    for idx, (q0, qn, k0, kn, first, last) in enumerate(items):
        scs = pending
        if idx + 1 < len(items):
            pending = scores(*items[idx + 1][:4])
        vblk = vb_scr[k0:k0 + kn, :]
        r0 = q0 % tq
        for mp in range(2):
            sc = scs[mp]
            if first:
                m_new = jnp.max(sc, axis=-1, keepdims=True)
                acc_refs[mp][r0:r0 + qn, :] = _dot(jnp.exp2(sc - m_new).astype(BF16), vblk)
            else:
                m_old = m_refs[mp][r0:r0 + qn, :]
                m_new = jnp.maximum(m_old, jnp.max(sc, axis=-1, keepdims=True))
                p = jnp.exp2(sc - m_new)
                acc_refs[mp][r0:r0 + qn, :] = (jnp.exp2(m_old - m_new) * acc_refs[mp][r0:r0 + qn, :]
                                               + _dot(p.astype(BF16), vblk))
            m_refs[mp][r0:r0 + qn, :] = m_new
        if last:
            a1 = acc0_scr[...]
            a2 = acc1_scr[...]
            o = (a1[:, 0:DIFF_DV] / a1[:, DIFF_DV:2 * DIFF_DV]
                 - lam * (a2[:, 0:DIFF_DV] / a2[:, DIFF_DV:2 * DIFF_DV]))
            o = _rms(o) * g_ref[...] * (1.0 - lam_init)
            qt = q0 + qn - tq
            o_ref[qt:qt + tq, :] = o.astype(BF16)


def _diffattn(qd, k2d, v2d, lam_vecs, subln_g, nb, seq, lam_init, tq):
    t = qd.shape[0]
    return pl.pallas_call(
        functools.partial(_diffattn_kernel, tq=tq, lam_init=lam_init),
        out_shape=jax.ShapeDtypeStruct((t, DIFF_HEADS * DIFF_DV), BF16),
        grid=(nb, DIFF_HEADS),
        in_specs=[pl.BlockSpec((seq, DIFF_DV), lambda b, h: (b, h)),
                  pl.BlockSpec((seq, DIFF_DV), lambda b, h: (b, h)),
                  pl.BlockSpec((seq, DIFF_DV), lambda b, h: (b, h)),
                  pl.BlockSpec((4, DIFF_DH), lambda b, h: (0, 0)),
                  pl.BlockSpec((1, DIFF_DV), lambda b, h: (0, 0))],
        out_specs=pl.BlockSpec((seq, DIFF_DV), lambda b, h: (b, h)),
        scratch_shapes=[pltpu.VMEM((seq, DIFF_DV), BF16), pltpu.VMEM((seq, 2 * DIFF_DV), BF16),
                        pltpu.VMEM((tq, 1), F32), pltpu.VMEM((tq, 1), F32),
                        pltpu.VMEM((tq, 2 * DIFF_DV), F32), pltpu.VMEM((tq, 2 * DIFF_DV), F32)],
        compiler_params=_cparams("arbitrary", "arbitrary"),
        name="diffattn",
    )(qd, k2d, v2d, lam_vecs, subln_g)


def _diffattn_cached_kernel(q_ref, kc_ref, vc_ref, kn_ref, vn_ref, lam_ref, g_ref, o_ref, k_scr, v_scr,
                            *, past, new, lam_init):
    total = k_scr.shape[0]
    pad = jnp.zeros((total - past, DIFF_DV), BF16)
    qpos = past + lax.broadcasted_iota(jnp.int32, (new, total), 0)
    kpos = lax.broadcasted_iota(jnp.int32, (new, total), 1)
    visible = ((kpos // CHUNK) <= (qpos // CHUNK)) & (kpos < past + new)
    lam = _lambda(lam_ref, lam_init)

    for h in range(DIFF_HEADS):
        cols = slice(h * DIFF_DV, (h + 1) * DIFF_DV)
        k_scr[0:past, :] = kc_ref[:, h, :].astype(BF16)
        v_scr[0:past, :] = vc_ref[:, h, :].astype(BF16)
        k_scr[past:total, :] = pad
        v_scr[past:total, :] = pad
        k_scr[past:past + new, :] = kn_ref[:, cols].astype(BF16)
        v_scr[past:past + new, :] = vn_ref[:, cols].astype(BF16)
        kall = k_scr[...]
        vall = v_scr[...]

        def softmax_av(qm):
            sc = jnp.where(visible, _dot_nt(qm, kall), -1e30)
            m = jnp.max(sc, axis=-1, keepdims=True)
            p = jnp.exp2(sc - m)
            return _dot(p.astype(BF16), vall) / jnp.sum(p, axis=-1, keepdims=True)

        qa, qb = _split_maps(q_ref[:, cols])
        o = softmax_av(qa) - lam * softmax_av(qb)
        o = _rms(o) * g_ref[...] * (1.0 - lam_init)
        o_ref[:, cols] = o.astype(BF16)


def _diffattn_cached(qd, k2d, v2d, kc, vc, li, lam_vecs, subln_g, nb, new, lam_init):
    t = qd.shape[0]
    past = kc.shape[2]
    width = DIFF_HEADS * DIFF_DV
    total = past + LANES * pl.cdiv(new, LANES)
    cache_spec = pl.BlockSpec((None, None, past, DIFF_HEADS, DIFF_DV), lambda b: (li, b, 0, 0, 0))
    return pl.pallas_call(
        functools.partial(_diffattn_cached_kernel, past=past, new=new, lam_init=lam_init),
        out_shape=jax.ShapeDtypeStruct((t, width), BF16),
        grid=(nb,),
        in_specs=[pl.BlockSpec((new, width), lambda b: (b, 0)),
                  cache_spec, cache_spec,
                  pl.BlockSpec((new, width), lambda b: (b, 0)),
                  pl.BlockSpec((new, width), lambda b: (b, 0)),
                  pl.BlockSpec((4, DIFF_DH), lambda b: (0, 0)),
                  pl.BlockSpec((1, DIFF_DV), lambda b: (0, 0))],
        out_specs=pl.BlockSpec((new, width), lambda b: (b, 0)),
        scratch_shapes=[pltpu.VMEM((total, DIFF_DV), BF16), pltpu.VMEM((total, DIFF_DV), BF16)],
        compiler_params=_cparams("arbitrary"),
        name="diffattn_cached",
    )(qd, kc, vc, k2d, v2d, lam_vecs, subln_g)


def _merge_kernel(ret_ref, dif_ref, gr_ref, gd_ref, x_ref, gt1_ref, sc2_ref, sh2_ref, g2_ref,
                  wr_ref, wd_ref, wo_ref, rwh_ref, rwl_ref, rb_ref, x1_ref, *out_refs, routed):
    a = _dot(ret_ref[...], wr_ref[...])
    b = _dot(dif_ref[...], wd_ref[...])
    merged = gr_ref[...].astype(F32) * a + gd_ref[...].astype(F32) * b
    x1 = x_ref[...] + gt1_ref[...] * _dot(merged.astype(BF16), wo_ref[...])
    x1_ref[...] = x1
    h2 = _rms(x1) * g2_ref[...]
    h2 = h2 * (1.0 + sc2_ref[...]) + sh2_ref[...]

    hi = h2.astype(BF16)
    lo = (h2 - hi.astype(F32)).astype(BF16)
    logits = _dot(hi, rwh_ref[...]) + _dot(lo, rwh_ref[...]) + _dot(hi, rwl_ref[...]) + rb_ref[...]

    lane = lax.broadcasted_iota(jnp.int32, logits.shape, 1)
    big = jnp.int32(4 * LANES)

    def masked_argmax(mask):
        vmax = jnp.max(jnp.where(mask, logits, -jnp.inf), axis=-1, keepdims=True)
        idx = jnp.min(jnp.where(mask & (logits == vmax), lane, big), axis=-1, keepdims=True)
        return vmax, idx

    is_group = lane < N_GROUPS
    gmax, gidx = masked_argmax(is_group)
    gprob = 1.0 / jnp.sum(jnp.where(is_group, jnp.exp(logits - gmax), 0.0), axis=-1, keepdims=True)
    first = N_GROUPS + gidx * EXPERTS_PER_GROUP
    in_group = (lane >= first) & (lane < first + EXPERTS_PER_GROUP)
    v1, i1 = masked_argmax(in_group)
    v2, i2 = masked_argmax(in_group & (lane != i1))
    e2 = jnp.exp(v2 - v1)
    w1 = gprob / (1.0 + e2)
    w2 = gprob * e2 / (1.0 + e2)
    if not routed:
        h2_ref, cmb_ref = out_refs
        h2_ref[...] = h2.astype(BF16)
        cmb_ref[...] = (jnp.where(lane + N_GROUPS == i1, w1, 0.0)
                        + jnp.where(lane + N_GROUPS == i2, w2, 0.0))
        return

    hx_ref, cls_ref, hist_ref = out_refs
    d = h2.shape[1]
    first_is_lo = i1 < i2
    lo = jnp.minimum(i1, i2) - first
    hi = jnp.maximum(i1, i2) - first
    cls = gidx * N_PAIRS + jnp.right_shift(lo * (7 - lo), 1) + hi - lo - 1
    hx_ref[:, 0:d] = h2
    hx_ref[:, d:d + LANES] = jnp.where(lane == 0, jnp.where(first_is_lo, w1, w2),
                                       jnp.where(lane == 1, jnp.where(first_is_lo, w2, w1), 0.0))
    cls_ref[...] = cls

    @pl.when(pl.program_id(0) == 0)
    def _():
        hist_ref[...] = jnp.zeros_like(hist_ref)

    hist_ref[...] += jnp.sum(jnp.where(lane == cls, 1.0, 0.0), axis=0, keepdims=True)


def _merge(ret_o, dif_o, p, x2d, gt1, sc2, sh2, g2, wr, wd, wo, rwh, rwl, rb, tm, rows_per_group, routed):
    t, d = x2d.shape
    const = lambda shape: pl.BlockSpec(shape, lambda i: (0, 0), pipeline_mode=pl.Buffered(1))
    if routed:
        out_shape = (jax.ShapeDtypeStruct((t, d), F32),
                     jax.ShapeDtypeStruct((t, d + LANES), F32),
                     jax.ShapeDtypeStruct((t, 1), jnp.int32),
                     jax.ShapeDtypeStruct((1, LANES), F32))
        out_specs = (pl.BlockSpec((tm, d), lambda i: (i, 0)),
                     pl.BlockSpec((tm, d + LANES), lambda i: (i, 0)),
                     pl.BlockSpec((tm, 1), lambda i: (i, 0)),
                     pl.BlockSpec((1, LANES), lambda i: (0, 0)))
    else:
        out_shape = (jax.ShapeDtypeStruct((t, d), F32),
                     jax.ShapeDtypeStruct((t, d), BF16),
                     jax.ShapeDtypeStruct((t, LANES), F32))
        out_specs = (pl.BlockSpec((tm, d), lambda i: (i, 0)),
                     pl.BlockSpec((tm, d), lambda i: (i, 0)),
                     pl.BlockSpec((tm, LANES), lambda i: (i, 0)))
    return pl.pallas_call(
        functools.partial(_merge_kernel, routed=routed),
        out_shape=out_shape,
        grid=(t // tm,),
        in_specs=[pl.BlockSpec((tm, ret_o.shape[1]), lambda i: (i, 0)),
                  pl.BlockSpec((tm, d), lambda i: (i, 0)),
                  pl.BlockSpec((None, tm, d), lambda i: (SLAB_GATE, i, 0)),
                  pl.BlockSpec((None, tm, d), lambda i: (SLAB_GATE + 1, i, 0)),
                  pl.BlockSpec((tm, d), lambda i: (i, 0)),
                  _mod_spec(gt1, tm, rows_per_group, 1),
                  _mod_spec(sc2, tm, rows_per_group, 1),
                  _mod_spec(sh2, tm, rows_per_group, 1),
                  const((1, d)),
                  const(wr.shape), const(wd.shape), const(wo.shape),
                  const(rwh.shape), const(rwl.shape), const(rb.shape)],
        out_specs=out_specs,
        compiler_params=_cparams("arbitrary"),
        name="merge",
    )(ret_o, dif_o, p, p, x2d, gt1, sc2, sh2, g2, wr, wd, wo, rwh, rwl, rb)


def _route_kernel(cls_ref, hist_ref, tri_ref, pos_ref, tinfo_ref, carry_scr, off_scr, *, tile_rows):
    i = pl.program_id(0)

    def excl_cumsum(v):
        lane8 = lax.broadcasted_iota(jnp.int32, v.shape, 1)
        inc = v
        for sh in (1, 2, 4, 8, 16, 32, 64):
            inc = inc + jnp.where(lane8 >= sh, pltpu.roll(inc, sh, 1), 0.0)
        return inc - v

    @pl.when(i == 0)
    def _():
        cnt = hist_ref[...]
        ntile = jnp.floor((cnt + (tile_rows - 1)) * (1.0 / tile_rows))
        start = excl_cumsum(jnp.broadcast_to(ntile, (SUBLANES, LANES)))[0:1, :]
        off_scr[...] = start * tile_rows
        carry_scr[...] = jnp.zeros_like(carry_scr)
        nt = tinfo_ref.shape[0]
        jrow = lax.broadcasted_iota(jnp.int32, (nt, LANES), 0).astype(F32)
        lane = lax.broadcasted_iota(jnp.int32, (nt, LANES), 1)
        is_cls = lane < N_CLASSES
        tcls = jnp.sum(jnp.where(is_cls & (start <= jrow), 1.0, 0.0), axis=-1, keepdims=True) - 1.0
        pick = lane == tcls.astype(jnp.int32)
        cnt_j = jnp.sum(jnp.where(pick, cnt, 0.0), axis=-1, keepdims=True)
        start_j = jnp.sum(jnp.where(pick, start, 0.0), axis=-1, keepdims=True)
        valid = jnp.clip(cnt_j - (jrow[:, 0:1] - start_j) * tile_rows, 0.0, float(tile_rows)).astype(jnp.int32)
        tc = tcls.astype(jnp.int32)
        grp = sum((tc >= g * N_PAIRS).astype(jnp.int32) for g in range(1, N_GROUPS))
        pair = tc - grp * N_PAIRS
        lo = (pair >= 3).astype(jnp.int32) + (pair >= 5).astype(jnp.int32)
        hi = pair - jnp.right_shift(lo * (7 - lo), 1) + lo + 1
        ea = grp * EXPERTS_PER_GROUP + lo
        eb = grp * EXPERTS_PER_GROUP + hi
        tinfo_ref[...] = jnp.where(lane == 0, ea, jnp.where(lane == 1, eb, jnp.where(lane == 2, valid, 0)))

    cls = cls_ref[...]
    lane = lax.broadcasted_iota(jnp.int32, (cls.shape[0], LANES), 1)
    onehot = lane == cls
    earlier = _dot(tri_ref[...], jnp.where(onehot, 1.0, 0.0).astype(BF16))
    base = carry_scr[...] + off_scr[...]
    pos = jnp.sum(jnp.where(onehot, earlier + base, 0.0), axis=-1, keepdims=True)
    pos_ref[...] = pos.astype(jnp.int32)
    carry_scr[...] += jnp.sum(jnp.where(onehot, 1.0, 0.0), axis=0, keepdims=True)


def _route(cls, hist, tile_rows, n_tiles, tm):
    t = cls.shape[0]
    tri = jnp.tril(jnp.ones((tm, tm), BF16), -1)
    return pl.pallas_call(
        functools.partial(_route_kernel, tile_rows=tile_rows),
        out_shape=(jax.ShapeDtypeStruct((t, 1), jnp.int32),
                   jax.ShapeDtypeStruct((n_tiles, LANES), jnp.int32)),
        grid=(t // tm,),
        in_specs=[pl.BlockSpec((tm, 1), lambda i: (i, 0)),
                  pl.BlockSpec((1, LANES), lambda i: (0, 0)),
                  pl.BlockSpec((tm, tm), lambda i: (0, 0))],
        out_specs=(pl.BlockSpec((tm, 1), lambda i: (i, 0)),
                   pl.BlockSpec((n_tiles, LANES), lambda i: (0, 0))),
        scratch_shapes=[pltpu.VMEM((1, LANES), F32), pltpu.VMEM((1, LANES), F32)],
        compiler_params=_cparams("arbitrary"),
        name="route",
    )(cls, hist, tri)


def _dispatch_kernel(pos_ref, hx_ref, sorted_ref, sem):
    tm = hx_ref.shape[0]

    def row_copy(r):
        return pltpu.make_async_copy(hx_ref.at[pl.ds(r, 1)], sorted_ref.at[pl.ds(pos_ref[0, r], 1)], sem)

    def start(r, carry):
        row_copy(r).start()
        return carry

    def wait(r, carry):
        row_copy(r).wait()
        return carry

    lax.fori_loop(0, tm, start, 0, unroll=8)
    lax.fori_loop(0, tm, wait, 0, unroll=8)


def _dispatch(pos, hx, n_slots, tm):
    t, w = hx.shape
    return pl.pallas_call(
        _dispatch_kernel,
        out_shape=jax.ShapeDtypeStruct((n_slots, w), F32),
        grid=(t // tm,),
        in_specs=[pl.BlockSpec((None, 1, tm), lambda i: (i, 0, 0), memory_space=pltpu.SMEM),
                  pl.BlockSpec((tm, w), lambda i: (i, 0))],
        out_specs=pl.BlockSpec(memory_space=pl.ANY),
        scratch_shapes=[pltpu.SemaphoreType.DMA],
        compiler_params=pltpu.CompilerParams(dimension_semantics=("arbitrary",), vmem_limit_bytes=VMEM_LIMIT,
                                             has_side_effects=True),
        name="dispatch",
    )(pos.reshape(t // tm, 1, tm), hx)


def _moe_routed_kernel(ea_ref, eb_ref, valid_ref, hx_ref, w1a_ref, w3a_ref, w2a_ref, w1b_ref, w3b_ref, w2b_ref, y_ref):
    j = pl.program_id(0)
    nvalid = valid_ref[j]

    @pl.when(nvalid == 0)
    def _():
        y_ref[...] = jnp.zeros_like(y_ref)

    @pl.when(nvalid > 0)
    def _():
        tm, d = y_ref.shape
        row = lax.broadcasted_iota(jnp.int32, (tm, 1), 0)
        real = row < nvalid
        h = jnp.where(real, hx_ref[:, 0:d], 0.0).astype(BF16)
        wts = jnp.where(real, hx_ref[:, d:d + LANES], 0.0)

        def expert(w1_ref, w3_ref, w2_ref):
            a = _dot(h, w1_ref[...])
            g = (a * jax.nn.sigmoid(a)) * _dot(h, w3_ref[...])
            return _dot(g.astype(BF16), w2_ref[...])

        y_ref[...] = (wts[:, 0:1] * expert(w1a_ref, w3a_ref, w2a_ref)
                      + wts[:, 1:2] * expert(w1b_ref, w3b_ref, w2b_ref))


def _moe_routed(hx_sorted, ea, eb, valid, w1, w3, w2, tile_rows):
    n_slots, w = hx_sorted.shape
    ne, d, de = w1.shape
    up = lambda sel: pl.BlockSpec((None, d, de), lambda j, ea, eb, valid: ((ea, eb)[sel][j], 0, 0))
    down = lambda sel: pl.BlockSpec((None, de, d), lambda j, ea, eb, valid: ((ea, eb)[sel][j], 0, 0))
    grid_spec = pltpu.PrefetchScalarGridSpec(
        num_scalar_prefetch=3,
        grid=(n_slots // tile_rows,),
        in_specs=[pl.BlockSpec((tile_rows, w), lambda j, ea, eb, valid: (j, 0)),
                  up(0), up(0), down(0), up(1), up(1), down(1)],
        out_specs=pl.BlockSpec((tile_rows, d), lambda j, ea, eb, valid: (j, 0)),
    )
    return pl.pallas_call(
        _moe_routed_kernel,
        out_shape=jax.ShapeDtypeStruct((n_slots, d), F32),
        grid_spec=grid_spec,
        compiler_params=_cparams("arbitrary"),
        name="moe_routed",
    )(ea, eb, valid, hx_sorted, w1, w3, w2, w1, w3, w2)


def _combine_kernel(pos_ref, ys_ref, x1_ref, gt2_ref, y_ref, buf, sem):
    tm = buf.shape[0]

    def row_copy(r):
        return pltpu.make_async_copy(ys_ref.at[pl.ds(pos_ref[0, r], 1)], buf.at[pl.ds(r, 1)], sem)

    def start(r, carry):
        row_copy(r).start()
        return carry

    def wait(r, carry):
        row_copy(r).wait()
        return carry

    lax.fori_loop(0, tm, start, 0, unroll=8)
    lax.fori_loop(0, tm, wait, 0, unroll=8)
    y_ref[...] = x1_ref[...] + gt2_ref[...] * buf[...]


def _combine(pos, ys, x1, gt2, tm, rows_per_group):
    t, d = x1.shape
    return pl.pallas_call(
        _combine_kernel,
        out_shape=jax.ShapeDtypeStruct((t, d), F32),
        grid=(t // tm,),
        in_specs=[pl.BlockSpec((None, 1, tm), lambda i: (i, 0, 0), memory_space=pltpu.SMEM),
                  pl.BlockSpec(memory_space=pl.ANY),
                  pl.BlockSpec((tm, d), lambda i: (i, 0)),
                  _mod_spec(gt2, tm, rows_per_group, 1)],
        out_specs=pl.BlockSpec((tm, d), lambda i: (i, 0)),
        scratch_shapes=[pltpu.VMEM((tm, d), F32), pltpu.SemaphoreType.DMA],
        compiler_params=_cparams("arbitrary"),
        name="combine",
    )(pos.reshape(t // tm, 1, tm), ys, x1, gt2)


def _moe_kernel(h_ref, cmb_ref, x1_ref, gt2_ref, w1_ref, w3_ref, w2_ref, y_ref, acc_scr):
    e = pl.program_id(1)

    @pl.when(e == 0)
    def _():
        acc_scr[...] = jnp.zeros_like(acc_scr)

    h = h_ref[...]
    a = _dot(h, w1_ref[...])
    g = (a * jax.nn.sigmoid(a)) * _dot(h, w3_ref[...])
    ff = _dot(g.astype(BF16), w2_ref[...])
    cmb = cmb_ref[...]
    lane = lax.broadcasted_iota(jnp.int32, cmb.shape, 1)
    w = jnp.sum(jnp.where(lane == e, cmb, 0.0), axis=-1, keepdims=True)
    acc_scr[...] += w * ff

    @pl.when(e == pl.num_programs(1) - 1)
    def _():
        y_ref[...] = x1_ref[...] + gt2_ref[...] * acc_scr[...]


def _moe(h2, cmb, x1, gt2, w1, w3, w2, tm, rows_per_group):
    t, d = x1.shape
    ne, _, de = w1.shape
    gt2_spec = (pl.BlockSpec((None, 1, d), lambda i, e: (i // (rows_per_group // tm), 0, 0))
                if gt2.ndim == 3 else pl.BlockSpec((tm, d), lambda i, e: (i, 0)))
    return pl.pallas_call(
        _moe_kernel,
        out_shape=jax.ShapeDtypeStruct((t, d), F32),
        grid=(t // tm, ne),
        in_specs=[pl.BlockSpec((tm, d), lambda i, e: (i, 0)),
                  pl.BlockSpec((tm, LANES), lambda i, e: (i, 0)),
                  pl.BlockSpec((tm, d), lambda i, e: (i, 0)),
                  gt2_spec,
                  pl.BlockSpec((None, d, de), lambda i, e: (e, 0, 0)),
                  pl.BlockSpec((None, d, de), lambda i, e: (e, 0, 0)),
                  pl.BlockSpec((None, de, d), lambda i, e: (e, 0, 0))],
        out_specs=pl.BlockSpec((tm, d), lambda i, e: (i, 0)),
        scratch_shapes=[pltpu.VMEM((tm, d), F32)],
        compiler_params=_cparams("arbitrary", "arbitrary"),
        name="moe",
    )(h2, cmb, x1, gt2, w1, w3, w2)


def _rope_tables(pos, gq, gk):
    posf = pos.astype(F32)[:, None]
    inv_r = ROPE_THETA ** (-jnp.arange(0, RET_DK, 2, dtype=F32) / RET_DK)
    ang_r = posf * inv_r[None, :]
    inv_d = ROPE_THETA ** (-jnp.arange(0, DIFF_DH, 2, dtype=F32) / DIFF_DH)
    lane = jnp.arange(LANES)
    half = DIFF_DH // 2
    ang_d = posf * inv_d[lane % half][None, :]
    first_half = ((lane % DIFF_DH) < half)[None, :]
    cos_d, sin_d = jnp.cos(ang_d), jnp.sin(ang_d)

    def diff_tables(g, scale):
        g128 = jnp.tile(g, LANES // DIFF_DH)
        return (cos_d * (g128 * scale)[None, :],
                jnp.where(first_half, -sin_d, 0.0) * (jnp.roll(g128, -half) * scale)[None, :],
                jnp.where(first_half, 0.0, sin_d) * (jnp.roll(g128, half) * scale)[None, :])

    return (jnp.cos(ang_r), jnp.sin(ang_r)) + diff_tables(gq, DIFF_DH ** -0.5 * LOG2E) + diff_tables(gk, 1.0)


def _tile_rows(n, target):
    tm = min(n, target)
    assert n % tm == 0
    return tm


def _layer(x, mod, pos, s0, kc, vc, lw, li, ret_chunk):
    nb, seq, d = x.shape
    t = nb * seq
    x2d = x.reshape(t, d)
    per_token = seq < 128
    if per_token:
        mods = [jnp.repeat(mod[:, k, :], seq, axis=0) for k in range(6)]
        tabs = tuple(jnp.tile(tb, (nb, 1)) for tb in _rope_tables(pos, lw['gq'], lw['gk']))
        rows_per_group = t
    else:
        mods = [mod[:, k:k + 1, :] for k in range(6)]
        tabs = _rope_tables(pos, lw['gq'], lw['gk'])
        rows_per_group = seq
    sh1, sc1, gt1, sh2, sc2, gt2 = mods
    tm = _tile_rows(rows_per_group, 512)

    p, qd, k2d, v2d = _inproj(x2d, sc1, sh1, lw['norm1_g'], lw['w_in'], tabs, lw['bd'],
                              _tile_rows(rows_per_group, 1024), rows_per_group)
    ret_o, s_new = _retention(p, s0, nb, seq, ret_chunk)
    lam_init = 0.8 - 0.6 * math.exp(-0.3 * li)
    if kc is None:
        dif_o = _diffattn(qd, k2d, v2d, lw['lam'], lw['subln_g'], nb, seq, lam_init, tq=min(seq, 512))
    else:
        dif_o = _diffattn_cached(qd, k2d, v2d, kc, vc, li, lw['lam'], lw['subln_g'], nb, seq, lam_init)
    merge_args = (ret_o, dif_o, p, x2d, gt1, sc2, sh2, lw['norm2_g'], lw['w_br_ret'], lw['w_br_diff'],
                  lw['w_out'], lw['rw_hi'], lw['rw_lo'], lw['rb'], tm, rows_per_group)
    if per_token:
        x1, h2, cmb = _merge(*merge_args, routed=False)
        y = _moe(h2, cmb, x1, gt2, lw['w1'], lw['w3'], lw['w2'], tm, rows_per_group)
    else:
        x1, hx, cls, hist = _merge(*merge_args, routed=True)
        n_tiles = SUBLANES * pl.cdiv(t // MOE_TILE_ROWS + N_CLASSES, SUBLANES)
        pos, tinfo = _route(cls, hist, MOE_TILE_ROWS, n_tiles, tm)
        hx_sorted = _dispatch(pos, hx, n_tiles * MOE_TILE_ROWS, tm)
        y_sorted = _moe_routed(hx_sorted, tinfo[:, 0], tinfo[:, 1], tinfo[:, 2],
                               lw['w1'], lw['w3'], lw['w2'], MOE_TILE_ROWS)
        y = _combine(pos, y_sorted, x1, gt2, tm, rows_per_group)
    return (y.reshape(nb, seq, d), s_new, k2d.reshape(nb, seq, DIFF_HEADS, 2 * DIFF_DH),
            v2d.reshape(nb, seq, DIFF_HEADS, DIFF_DV))


def kernel(x_prompt, x_sample, c_prompt, c_sample, cache_diff_k, cache_diff_v, state_ret, w_ada, b_ada, norm1_g, w_in, diff_qnorm_g, diff_knorm_g, diff_lambda_q1, diff_lambda_k1, diff_lambda_q2, diff_lambda_k2, diff_subln_g, w_br_ret, w_br_diff, w_out, norm2_g, w_group, b_group, w_expert_router, b_expert, w1, w3, w2):
    nb, seq, d = x_prompt.shape
    nbs, seqs, _ = x_sample.shape
    depth = w_in.shape[0]
    past = cache_diff_k.shape[2]
    pos_p = jnp.arange(seq, dtype=jnp.int32)
    pos_s = past + jnp.arange(seqs, dtype=jnp.int32)
    c_all = jnp.concatenate([c_prompt, c_sample], axis=0)
    col = jnp.arange(MXU_COLS)
    bd = jnp.where((col[:, None] // DIFF_DH) == (col[None, :] // DIFF_DH), 1.0 / DIFF_DH, 0.0).astype(BF16)

    xp, xs = x_prompt, x_sample
    outs = [[] for _ in range(6)]
    for li in range(depth):
        rw = jnp.zeros((d, LANES), F32)
        rw = rw.at[:, :N_GROUPS].set(w_group[li]).at[:, N_GROUPS:N_GROUPS + N_EXPERTS].set(w_expert_router[li])
        rw_hi = rw.astype(BF16)
        rb = jnp.zeros((1, LANES), F32)
        rb = rb.at[0, :N_GROUPS].set(b_group[li]).at[0, N_GROUPS:N_GROUPS + N_EXPERTS].set(b_expert[li])
        lw = {
            'norm1_g': norm1_g[li].reshape(1, d), 'norm2_g': norm2_g[li].reshape(1, d),
            'w_in': jnp.stack([w_in[li, :, blk * COL_BLOCK:(blk + 1) * COL_BLOCK] for blk in IN_BLOCK_ORDER]
                              ).astype(BF16),
            'gq': diff_qnorm_g[li], 'gk': diff_knorm_g[li],
            'bd': bd,
            'lam': jnp.stack([diff_lambda_q1[li], diff_lambda_k1[li], diff_lambda_q2[li], diff_lambda_k2[li]]),
            'subln_g': diff_subln_g[li].reshape(1, DIFF_DV),
            'w_br_ret': w_br_ret[li].astype(BF16), 'w_br_diff': w_br_diff[li].astype(BF16),
            'w_out': w_out[li].astype(BF16),
            'rw_hi': rw_hi, 'rw_lo': (rw - rw_hi.astype(F32)).astype(BF16), 'rb': rb,
            'w1': w1[li].astype(BF16), 'w3': w3[li].astype(BF16), 'w2': w2[li].astype(BF16),
        }
        mod = _adaln(c_all, w_ada[li], b_ada[li]).reshape(nb + nbs, 6, d)
        xp, sp, kp, vp = _layer(xp, mod[:nb], pos_p, None, None, None, lw, li, ret_chunk=min(seq, 256))
        xs, ss, kss, vss = _layer(xs, mod[nb:], pos_s, state_ret[li], cache_diff_k, cache_diff_v,
                                  lw, li, ret_chunk=seqs)
        for lst, val in zip(outs, (kp, vp, sp, kss, vss, ss)):
            lst.append(val)
    kp, vp, sp, kss, vss, ss = (jnp.stack(lst) for lst in outs)
    return (xp, xs, kp, vp, sp.astype(x_prompt.dtype), kss, vss, ss.astype(state_ret.dtype))
```

```python
import functools
import math

import jax
import jax.numpy as jnp
from jax import lax
from jax.experimental import pallas as pl
from jax.experimental.pallas import tpu as pltpu

F32 = jnp.float32
BF16 = jnp.bfloat16

D_MODEL = 1024
CHUNK = 64
ROPE_THETA = 10000.0
EPS = 1e-6
RET_HEADS = 4
RET_DK = D_MODEL // RET_HEADS
RET_DV = 2 * RET_DK
DIFF_HEADS = 8
DIFF_DH = D_MODEL // (2 * DIFF_HEADS)
DIFF_DV = 2 * DIFF_DH
N_GROUPS = 4
EXPERTS_PER_GROUP = 4
N_EXPERTS = N_GROUPS * EXPERTS_PER_GROUP
D_EXPERT = 512
N_PAIRS = EXPERTS_PER_GROUP * (EXPERTS_PER_GROUP - 1) // 2
N_CLASSES = N_GROUPS * N_PAIRS
MOE_TILE_ROWS = 512
DIAG_PARTS = 2
MERGE_ROW_PARTS = 1
DMA_UNROLL = 8

LANES = 128
SUBLANES = 8
MXU_COLS = 256
COL_BLOCK = 1024
IN_BLOCK_ORDER = (0, 1, 2, 3, 4, 5, 6, 9, 10, 7, 8)
N_SLAB_BLOCKS = 8
SLAB_RQ, SLAB_RK, SLAB_RV, SLAB_RG, SLAB_GATE = 0, 1, 2, 4, 6
VMEM_LIMIT = 56 * 1024 * 1024
LOG2E = math.log2(math.e)


def _cparams(*sem):
    return pltpu.CompilerParams(dimension_semantics=sem, vmem_limit_bytes=VMEM_LIMIT)


def _dot(a, b):
    return jnp.dot(a, b, preferred_element_type=F32)


def _dot_nt(a, b):
    return lax.dot_general(a, b, (((1,), (1,)), ((), ())), preferred_element_type=F32)


def _adaln_kernel(c_ref, w_ref, b_ref, o_ref):
    c = c_ref[...]
    a = c * jax.nn.sigmoid(c)
    o_ref[...] = jnp.dot(a, w_ref[...], preferred_element_type=F32,
                         precision=lax.Precision.HIGHEST) + b_ref[...]


def _adaln(c, w_ada, b_ada):
    rows, d = c.shape
    n = w_ada.shape[1]
    tn = 1536
    return pl.pallas_call(
        _adaln_kernel,
        out_shape=jax.ShapeDtypeStruct((rows, n), F32),
        grid=(n // tn,),
        in_specs=[pl.BlockSpec((rows, d), lambda j: (0, 0)),
                  pl.BlockSpec((d, tn), lambda j: (0, j)),
                  pl.BlockSpec((1, tn), lambda j: (0, j))],
        out_specs=pl.BlockSpec((rows, tn), lambda j: (0, j)),
        compiler_params=_cparams("arbitrary"),
        name="adaln",
    )(c, w_ada, b_ada.reshape(1, n))


def _rms(x):
    return x * lax.rsqrt(jnp.mean(x * x, axis=-1, keepdims=True) + EPS)


def _inproj_kernel(x_ref, sc_ref, sh_ref, g1_ref, w_ref, rc_ref, rs_ref, qc_ref, qsn_ref, qsp_ref,
                   kc_ref, ksn_ref, ksp_ref, bd_ref, p_ref, qd_ref, k_ref, v_ref, h_scr):
    j = pl.program_id(1)

    @pl.when(j == 0)
    def _():
        h = _rms(x_ref[...]) * g1_ref[...]
        h = h * (1.0 + sc_ref[...]) + sh_ref[...]
        h_scr[...] = h.astype(BF16)

    def strips():
        h = h_scr[...]
        for lo in range(0, COL_BLOCK, MXU_COLS):
            yield lo, _dot(h, w_ref[:, lo:lo + MXU_COLS])

    def ret_rope(scale):
        c = rc_ref[...]
        s = rs_ref[...]
        for lo, acc in strips():
            x1 = acc[:, 0:LANES]
            x2 = acc[:, LANES:2 * LANES]
            p_ref[:, lo:lo + LANES] = ((x1 * c - x2 * s) * scale).astype(BF16)
            p_ref[:, lo + LANES:lo + 2 * LANES] = ((x1 * s + x2 * c) * scale).astype(BF16)

    def diff_norm():
        for lo, acc in list(strips()):
            k_ref[:, lo:lo + MXU_COLS] = acc * lax.rsqrt(_dot((acc * acc).astype(BF16), bd_ref[...]) + EPS)

    def diff_rope(lo, c_ref, sn_ref, sp_ref, out_ref):
        for off in range(lo, lo + MXU_COLS, LANES):
            yg = k_ref[:, off:off + LANES]
            o = (yg * c_ref[...] + pltpu.roll(yg, LANES - DIFF_DH // 2, 1) * sn_ref[...]
                 + pltpu.roll(yg, DIFF_DH // 2, 1) * sp_ref[...])
            out_ref[:, off:off + LANES] = o.astype(out_ref.dtype)

    def elementwise(fn, out_ref, after_strip=None):
        for lo, acc in strips():
            out_ref[:, lo:lo + MXU_COLS] = fn(acc).astype(out_ref.dtype)
            if after_strip is not None:
                after_strip(lo)

    @pl.when(j == 0)
    def _():
        ret_rope(1.0)

    @pl.when(j == 1)
    def _():
        ret_rope(RET_DK ** -0.5)

    @pl.when((j == 2) | (j == 3))
    def _():
        elementwise(lambda a: a, p_ref)

    @pl.when((j == 4) | (j == 5))
    def _():
        elementwise(lambda a: a * jax.nn.sigmoid(a), p_ref)

    @pl.when((j == 6) | (j == 9))
    def _():
        diff_norm()

    @pl.when(j == 7)
    def _():
        elementwise(jax.nn.sigmoid, p_ref, lambda lo: diff_rope(lo, qc_ref, qsn_ref, qsp_ref, qd_ref))

    @pl.when(j == 8)
    def _():
        elementwise(jax.nn.sigmoid, p_ref)

    @pl.when(j == 10)
    def _():
        elementwise(lambda a: a, v_ref, lambda lo: diff_rope(lo, kc_ref, ksn_ref, ksp_ref, k_ref))


def _mod_spec(arr, tm, rows_per_group, ngrid):
    d = arr.shape[-1]
    if arr.ndim == 3:
        tiles_per_group = rows_per_group // tm
        if ngrid == 2:
            return pl.BlockSpec((None, 1, d), lambda i, j: (i // tiles_per_group, 0, 0))
        return pl.BlockSpec((None, 1, d), lambda i: (i // tiles_per_group, 0, 0))
    if ngrid == 2:
        return pl.BlockSpec((tm, d), lambda i, j: (i, 0))
    return pl.BlockSpec((tm, d), lambda i: (i, 0))


def _inproj(x2d, sc1, sh1, g1, w_blocks, tabs, bd, tm, rows_per_group):
    t, d = x2d.shape
    nrep = tabs[0].shape[0] // tm

    def slab_map(i, j):
        return (i, jnp.where(j < 6, j, jnp.where(j < 9, jnp.maximum(j - 1, 5), N_SLAB_BLOCKS - 1)))

    tab_spec = pl.BlockSpec((tm, LANES), lambda i, j: (i % nrep, 0))
    row_spec = pl.BlockSpec((tm, d), lambda i, j: (i, 0))
    return pl.pallas_call(
        _inproj_kernel,
        out_shape=(jax.ShapeDtypeStruct((t, N_SLAB_BLOCKS * COL_BLOCK), BF16),
                   jax.ShapeDtypeStruct((t, d), BF16),
                   jax.ShapeDtypeStruct((t, d), F32),
                   jax.ShapeDtypeStruct((t, d), F32)),
        grid=(t // tm, len(IN_BLOCK_ORDER)),
        in_specs=[row_spec,
                  _mod_spec(sc1, tm, rows_per_group, 2),
                  _mod_spec(sh1, tm, rows_per_group, 2),
                  pl.BlockSpec((1, d), lambda i, j: (0, 0)),
                  pl.BlockSpec((None, d, COL_BLOCK), lambda i, j: (j, 0, 0)),
                  *([tab_spec] * len(tabs)),
                  pl.BlockSpec((MXU_COLS, MXU_COLS), lambda i, j: (0, 0))],
        out_specs=(pl.BlockSpec((tm, COL_BLOCK), slab_map), row_spec, row_spec, row_spec),
        scratch_shapes=[pltpu.VMEM((tm, d), BF16)],
        compiler_params=_cparams("arbitrary", "arbitrary"),
        name="inproj",
    )(x2d, sc1, sh1, g1, w_blocks, *tabs, bd)


def _retention_kernel(*refs, has_state, chunk):
    if has_state:
        q_ref, k_ref, v_ref, rg_ref, dm_ref, dv_ref, s0_ref, o_ref, sout_ref = refs
        s = s0_ref[...]
    else:
        q_ref, k_ref, v_ref, rg_ref, dm_ref, dv_ref, o_ref, sout_ref = refs
        s = jnp.zeros(sout_ref.shape, F32)
    dvec = dv_ref[...]
    q_decay = dvec[:, 0:1]
    k_decay = dvec[:, 1:2]
    chunk_decay = dvec[0:1, 2:3]

    for c in range(q_ref.shape[0] // chunk):
        rows = slice(c * chunk, (c + 1) * chunk)
        q = q_ref[rows, :]
        k = k_ref[rows, :]
        v = v_ref[rows, :]
        scores = (_dot_nt(q, k) * dm_ref[...]).astype(BF16)
        o = _dot(scores, v) + _dot(q, s.astype(BF16)) * q_decay
        o = _rms(o) * rg_ref[rows, :].astype(F32)
        o_ref[rows, :] = o.astype(BF16)
        kd_t = (k.astype(F32) * k_decay).T.astype(BF16)
        s = s * chunk_decay + _dot(kd_t, v)
    sout_ref[...] = s


def _retention(p, s0, nb, seq, chunk):
    t = p.shape[0]
    idx = jnp.arange(chunk, dtype=F32)
    log_gamma = jnp.log1p(-jnp.exp2(-5.0 - jnp.arange(RET_HEADS, dtype=F32)))
    rel = idx[:, None] - idx[None, :]
    dmat = jnp.where(rel >= 0, jnp.exp(jnp.maximum(rel, 0.0)[None] * log_gamma[:, None, None]), 0.0)
    q_decay = jnp.exp((idx + 1.0)[None, :] * log_gamma[:, None])
    k_decay = jnp.exp((chunk - 1.0 - idx)[None, :] * log_gamma[:, None])
    c_decay = jnp.broadcast_to(jnp.exp(chunk * log_gamma)[:, None], (RET_HEADS, chunk))
    dvec = jnp.zeros((RET_HEADS, chunk, LANES), F32)
    dvec = dvec.at[:, :, 0].set(q_decay).at[:, :, 1].set(k_decay).at[:, :, 2].set(c_decay)

    kcols, vcols = COL_BLOCK // RET_DK, COL_BLOCK // RET_DV
    in_specs = [pl.BlockSpec((seq, RET_DK), lambda b, h: (b, SLAB_RQ * kcols + h)),
                pl.BlockSpec((seq, RET_DK), lambda b, h: (b, SLAB_RK * kcols + h)),
                pl.BlockSpec((seq, RET_DV), lambda b, h: (b, SLAB_RV * vcols + h)),
                pl.BlockSpec((seq, RET_DV), lambda b, h: (b, SLAB_RG * vcols + h)),
                pl.BlockSpec((None, chunk, chunk), lambda b, h: (h, 0, 0)),
                pl.BlockSpec((None, chunk, LANES), lambda b, h: (h, 0, 0))]
    args = [p, p, p, p, dmat, dvec]
    state_spec = pl.BlockSpec((None, None, RET_DK, RET_DV), lambda b, h: (b, h, 0, 0))
    if s0 is not None:
        in_specs.append(state_spec)
        args.append(s0)
    return pl.pallas_call(
        functools.partial(_retention_kernel, has_state=s0 is not None, chunk=chunk),
        out_shape=(jax.ShapeDtypeStruct((t, RET_HEADS * RET_DV), BF16),
                   jax.ShapeDtypeStruct((nb, RET_HEADS, RET_DK, RET_DV), F32)),
        grid=(nb, RET_HEADS),
        in_specs=in_specs,
        out_specs=(pl.BlockSpec((seq, RET_DV), lambda b, h: (b, h)), state_spec),
        compiler_params=_cparams("arbitrary", "arbitrary"),
        name="retention",
    )(*args)


def _lambda(lam_ref, lam_init):
    l1 = jnp.sum(lam_ref[0:1, :] * lam_ref[1:2, :], axis=-1, keepdims=True)
    l2 = jnp.sum(lam_ref[2:3, :] * lam_ref[3:4, :], axis=-1, keepdims=True)
    return jnp.exp(l1) - jnp.exp(l2) + lam_init


def _split_maps(q):
    lane = lax.broadcasted_iota(jnp.int32, q.shape, 1)
    zero = jnp.zeros_like(q)
    return jnp.where(lane < DIFF_DH, q, zero), jnp.where(lane >= DIFF_DH, q, zero)


def _diffattn_kernel(q_ref, k_ref, v_ref, lam_ref, g_ref, o_ref, kb_scr, vb_scr, m0_scr, m1_scr, acc0_scr, acc1_scr,
                     *, tq, lam_init):
    seq = k_ref.shape[0]
    nq = seq // tq
    kb_scr[...] = k_ref[...].astype(BF16)
    vb_scr[:, 0:DIFF_DV] = v_ref[...].astype(BF16)
    vb_scr[:, DIFF_DV:2 * DIFF_DV] = jnp.ones((seq, DIFF_DV), BF16)
    lam = _lambda(lam_ref, lam_init)
    m_refs = (m0_scr, m1_scr)
    acc_refs = (acc0_scr, acc1_scr)
    part = tq // DIAG_PARTS

    items = []
    for qi in range(nq):
        base = qi * tq
        for kt in range(qi):
            items.append((base, tq, kt * tq, tq, kt == 0, False))
        for r in range(DIAG_PARTS):
            items.append((base + r * part, part, base, (r + 1) * part, qi == 0, r == DIAG_PARTS - 1))

    def scores(q0, qn, k0, kn):
        qmaps = _split_maps(q_ref[q0:q0 + qn, :])
        kblk = kb_scr[k0:k0 + kn, :]
        needs_mask = (k0 + kn - 1) // CHUNK > q0 // CHUNK
        if needs_mask:
            row = q0 + lax.broadcasted_iota(jnp.int32, (qn, kn), 0)
            col = k0 + lax.broadcasted_iota(jnp.int32, (qn, kn), 1)
            visible = (col // CHUNK) <= (row // CHUNK)
        out = []
        for mp in range(2):
            sc = _dot_nt(qmaps[mp], kblk)
            out.append(jnp.where(visible, sc, -1e30) if needs_mask else sc)
        return out

    pending = scores(*items[0][:4])
    for idx, (q0, qn, k0, kn, first, last) in enumerate(items):
        scs = pending
        if idx + 1 < len(items):
            pending = scores(*items[idx + 1][:4])
        vblk = vb_scr[k0:k0 + kn, :]
        r0 = q0 % tq
        for mp in range(2):
            sc = scs[mp]
            if first:
                m_new = jnp.max(sc, axis=-1, keepdims=True)
                acc_refs[mp][r0:r0 + qn, :] = _dot(jnp.exp2(sc - m_new).astype(BF16), vblk)
            else:
                m_old = m_refs[mp][r0:r0 + qn, :]
                m_new = jnp.maximum(m_old, jnp.max(sc, axis=-1, keepdims=True))
                p = jnp.exp2(sc - m_new)
                acc_refs[mp][r0:r0 + qn, :] = (jnp.exp2(m_old - m_new) * acc_refs[mp][r0:r0 + qn, :]
                                               + _dot(p.astype(BF16), vblk))
            m_refs[mp][r0:r0 + qn, :] = m_new
        if last:
            a1 = acc0_scr[...]
            a2 = acc1_scr[...]
            o = (a1[:, 0:DIFF_DV] / a1[:, DIFF_DV:2 * DIFF_DV]
                 - lam * (a2[:, 0:DIFF_DV] / a2[:, DIFF_DV:2 * DIFF_DV]))
            o = _rms(o) * g_ref[...] * (1.0 - lam_init)
            qt = q0 + qn - tq
            o_ref[qt:qt + tq, :] = o.astype(BF16)


def _diffattn(qd, k2d, v2d, lam_vecs, subln_g, nb, seq, lam_init, tq):
    t = qd.shape[0]
    return pl.pallas_call(
        functools.partial(_diffattn_kernel, tq=tq, lam_init=lam_init),
        out_shape=jax.ShapeDtypeStruct((t, DIFF_HEADS * DIFF_DV), BF16),
        grid=(nb, DIFF_HEADS),
        in_specs=[pl.BlockSpec((seq, DIFF_DV), lambda b, h: (b, h)),
                  pl.BlockSpec((seq, DIFF_DV), lambda b, h: (b, h)),
                  pl.BlockSpec((seq, DIFF_DV), lambda b, h: (b, h)),
                  pl.BlockSpec((4, DIFF_DH), lambda b, h: (0, 0)),
                  pl.BlockSpec((1, DIFF_DV), lambda b, h: (0, 0))],
        out_specs=pl.BlockSpec((seq, DIFF_DV), lambda b, h: (b, h)),
        scratch_shapes=[pltpu.VMEM((seq, DIFF_DV), BF16), pltpu.VMEM((seq, 2 * DIFF_DV), BF16),
                        pltpu.VMEM((tq, 1), F32), pltpu.VMEM((tq, 1), F32),
                        pltpu.VMEM((tq, 2 * DIFF_DV), F32), pltpu.VMEM((tq, 2 * DIFF_DV), F32)],
        compiler_params=_cparams("arbitrary", "arbitrary"),
        name="diffattn",
    )(qd, k2d, v2d, lam_vecs, subln_g)


def _diffattn_cached_kernel(q_ref, kc_ref, vc_ref, kn_ref, vn_ref, lam_ref, g_ref, o_ref, k_scr, v_scr,
                            *, past, new, lam_init):
    total = k_scr.shape[0]
    pad = jnp.zeros((total - past, DIFF_DV), BF16)
    qpos = past + lax.broadcasted_iota(jnp.int32, (new, total), 0)
    kpos = lax.broadcasted_iota(jnp.int32, (new, total), 1)
    visible = ((kpos // CHUNK) <= (qpos // CHUNK)) & (kpos < past + new)
    lam = _lambda(lam_ref, lam_init)

    for h in range(DIFF_HEADS):
        cols = slice(h * DIFF_DV, (h + 1) * DIFF_DV)
        k_scr[0:past, :] = kc_ref[:, h, :].astype(BF16)
        v_scr[0:past, :] = vc_ref[:, h, :].astype(BF16)
        k_scr[past:total, :] = pad
        v_scr[past:total, :] = pad
        k_scr[past:past + new, :] = kn_ref[:, cols].astype(BF16)
        v_scr[past:past + new, :] = vn_ref[:, cols].astype(BF16)
        kall = k_scr[...]
        vall = v_scr[...]

        def softmax_av(qm):
            sc = jnp.where(visible, _dot_nt(qm, kall), -1e30)
            m = jnp.max(sc, axis=-1, keepdims=True)
            p = jnp.exp2(sc - m)
            return _dot(p.astype(BF16), vall) / jnp.sum(p, axis=-1, keepdims=True)

        qa, qb = _split_maps(q_ref[:, cols])
        o = softmax_av(qa) - lam * softmax_av(qb)
        o = _rms(o) * g_ref[...] * (1.0 - lam_init)
        o_ref[:, cols] = o.astype(BF16)


def _diffattn_cached(qd, k2d, v2d, kc, vc, li, lam_vecs, subln_g, nb, new, lam_init):
    t = qd.shape[0]
    past = kc.shape[2]
    width = DIFF_HEADS * DIFF_DV
    total = past + LANES * pl.cdiv(new, LANES)
    cache_spec = pl.BlockSpec((None, None, past, DIFF_HEADS, DIFF_DV), lambda b: (li, b, 0, 0, 0))
    return pl.pallas_call(
        functools.partial(_diffattn_cached_kernel, past=past, new=new, lam_init=lam_init),
        out_shape=jax.ShapeDtypeStruct((t, width), BF16),
        grid=(nb,),
        in_specs=[pl.BlockSpec((new, width), lambda b: (b, 0)),
                  cache_spec, cache_spec,
                  pl.BlockSpec((new, width), lambda b: (b, 0)),
                  pl.BlockSpec((new, width), lambda b: (b, 0)),
                  pl.BlockSpec((4, DIFF_DH), lambda b: (0, 0)),
                  pl.BlockSpec((1, DIFF_DV), lambda b: (0, 0))],
        out_specs=pl.BlockSpec((new, width), lambda b: (b, 0)),
        scratch_shapes=[pltpu.VMEM((total, DIFF_DV), BF16), pltpu.VMEM((total, DIFF_DV), BF16)],
        compiler_params=_cparams("arbitrary"),
        name="diffattn_cached",
    )(qd, kc, vc, k2d, v2d, lam_vecs, subln_g)


def _merge_kernel(ret_ref, dif_ref, gr_ref, gd_ref, x_ref, gt1_ref, sc2_ref, sh2_ref, g2_ref,
                  wr_ref, wd_ref, wo_ref, rwh_ref, rwl_ref, rb_ref, x1_ref, *out_refs, routed):
    tm, d = x_ref.shape
    n_parts = MERGE_ROW_PARTS if tm % (MERGE_ROW_PARTS * LANES) == 0 else 1
    part = tm // n_parts

    def mod_rows(ref, rows):
        return ref[rows, :] if ref.shape[0] > 1 else ref[...]

    hist = jnp.zeros((1, LANES), F32)
    for r in range(n_parts):
        rows = slice(r * part, (r + 1) * part)
        a = _dot(ret_ref[rows, :], wr_ref[...])
        b = _dot(dif_ref[rows, :], wd_ref[...])
        merged = gr_ref[rows, :].astype(F32) * a + gd_ref[rows, :].astype(F32) * b
        x1 = x_ref[rows, :] + mod_rows(gt1_ref, rows) * _dot(merged.astype(BF16), wo_ref[...])
        x1_ref[rows, :] = x1
        h2 = _rms(x1) * g2_ref[...]
        h2 = h2 * (1.0 + mod_rows(sc2_ref, rows)) + mod_rows(sh2_ref, rows)

        hi = h2.astype(BF16)
        lo = (h2 - hi.astype(F32)).astype(BF16)
        logits = _dot(hi, rwh_ref[...]) + _dot(lo, rwh_ref[...]) + _dot(hi, rwl_ref[...]) + rb_ref[...]

        lane = lax.broadcasted_iota(jnp.int32, logits.shape, 1)
        big = jnp.int32(4 * LANES)

        def masked_argmax(mask):
            vmax = jnp.max(jnp.where(mask, logits, -jnp.inf), axis=-1, keepdims=True)
            idx = jnp.min(jnp.where(mask & (logits == vmax), lane, big), axis=-1, keepdims=True)
            return vmax, idx

        is_group = lane < N_GROUPS
        gmax, gidx = masked_argmax(is_group)
        gprob = 1.0 / jnp.sum(jnp.where(is_group, jnp.exp(logits - gmax), 0.0), axis=-1, keepdims=True)
        first = N_GROUPS + gidx * EXPERTS_PER_GROUP
        in_group = (lane >= first) & (lane < first + EXPERTS_PER_GROUP)
        v1, i1 = masked_argmax(in_group)
        v2, i2 = masked_argmax(in_group & (lane != i1))
        e2 = jnp.exp(v2 - v1)
        w1 = gprob / (1.0 + e2)
        w2 = gprob * e2 / (1.0 + e2)
        if not routed:
            h2_ref, cmb_ref = out_refs
            h2_ref[rows, :] = h2.astype(BF16)
            cmb_ref[rows, :] = (jnp.where(lane + N_GROUPS == i1, w1, 0.0)
                                + jnp.where(lane + N_GROUPS == i2, w2, 0.0))
            continue

        hx_ref, cls_ref, hist_ref = out_refs
        first_is_lo = i1 < i2
        lo_e = jnp.minimum(i1, i2) - first
        hi_e = jnp.maximum(i1, i2) - first
        cls = gidx * N_PAIRS + jnp.right_shift(lo_e * (7 - lo_e), 1) + hi_e - lo_e - 1
        hx_ref[rows, 0:d] = h2
        hx_ref[rows, d:d + LANES] = jnp.where(lane == 0, jnp.where(first_is_lo, w1, w2),
                                              jnp.where(lane == 1, jnp.where(first_is_lo, w2, w1), 0.0))
        cls_ref[rows, :] = cls
        hist = hist + jnp.sum(jnp.where(lane == cls, 1.0, 0.0), axis=0, keepdims=True)

    if routed:
        hist_ref = out_refs[2]

        @pl.when(pl.program_id(0) == 0)
        def _():
            hist_ref[...] = jnp.zeros_like(hist_ref)

        hist_ref[...] += hist


def _merge(ret_o, dif_o, p, x2d, gt1, sc2, sh2, g2, wr, wd, wo, rwh, rwl, rb, tm, rows_per_group, routed):
    t, d = x2d.shape
    const = lambda shape: pl.BlockSpec(shape, lambda i: (0, 0), pipeline_mode=pl.Buffered(1))
    if routed:
        out_shape = (jax.ShapeDtypeStruct((t, d), F32),
                     jax.ShapeDtypeStruct((t, d + LANES), F32),
                     jax.ShapeDtypeStruct((t, 1), jnp.int32),
                     jax.ShapeDtypeStruct((1, LANES), F32))
        out_specs = (pl.BlockSpec((tm, d), lambda i: (i, 0)),
                     pl.BlockSpec((tm, d + LANES), lambda i: (i, 0)),
                     pl.BlockSpec((tm, 1), lambda i: (i, 0)),
                     pl.BlockSpec((1, LANES), lambda i: (0, 0)))
    else:
        out_shape = (jax.ShapeDtypeStruct((t, d), F32),
                     jax.ShapeDtypeStruct((t, d), BF16),
                     jax.ShapeDtypeStruct((t, LANES), F32))
        out_specs = (pl.BlockSpec((tm, d), lambda i: (i, 0)),
                     pl.BlockSpec((tm, d), lambda i: (i, 0)),
                     pl.BlockSpec((tm, LANES), lambda i: (i, 0)))
    return pl.pallas_call(
        functools.partial(_merge_kernel, routed=routed),
        out_shape=out_shape,
        grid=(t // tm,),
        in_specs=[pl.BlockSpec((tm, ret_o.shape[1]), lambda i: (i, 0)),
                  pl.BlockSpec((tm, d), lambda i: (i, 0)),
                  pl.BlockSpec((tm, d), lambda i: (i, SLAB_GATE)),
                  pl.BlockSpec((tm, d), lambda i: (i, SLAB_GATE + 1)),
                  pl.BlockSpec((tm, d), lambda i: (i, 0)),
                  _mod_spec(gt1, tm, rows_per_group, 1),
                  _mod_spec(sc2, tm, rows_per_group, 1),
                  _mod_spec(sh2, tm, rows_per_group, 1),
                  const((1, d)),
                  const(wr.shape), const(wd.shape), const(wo.shape),
                  const(rwh.shape), const(rwl.shape), const(rb.shape)],
        out_specs=out_specs,
        compiler_params=_cparams("arbitrary"),
        name="merge",
    )(ret_o, dif_o, p, p, x2d, gt1, sc2, sh2, g2, wr, wd, wo, rwh, rwl, rb)


def _route_kernel(cls_ref, hist_ref, tri_ref, pos_ref, tinfo_ref, carry_scr, off_scr, *, tile_rows):
    i = pl.program_id(0)

    def excl_cumsum(v):
        lane8 = lax.broadcasted_iota(jnp.int32, v.shape, 1)
        inc = v
        for sh in (1, 2, 4, 8, 16, 32, 64):
            inc = inc + jnp.where(lane8 >= sh, pltpu.roll(inc, sh, 1), 0.0)
        return inc - v

    @pl.when(i == 0)
    def _():
        cnt = hist_ref[...]
        ntile = jnp.floor((cnt + (tile_rows - 1)) * (1.0 / tile_rows))
        start = excl_cumsum(jnp.broadcast_to(ntile, (SUBLANES, LANES)))[0:1, :]
        off_scr[...] = start * tile_rows
        carry_scr[...] = jnp.zeros_like(carry_scr)
        nt = tinfo_ref.shape[0]
        jrow = lax.broadcasted_iota(jnp.int32, (nt, LANES), 0).astype(F32)
        lane = lax.broadcasted_iota(jnp.int32, (nt, LANES), 1)
        is_cls = lane < N_CLASSES
        tcls = jnp.sum(jnp.where(is_cls & (start <= jrow), 1.0, 0.0), axis=-1, keepdims=True) - 1.0
        pick = lane == tcls.astype(jnp.int32)
        cnt_j = jnp.sum(jnp.where(pick, cnt, 0.0), axis=-1, keepdims=True)
        start_j = jnp.sum(jnp.where(pick, start, 0.0), axis=-1, keepdims=True)
        valid = jnp.clip(cnt_j - (jrow[:, 0:1] - start_j) * tile_rows, 0.0, float(tile_rows)).astype(jnp.int32)
        tc = tcls.astype(jnp.int32)
        grp = sum((tc >= g * N_PAIRS).astype(jnp.int32) for g in range(1, N_GROUPS))
        pair = tc - grp * N_PAIRS
        lo = (pair >= 3).astype(jnp.int32) + (pair >= 5).astype(jnp.int32)
        hi = pair - jnp.right_shift(lo * (7 - lo), 1) + lo + 1
        ea = grp * EXPERTS_PER_GROUP + lo
        eb = grp * EXPERTS_PER_GROUP + hi
        tinfo_ref[...] = jnp.where(lane == 0, ea, jnp.where(lane == 1, eb, jnp.where(lane == 2, valid, 0)))

    cls = cls_ref[...]
    lane = lax.broadcasted_iota(jnp.int32, (cls.shape[0], LANES), 1)
    onehot = lane == cls
    earlier = _dot(tri_ref[...], jnp.where(onehot, 1.0, 0.0).astype(BF16))
    base = carry_scr[...] + off_scr[...]
    pos = jnp.sum(jnp.where(onehot, earlier + base, 0.0), axis=-1, keepdims=True)
    pos_ref[...] = pos.astype(jnp.int32)
    carry_scr[...] += jnp.sum(jnp.where(onehot, 1.0, 0.0), axis=0, keepdims=True)


def _route(cls, hist, tile_rows, n_tiles, tm):
    t = cls.shape[0]
    tri = jnp.tril(jnp.ones((tm, tm), BF16), -1)
    return pl.pallas_call(
        functools.partial(_route_kernel, tile_rows=tile_rows),
        out_shape=(jax.ShapeDtypeStruct((t, 1), jnp.int32),
                   jax.ShapeDtypeStruct((n_tiles, LANES), jnp.int32)),
        grid=(t // tm,),
        in_specs=[pl.BlockSpec((tm, 1), lambda i: (i, 0)),
                  pl.BlockSpec((1, LANES), lambda i: (0, 0)),
                  pl.BlockSpec((tm, tm), lambda i: (0, 0))],
        out_specs=(pl.BlockSpec((tm, 1), lambda i: (i, 0)),
                   pl.BlockSpec((n_tiles, LANES), lambda i: (0, 0))),
        scratch_shapes=[pltpu.VMEM((1, LANES), F32), pltpu.VMEM((1, LANES), F32)],
        compiler_params=_cparams("arbitrary"),
        name="route",
    )(cls, hist, tri)


def _start_and_wait_rows(row_copy, n_rows):
    def start(g, carry):
        for u in range(DMA_UNROLL):
            row_copy(g * DMA_UNROLL + u).start(priority=u % 2)
        return carry

    def wait(r, carry):
        row_copy(r).wait()
        return carry

    lax.fori_loop(0, n_rows // DMA_UNROLL, start, 0)
    lax.fori_loop(0, n_rows, wait, 0, unroll=DMA_UNROLL)


def _dispatch_kernel(pos_ref, hx_ref, sorted_ref, sem):
    tm = hx_ref.shape[0]

    def row_copy(r):
        return pltpu.make_async_copy(hx_ref.at[pl.ds(r, 1)], sorted_ref.at[pl.ds(pos_ref[0, r], 1)], sem)

    _start_and_wait_rows(row_copy, tm)


def _dispatch(pos, hx, n_slots, tm):
    t, w = hx.shape
    return pl.pallas_call(
        _dispatch_kernel,
        out_shape=jax.ShapeDtypeStruct((n_slots, w), F32),
        grid=(t // tm,),
        in_specs=[pl.BlockSpec((None, 1, tm), lambda i: (i, 0, 0), memory_space=pltpu.SMEM),
                  pl.BlockSpec((tm, w), lambda i: (i, 0))],
        out_specs=pl.BlockSpec(memory_space=pl.ANY),
        scratch_shapes=[pltpu.SemaphoreType.DMA],
        compiler_params=pltpu.CompilerParams(dimension_semantics=("arbitrary",), vmem_limit_bytes=VMEM_LIMIT,
                                             has_side_effects=True),
        name="dispatch",
    )(pos.reshape(t // tm, 1, tm), hx)


def _moe_routed_kernel(ea_ref, eb_ref, valid_ref, hx_ref, w1a_ref, w3a_ref, w2a_ref, w1b_ref, w3b_ref, w2b_ref, y_ref):
    j = pl.program_id(0)
    nvalid = valid_ref[j]

    @pl.when(nvalid == 0)
    def _():
        y_ref[...] = jnp.zeros_like(y_ref)

    @pl.when(nvalid > 0)
    def _():
        tm, d = y_ref.shape
        row = lax.broadcasted_iota(jnp.int32, (tm, 1), 0)
        real = row < nvalid
        h = jnp.where(real, hx_ref[:, 0:d], 0.0).astype(BF16)
        wts = jnp.where(real, hx_ref[:, d:d + LANES], 0.0)

        def expert(w1_ref, w3_ref, w2_ref):
            a = _dot(h, w1_ref[...])
            g = (a * jax.nn.sigmoid(a)) * _dot(h, w3_ref[...])
            return _dot(g.astype(BF16), w2_ref[...])

        y_ref[...] = (wts[:, 0:1] * expert(w1a_ref, w3a_ref, w2a_ref)
                      + wts[:, 1:2] * expert(w1b_ref, w3b_ref, w2b_ref))


def _moe_routed(hx_sorted, ea, eb, valid, w1, w3, w2, tile_rows):
    n_slots, w = hx_sorted.shape
    ne, d, de = w1.shape
    up = lambda sel: pl.BlockSpec((None, d, de), lambda j, ea, eb, valid: ((ea, eb)[sel][j], 0, 0))
    down = lambda sel: pl.BlockSpec((None, de, d), lambda j, ea, eb, valid: ((ea, eb)[sel][j], 0, 0))
    grid_spec = pltpu.PrefetchScalarGridSpec(
        num_scalar_prefetch=3,
        grid=(n_slots // tile_rows,),
        in_specs=[pl.BlockSpec((tile_rows, w), lambda j, ea, eb, valid: (j, 0)),
                  up(0), up(0), down(0), up(1), up(1), down(1)],
        out_specs=pl.BlockSpec((tile_rows, d), lambda j, ea, eb, valid: (j, 0)),
    )
    return pl.pallas_call(
        _moe_routed_kernel,
        out_shape=jax.ShapeDtypeStruct((n_slots, d), F32),
        grid_spec=grid_spec,
        compiler_params=_cparams("arbitrary"),
        name="moe_routed",
    )(ea, eb, valid, hx_sorted, w1, w3, w2, w1, w3, w2)


def _combine_kernel(pos_ref, ys_ref, x1_ref, gt2_ref, y_ref, buf, sem):
    tm = buf.shape[0]

    def row_copy(r):
        return pltpu.make_async_copy(ys_ref.at[pl.ds(pos_ref[0, r], 1)], buf.at[pl.ds(r, 1)], sem)

    _start_and_wait_rows(row_copy, tm)
    y_ref[...] = x1_ref[...] + gt2_ref[...] * buf[...]


def _combine(pos, ys, x1, gt2, tm, rows_per_group):
    t, d = x1.shape
    return pl.pallas_call(
        _combine_kernel,
        out_shape=jax.ShapeDtypeStruct((t, d), F32),
        grid=(t // tm,),
        in_specs=[pl.BlockSpec((None, 1, tm), lambda i: (i, 0, 0), memory_space=pltpu.SMEM),
                  pl.BlockSpec(memory_space=pl.ANY),
                  pl.BlockSpec((tm, d), lambda i: (i, 0)),
                  _mod_spec(gt2, tm, rows_per_group, 1)],
        out_specs=pl.BlockSpec((tm, d), lambda i: (i, 0)),
        scratch_shapes=[pltpu.VMEM((tm, d), F32), pltpu.SemaphoreType.DMA],
        compiler_params=_cparams("arbitrary"),
        name="combine",
    )(pos.reshape(t // tm, 1, tm), ys, x1, gt2)


def _moe_kernel(h_ref, cmb_ref, x1_ref, gt2_ref, w1_ref, w3_ref, w2_ref, y_ref, acc_scr):
    e = pl.program_id(1)

    @pl.when(e == 0)
    def _():
        acc_scr[...] = jnp.zeros_like(acc_scr)

    h = h_ref[...]
    a = _dot(h, w1_ref[...])
    g = (a * jax.nn.sigmoid(a)) * _dot(h, w3_ref[...])
    ff = _dot(g.astype(BF16), w2_ref[...])
    cmb = cmb_ref[...]
    lane = lax.broadcasted_iota(jnp.int32, cmb.shape, 1)
    w = jnp.sum(jnp.where(lane == e, cmb, 0.0), axis=-1, keepdims=True)
    acc_scr[...] += w * ff

    @pl.when(e == pl.num_programs(1) - 1)
    def _():
        y_ref[...] = x1_ref[...] + gt2_ref[...] * acc_scr[...]


def _moe(h2, cmb, x1, gt2, w1, w3, w2, tm, rows_per_group):
    t, d = x1.shape
    ne, _, de = w1.shape
    gt2_spec = (pl.BlockSpec((None, 1, d), lambda i, e: (i // (rows_per_group // tm), 0, 0))
                if gt2.ndim == 3 else pl.BlockSpec((tm, d), lambda i, e: (i, 0)))
    return pl.pallas_call(
        _moe_kernel,
        out_shape=jax.ShapeDtypeStruct((t, d), F32),
        grid=(t // tm, ne),
        in_specs=[pl.BlockSpec((tm, d), lambda i, e: (i, 0)),
                  pl.BlockSpec((tm, LANES), lambda i, e: (i, 0)),
                  pl.BlockSpec((tm, d), lambda i, e: (i, 0)),
                  gt2_spec,
                  pl.BlockSpec((None, d, de), lambda i, e: (e, 0, 0)),
                  pl.BlockSpec((None, d, de), lambda i, e: (e, 0, 0)),
                  pl.BlockSpec((None, de, d), lambda i, e: (e, 0, 0))],
        out_specs=pl.BlockSpec((tm, d), lambda i, e: (i, 0)),
        scratch_shapes=[pltpu.VMEM((tm, d), F32)],
        compiler_params=_cparams("arbitrary", "arbitrary"),
        name="moe",
    )(h2, cmb, x1, gt2, w1, w3, w2)


def _rope_tables(pos, gq, gk):
    posf = pos.astype(F32)[:, None]
    inv_r = ROPE_THETA ** (-jnp.arange(0, RET_DK, 2, dtype=F32) / RET_DK)
    ang_r = posf * inv_r[None, :]
    inv_d = ROPE_THETA ** (-jnp.arange(0, DIFF_DH, 2, dtype=F32) / DIFF_DH)
    lane = jnp.arange(LANES)
    half = DIFF_DH // 2
    ang_d = posf * inv_d[lane % half][None, :]
    first_half = ((lane % DIFF_DH) < half)[None, :]
    cos_d, sin_d = jnp.cos(ang_d), jnp.sin(ang_d)

    def diff_tables(g, scale):
        g128 = jnp.tile(g, LANES // DIFF_DH)
        return (cos_d * (g128 * scale)[None, :],
                jnp.where(first_half, -sin_d, 0.0) * (jnp.roll(g128, -half) * scale)[None, :],
                jnp.where(first_half, 0.0, sin_d) * (jnp.roll(g128, half) * scale)[None, :])

    return (jnp.cos(ang_r), jnp.sin(ang_r)) + diff_tables(gq, DIFF_DH ** -0.5 * LOG2E) + diff_tables(gk, 1.0)


def _tile_rows(n, target):
    tm = min(n, target)
    assert n % tm == 0
    return tm


def _layer(x, mod, pos, s0, kc, vc, lw, li, ret_chunk):
    nb, seq, d = x.shape
    t = nb * seq
    x2d = x.reshape(t, d)
    per_token = seq < 128
    if per_token:
        mods = [jnp.repeat(mod[:, k, :], seq, axis=0) for k in range(6)]
        tabs = tuple(jnp.tile(tb, (nb, 1)) for tb in _rope_tables(pos, lw['gq'], lw['gk']))
        rows_per_group = t
    else:
        mods = [mod[:, k:k + 1, :] for k in range(6)]
        tabs = _rope_tables(pos, lw['gq'], lw['gk'])
        rows_per_group = seq
    sh1, sc1, gt1, sh2, sc2, gt2 = mods
    tm = _tile_rows(rows_per_group, 512)

    p, qd, k2d, v2d = _inproj(x2d, sc1, sh1, lw['norm1_g'], lw['w_in'], tabs, lw['bd'],
                              _tile_rows(rows_per_group, 1024), rows_per_group)
    ret_o, s_new = _retention(p, s0, nb, seq, ret_chunk)
    lam_init = 0.8 - 0.6 * math.exp(-0.3 * li)
    if kc is None:
        dif_o = _diffattn(qd, k2d, v2d, lw['lam'], lw['subln_g'], nb, seq, lam_init, tq=min(seq, 512))
    else:
        dif_o = _diffattn_cached(qd, k2d, v2d, kc, vc, li, lw['lam'], lw['subln_g'], nb, seq, lam_init)
    merge_args = (ret_o, dif_o, p, x2d, gt1, sc2, sh2, lw['norm2_g'], lw['w_br_ret'], lw['w_br_diff'],
                  lw['w_out'], lw['rw_hi'], lw['rw_lo'], lw['rb'], tm, rows_per_group)
    if per_token:
        x1, h2, cmb = _merge(*merge_args, routed=False)
        y = _moe(h2, cmb, x1, gt2, lw['w1'], lw['w3'], lw['w2'], tm, rows_per_group)
    else:
        x1, hx, cls, hist = _merge(*merge_args, routed=True)
        n_tiles = SUBLANES * pl.cdiv(t // MOE_TILE_ROWS + N_CLASSES, SUBLANES)
        pos, tinfo = _route(cls, hist, MOE_TILE_ROWS, n_tiles, tm)
        hx_sorted = _dispatch(pos, hx, n_tiles * MOE_TILE_ROWS, tm)
        y_sorted = _moe_routed(hx_sorted, tinfo[:, 0], tinfo[:, 1], tinfo[:, 2],
                               lw['w1'], lw['w3'], lw['w2'], MOE_TILE_ROWS)
        y = _combine(pos, y_sorted, x1, gt2, tm, rows_per_group)
    return (y.reshape(nb, seq, d), s_new, k2d.reshape(nb, seq, DIFF_HEADS, 2 * DIFF_DH),
            v2d.reshape(nb, seq, DIFF_HEADS, DIFF_DV))


def kernel(x_prompt, x_sample, c_prompt, c_sample, cache_diff_k, cache_diff_v, state_ret, w_ada, b_ada, norm1_g, w_in, diff_qnorm_g, diff_knorm_g, diff_lambda_q1, diff_lambda_k1, diff_lambda_q2, diff_lambda_k2, diff_subln_g, w_br_ret, w_br_diff, w_out, norm2_g, w_group, b_group, w_expert_router, b_expert, w1, w3, w2):
    nb, seq, d = x_prompt.shape
    nbs, seqs, _ = x_sample.shape
    depth = w_in.shape[0]
    past = cache_diff_k.shape[2]
    pos_p = jnp.arange(seq, dtype=jnp.int32)
    pos_s = past + jnp.arange(seqs, dtype=jnp.int32)
    c_all = jnp.concatenate([c_prompt, c_sample], axis=0)
    col = jnp.arange(MXU_COLS)
    bd = jnp.where((col[:, None] // DIFF_DH) == (col[None, :] // DIFF_DH), 1.0 / DIFF_DH, 0.0).astype(BF16)

    xp, xs = x_prompt, x_sample
    outs = [[] for _ in range(6)]
    for li in range(depth):
        rw = jnp.zeros((d, LANES), F32)
        rw = rw.at[:, :N_GROUPS].set(w_group[li]).at[:, N_GROUPS:N_GROUPS + N_EXPERTS].set(w_expert_router[li])
        rw_hi = rw.astype(BF16)
        rb = jnp.zeros((1, LANES), F32)
        rb = rb.at[0, :N_GROUPS].set(b_group[li]).at[0, N_GROUPS:N_GROUPS + N_EXPERTS].set(b_expert[li])
        lw = {
            'norm1_g': norm1_g[li].reshape(1, d), 'norm2_g': norm2_g[li].reshape(1, d),
            'w_in': jnp.stack([w_in[li, :, blk * COL_BLOCK:(blk + 1) * COL_BLOCK] for blk in IN_BLOCK_ORDER]
                              ).astype(BF16),
            'gq': diff_qnorm_g[li], 'gk': diff_knorm_g[li],
            'bd': bd,
            'lam': jnp.stack([diff_lambda_q1[li], diff_lambda_k1[li], diff_lambda_q2[li], diff_lambda_k2[li]]),
            'subln_g': diff_subln_g[li].reshape(1, DIFF_DV),
            'w_br_ret': w_br_ret[li].astype(BF16), 'w_br_diff': w_br_diff[li].astype(BF16),
            'w_out': w_out[li].astype(BF16),
            'rw_hi': rw_hi, 'rw_lo': (rw - rw_hi.astype(F32)).astype(BF16), 'rb': rb,
            'w1': w1[li].astype(BF16), 'w3': w3[li].astype(BF16), 'w2': w2[li].astype(BF16),
        }
        mod = _adaln(c_all, w_ada[li], b_ada[li]).reshape(nb + nbs, 6, d)
        xp, sp, kp, vp = _layer(xp, mod[:nb], pos_p, None, None, None, lw, li, ret_chunk=min(seq, 256))
        xs, ss, kss, vss = _layer(xs, mod[nb:], pos_s, state_ret[li], cache_diff_k, cache_diff_v,
                                  lw, li, ret_chunk=seqs)
        for lst, val in zip(outs, (kp, vp, sp, kss, vss, ss)):
            lst.append(val)
    kp, vp, sp, kss, vss, ss = (jnp.stack(lst) for lst in outs)
    return (xp, xs, kp, vp, sp.astype(x_prompt.dtype), kss, vss, ss.astype(state_ret.dtype))
```

```python
import functools
import math

import jax
import jax.numpy as jnp
from jax import lax
from jax.experimental import pallas as pl
from jax.experimental.pallas import tpu as pltpu

F32 = jnp.float32
BF16 = jnp.bfloat16

D_MODEL = 1024
CHUNK = 64
ROPE_THETA = 10000.0
EPS = 1e-6
RET_HEADS = 4
RET_DK = D_MODEL // RET_HEADS
RET_DV = 2 * RET_DK
DIFF_HEADS = 8
DIFF_DH = D_MODEL // (2 * DIFF_HEADS)
DIFF_DV = 2 * DIFF_DH
N_GROUPS = 4
EXPERTS_PER_GROUP = 4
N_EXPERTS = N_GROUPS * EXPERTS_PER_GROUP
D_EXPERT = 512
N_PAIRS = EXPERTS_PER_GROUP * (EXPERTS_PER_GROUP - 1) // 2
N_CLASSES = N_GROUPS * N_PAIRS
MOE_TILE_ROWS = 512
ATTN_ROW_PART = 256
MERGE_ROW_PARTS = 1
DMA_UNROLL = 8

LANES = 128
SUBLANES = 8
MXU_COLS = 256
COL_BLOCK = 1024
IN_BLOCK_ORDER = (0, 1, 2, 3, 4, 5, 6, 9, 10, 7, 8)
N_SLAB_BLOCKS = 8
SLAB_STRIDE_BLOCKS = 9
SLAB_RQ, SLAB_RK, SLAB_RV, SLAB_RG, SLAB_GATE = 0, 1, 2, 4, 6
VMEM_LIMIT = 56 * 1024 * 1024
LOG2E = math.log2(math.e)


def _cparams(*sem):
    return pltpu.CompilerParams(dimension_semantics=sem, vmem_limit_bytes=VMEM_LIMIT)


def _dot(a, b):
    return jnp.dot(a, b, preferred_element_type=F32)


def _dot_nt(a, b):
    return lax.dot_general(a, b, (((1,), (1,)), ((), ())), preferred_element_type=F32)


def _adaln_kernel(c_ref, w_ref, b_ref, o_ref):
    c = c_ref[...]
    a = c * jax.nn.sigmoid(c)
    o_ref[...] = jnp.dot(a, w_ref[...], preferred_element_type=F32,
                         precision=lax.Precision.HIGHEST) + b_ref[...]


def _adaln(c, w_ada, b_ada):
    rows, d = c.shape
    n = w_ada.shape[1]
    tn = 1536
    return pl.pallas_call(
        _adaln_kernel,
        out_shape=jax.ShapeDtypeStruct((rows, n), F32),
        grid=(n // tn,),
        in_specs=[pl.BlockSpec((rows, d), lambda j: (0, 0)),
                  pl.BlockSpec((d, tn), lambda j: (0, j)),
                  pl.BlockSpec((1, tn), lambda j: (0, j))],
        out_specs=pl.BlockSpec((rows, tn), lambda j: (0, j)),
        compiler_params=_cparams("arbitrary"),
        name="adaln",
    )(c, w_ada, b_ada.reshape(1, n))


def _rms(x):
    return x * lax.rsqrt(jnp.mean(x * x, axis=-1, keepdims=True) + EPS)


def _inproj_kernel(x_ref, sc_ref, sh_ref, g1_ref, w_ref, rc_ref, rs_ref, qc_ref, qsn_ref, qsp_ref,
                   kc_ref, ksn_ref, ksp_ref, bd_ref, p_ref, qd_ref, k_ref, v_ref, h_scr):
    j = pl.program_id(1)

    @pl.when(j == 0)
    def _():
        h = _rms(x_ref[...]) * g1_ref[...]
        h = h * (1.0 + sc_ref[...]) + sh_ref[...]
        h_scr[...] = h.astype(BF16)

    def strips():
        h = h_scr[...]
        starts = list(range(0, COL_BLOCK, MXU_COLS))
        ahead = _dot(h, w_ref[:, 0:MXU_COLS])
        for n, lo in enumerate(starts):
            acc = ahead
            if n + 1 < len(starts):
                ahead = _dot(h, w_ref[:, starts[n + 1]:starts[n + 1] + MXU_COLS])
            yield lo, acc

    def ret_rope(scale):
        c = rc_ref[...]
        s = rs_ref[...]
        for lo, acc in strips():
            x1 = acc[:, 0:LANES]
            x2 = acc[:, LANES:2 * LANES]
            p_ref[:, lo:lo + LANES] = ((x1 * c - x2 * s) * scale).astype(BF16)
            p_ref[:, lo + LANES:lo + 2 * LANES] = ((x1 * s + x2 * c) * scale).astype(BF16)

    def diff_norm():
        for lo, acc in list(strips()):
            k_ref[:, lo:lo + MXU_COLS] = acc * lax.rsqrt(_dot((acc * acc).astype(BF16), bd_ref[...]) + EPS)

    def diff_rope(lo, c_ref, sn_ref, sp_ref, out_ref):
        for off in range(lo, lo + MXU_COLS, LANES):
            yg = k_ref[:, off:off + LANES]
            o = (yg * c_ref[...] + pltpu.roll(yg, LANES - DIFF_DH // 2, 1) * sn_ref[...]
                 + pltpu.roll(yg, DIFF_DH // 2, 1) * sp_ref[...])
            out_ref[:, off:off + LANES] = o.astype(out_ref.dtype)

    def elementwise(fn, out_ref, after_strip=None):
        for lo, acc in strips():
            out_ref[:, lo:lo + MXU_COLS] = fn(acc).astype(out_ref.dtype)
            if after_strip is not None:
                after_strip(lo)

    @pl.when(j == 0)
    def _():
        ret_rope(1.0)

    @pl.when(j == 1)
    def _():
        ret_rope(RET_DK ** -0.5)

    @pl.when((j == 2) | (j == 3))
    def _():
        elementwise(lambda a: a, p_ref)

    @pl.when((j == 4) | (j == 5))
    def _():
        elementwise(lambda a: a * jax.nn.sigmoid(a), p_ref)

    @pl.when((j == 6) | (j == 9))
    def _():
        diff_norm()

    @pl.when(j == 7)
    def _():
        elementwise(jax.nn.sigmoid, p_ref, lambda lo: diff_rope(lo, qc_ref, qsn_ref, qsp_ref, qd_ref))

    @pl.when(j == 8)
    def _():
        elementwise(jax.nn.sigmoid, p_ref)

    @pl.when(j == 10)
    def _():
        elementwise(lambda a: a, v_ref, lambda lo: diff_rope(lo, kc_ref, ksn_ref, ksp_ref, k_ref))


def _mod_spec(arr, tm, rows_per_group, ngrid):
    d = arr.shape[-1]
    if arr.ndim == 3:
        tiles_per_group = rows_per_group // tm
        if ngrid == 2:
            return pl.BlockSpec((None, 1, d), lambda i, j: (i // tiles_per_group, 0, 0))
        return pl.BlockSpec((None, 1, d), lambda i: (i // tiles_per_group, 0, 0))
    if ngrid == 2:
        return pl.BlockSpec((tm, d), lambda i, j: (i, 0))
    return pl.BlockSpec((tm, d), lambda i: (i, 0))


def _inproj(x2d, sc1, sh1, g1, w_blocks, tabs, bd, tm, rows_per_group):
    t, d = x2d.shape
    nrep = tabs[0].shape[0] // tm

    def slab_map(i, j):
        return (i, jnp.where(j < 6, j, jnp.where(j < 9, jnp.maximum(j - 1, 5), N_SLAB_BLOCKS - 1)))

    tab_spec = pl.BlockSpec((tm, LANES), lambda i, j: (i % nrep, 0))
    row_spec = pl.BlockSpec((tm, d), lambda i, j: (i, 0))
    return pl.pallas_call(
        _inproj_kernel,
        out_shape=(jax.ShapeDtypeStruct((t, SLAB_STRIDE_BLOCKS * COL_BLOCK), BF16),
                   jax.ShapeDtypeStruct((t, d), BF16),
                   jax.ShapeDtypeStruct((t, d), F32),
                   jax.ShapeDtypeStruct((t, d), F32)),
        grid=(t // tm, len(IN_BLOCK_ORDER)),
        in_specs=[row_spec,
                  _mod_spec(sc1, tm, rows_per_group, 2),
                  _mod_spec(sh1, tm, rows_per_group, 2),
                  pl.BlockSpec((1, d), lambda i, j: (0, 0)),
                  pl.BlockSpec((None, d, COL_BLOCK), lambda i, j: (j, 0, 0)),
                  *([tab_spec] * len(tabs)),
                  pl.BlockSpec((MXU_COLS, MXU_COLS), lambda i, j: (0, 0))],
        out_specs=(pl.BlockSpec((tm, COL_BLOCK), slab_map), row_spec, row_spec, row_spec),
        scratch_shapes=[pltpu.VMEM((tm, d), BF16)],
        compiler_params=_cparams("arbitrary", "arbitrary"),
        name="inproj",
    )(x2d, sc1, sh1, g1, w_blocks, *tabs, bd)


def _retention_kernel(*refs, has_state, chunk):
    if has_state:
        q_ref, k_ref, v_ref, rg_ref, dm_ref, dv_ref, s0_ref, o_ref, sout_ref = refs
        s = s0_ref[...]
    else:
        q_ref, k_ref, v_ref, rg_ref, dm_ref, dv_ref, o_ref, sout_ref = refs
        s = jnp.zeros(sout_ref.shape, F32)
    dvec = dv_ref[...]
    q_decay = dvec[:, 0:1]
    k_decay = dvec[:, 1:2]
    chunk_decay = dvec[0:1, 2:3]

    for c in range(q_ref.shape[0] // chunk):
        rows = slice(c * chunk, (c + 1) * chunk)
        q = q_ref[rows, :]
        k = k_ref[rows, :]
        v = v_ref[rows, :]
        scores = (_dot_nt(q, k) * dm_ref[...]).astype(BF16)
        o = _dot(scores, v) + _dot(q, s.astype(BF16)) * q_decay
        o = _rms(o) * rg_ref[rows, :].astype(F32)
        o_ref[rows, :] = o.astype(BF16)
        kd_t = (k.astype(F32) * k_decay).T.astype(BF16)
        s = s * chunk_decay + _dot(kd_t, v)
    sout_ref[...] = s


def _retention(p, s0, nb, seq, chunk):
    t = p.shape[0]
    idx = jnp.arange(chunk, dtype=F32)
    log_gamma = jnp.log1p(-jnp.exp2(-5.0 - jnp.arange(RET_HEADS, dtype=F32)))
    rel = idx[:, None] - idx[None, :]
    dmat = jnp.where(rel >= 0, jnp.exp(jnp.maximum(rel, 0.0)[None] * log_gamma[:, None, None]), 0.0)
    q_decay = jnp.exp((idx + 1.0)[None, :] * log_gamma[:, None])
    k_decay = jnp.exp((chunk - 1.0 - idx)[None, :] * log_gamma[:, None])
    c_decay = jnp.broadcast_to(jnp.exp(chunk * log_gamma)[:, None], (RET_HEADS, chunk))
    dvec = jnp.zeros((RET_HEADS, chunk, LANES), F32)
    dvec = dvec.at[:, :, 0].set(q_decay).at[:, :, 1].set(k_decay).at[:, :, 2].set(c_decay)

    kcols, vcols = COL_BLOCK // RET_DK, COL_BLOCK // RET_DV
    in_specs = [pl.BlockSpec((seq, RET_DK), lambda b, h: (b, SLAB_RQ * kcols + h)),
                pl.BlockSpec((seq, RET_DK), lambda b, h: (b, SLAB_RK * kcols + h)),
                pl.BlockSpec((seq, RET_DV), lambda b, h: (b, SLAB_RV * vcols + h)),
                pl.BlockSpec((seq, RET_DV), lambda b, h: (b, SLAB_RG * vcols + h)),
                pl.BlockSpec((None, chunk, chunk), lambda b, h: (h, 0, 0)),
                pl.BlockSpec((None, chunk, LANES), lambda b, h: (h, 0, 0))]
    args = [p, p, p, p, dmat, dvec]
    state_spec = pl.BlockSpec((None, None, RET_DK, RET_DV), lambda b, h: (b, h, 0, 0))
    if s0 is not None:
        in_specs.append(state_spec)
        args.append(s0)
    return pl.pallas_call(
        functools.partial(_retention_kernel, has_state=s0 is not None, chunk=chunk),
        out_shape=(jax.ShapeDtypeStruct((t, RET_HEADS * RET_DV), BF16),
                   jax.ShapeDtypeStruct((nb, RET_HEADS, RET_DK, RET_DV), F32)),
        grid=(nb, RET_HEADS),
        in_specs=in_specs,
        out_specs=(pl.BlockSpec((seq, RET_DV), lambda b, h: (b, h)), state_spec),
        compiler_params=_cparams("arbitrary", "arbitrary"),
        name="retention",
    )(*args)


def _lambda(lam_ref, lam_init):
    l1 = jnp.sum(lam_ref[0:1, :] * lam_ref[1:2, :], axis=-1, keepdims=True)
    l2 = jnp.sum(lam_ref[2:3, :] * lam_ref[3:4, :], axis=-1, keepdims=True)
    return jnp.exp(l1) - jnp.exp(l2) + lam_init


def _split_maps(q):
    lane = lax.broadcasted_iota(jnp.int32, q.shape, 1)
    zero = jnp.zeros_like(q)
    return jnp.where(lane < DIFF_DH, q, zero), jnp.where(lane >= DIFF_DH, q, zero)


def _diffattn_kernel(q_ref, k_ref, v_ref, lam_ref, g_ref, o_ref, kb_scr, vb_scr, *, part, lam_init):
    seq = k_ref.shape[0]
    kb_scr[...] = k_ref[...].astype(BF16)
    vb_scr[:, 0:DIFF_DV] = v_ref[...].astype(BF16)
    vb_scr[:, DIFF_DV:2 * DIFF_DV] = jnp.ones((seq, DIFF_DV), BF16)
    lam = _lambda(lam_ref, lam_init)
    row = lax.broadcasted_iota(jnp.int32, (part, part), 0)
    col = lax.broadcasted_iota(jnp.int32, (part, part), 1)
    visible = (col // CHUNK) <= (row // CHUNK)

    def scores(q0):
        qmaps = _split_maps(q_ref[q0:q0 + part, :])
        kblk = kb_scr[0:q0 + part, :]
        out = []
        for mp in range(2):
            sc = _dot_nt(qmaps[mp], kblk)
            diag = jnp.where(visible, sc[:, q0:q0 + part], -1e30)
            out.append(jnp.concatenate([sc[:, 0:q0], diag], axis=1) if q0 else diag)
        return out

    starts = list(range(0, seq, part))
    pending = scores(starts[0])
    for n, q0 in enumerate(starts):
        scs = pending
        if n + 1 < len(starts):
            pending = scores(starts[n + 1])
        vblk = vb_scr[0:q0 + part, :]
        att = []
        for mp in range(2):
            sc = scs[mp]
            pv = _dot(jnp.exp2(sc - jnp.max(sc, axis=-1, keepdims=True)).astype(BF16), vblk)
            att.append(pv[:, 0:DIFF_DV] / pv[:, DIFF_DV:2 * DIFF_DV])
        o = att[0] - lam * att[1]
        o = _rms(o) * g_ref[...] * (1.0 - lam_init)
        o_ref[q0:q0 + part, :] = o.astype(BF16)


def _diffattn(qd, k2d, v2d, lam_vecs, subln_g, nb, seq, lam_init):
    t = qd.shape[0]
    part = min(seq, ATTN_ROW_PART)
    assert seq % part == 0 and part % CHUNK == 0
    return pl.pallas_call(
        functools.partial(_diffattn_kernel, part=part, lam_init=lam_init),
        out_shape=jax.ShapeDtypeStruct((t, DIFF_HEADS * DIFF_DV), BF16),
        grid=(nb, DIFF_HEADS),
        in_specs=[pl.BlockSpec((seq, DIFF_DV), lambda b, h: (b, h)),
                  pl.BlockSpec((seq, DIFF_DV), lambda b, h: (b, h)),
                  pl.BlockSpec((seq, DIFF_DV), lambda b, h: (b, h)),
                  pl.BlockSpec((4, DIFF_DH), lambda b, h: (0, 0)),
                  pl.BlockSpec((1, DIFF_DV), lambda b, h: (0, 0))],
        out_specs=pl.BlockSpec((seq, DIFF_DV), lambda b, h: (b, h)),
        scratch_shapes=[pltpu.VMEM((seq, DIFF_DV), BF16), pltpu.VMEM((seq, 2 * DIFF_DV), BF16)],
        compiler_params=_cparams("arbitrary", "arbitrary"),
        name="diffattn",
    )(qd, k2d, v2d, lam_vecs, subln_g)


def _diffattn_cached_kernel(q_ref, kc_ref, vc_ref, kn_ref, vn_ref, lam_ref, g_ref, o_ref, k_scr, v_scr,
                            *, past, new, lam_init):
    total = k_scr.shape[0]
    pad = jnp.zeros((total - past, DIFF_DV), BF16)
    qpos = past + lax.broadcasted_iota(jnp.int32, (new, total), 0)
    kpos = lax.broadcasted_iota(jnp.int32, (new, total), 1)
    visible = ((kpos // CHUNK) <= (qpos // CHUNK)) & (kpos < past + new)
    lam = _lambda(lam_ref, lam_init)

    for h in range(DIFF_HEADS):
        cols = slice(h * DIFF_DV, (h + 1) * DIFF_DV)
        k_scr[0:past, :] = kc_ref[:, h, :].astype(BF16)
        v_scr[0:past, :] = vc_ref[:, h, :].astype(BF16)
        k_scr[past:total, :] = pad
        v_scr[past:total, :] = pad
        k_scr[past:past + new, :] = kn_ref[:, cols].astype(BF16)
        v_scr[past:past + new, :] = vn_ref[:, cols].astype(BF16)
        kall = k_scr[...]
        vall = v_scr[...]

        def softmax_av(qm):
            sc = jnp.where(visible, _dot_nt(qm, kall), -1e30)
            m = jnp.max(sc, axis=-1, keepdims=True)
            p = jnp.exp2(sc - m)
            return _dot(p.astype(BF16), vall) / jnp.sum(p, axis=-1, keepdims=True)

        qa, qb = _split_maps(q_ref[:, cols])
        o = softmax_av(qa) - lam * softmax_av(qb)
        o = _rms(o) * g_ref[...] * (1.0 - lam_init)
        o_ref[:, cols] = o.astype(BF16)


def _diffattn_cached(qd, k2d, v2d, kc, vc, li, lam_vecs, subln_g, nb, new, lam_init):
    t = qd.shape[0]
    past = kc.shape[2]
    width = DIFF_HEADS * DIFF_DV
    total = past + LANES * pl.cdiv(new, LANES)
    cache_spec = pl.BlockSpec((None, None, past, DIFF_HEADS, DIFF_DV), lambda b: (li, b, 0, 0, 0))
    return pl.pallas_call(
        functools.partial(_diffattn_cached_kernel, past=past, new=new, lam_init=lam_init),
        out_shape=jax.ShapeDtypeStruct((t, width), BF16),
        grid=(nb,),
        in_specs=[pl.BlockSpec((new, width), lambda b: (b, 0)),
                  cache_spec, cache_spec,
                  pl.BlockSpec((new, width), lambda b: (b, 0)),
                  pl.BlockSpec((new, width), lambda b: (b, 0)),
                  pl.BlockSpec((4, DIFF_DH), lambda b: (0, 0)),
                  pl.BlockSpec((1, DIFF_DV), lambda b: (0, 0))],
        out_specs=pl.BlockSpec((new, width), lambda b: (b, 0)),
        scratch_shapes=[pltpu.VMEM((total, DIFF_DV), BF16), pltpu.VMEM((total, DIFF_DV), BF16)],
        compiler_params=_cparams("arbitrary"),
        name="diffattn_cached",
    )(qd, kc, vc, k2d, v2d, lam_vecs, subln_g)


def _merge_kernel(ret_ref, dif_ref, gr_ref, gd_ref, x_ref, gt1_ref, sc2_ref, sh2_ref, g2_ref,
                  wr_ref, wd_ref, wo_ref, rwh_ref, rwl_ref, rb_ref, x1_ref, *out_refs, routed):
    tm, d = x_ref.shape
    n_parts = MERGE_ROW_PARTS if tm % (MERGE_ROW_PARTS * LANES) == 0 else 1
    part = tm // n_parts

    def mod_rows(ref, rows):
        return ref[rows, :] if ref.shape[0] > 1 else ref[...]

    hist = jnp.zeros((1, LANES), F32)
    for r in range(n_parts):
        rows = slice(r * part, (r + 1) * part)
        a = _dot(ret_ref[rows, :], wr_ref[...])
        b = _dot(dif_ref[rows, :], wd_ref[...])
        merged = gr_ref[rows, :].astype(F32) * a + gd_ref[rows, :].astype(F32) * b
        x1 = x_ref[rows, :] + mod_rows(gt1_ref, rows) * _dot(merged.astype(BF16), wo_ref[...])
        x1_ref[rows, :] = x1
        h2 = _rms(x1) * g2_ref[...]
        h2 = h2 * (1.0 + mod_rows(sc2_ref, rows)) + mod_rows(sh2_ref, rows)

        hi = h2.astype(BF16)
        lo = (h2 - hi.astype(F32)).astype(BF16)
        logits = _dot(hi, rwh_ref[...]) + _dot(lo, rwh_ref[...]) + _dot(hi, rwl_ref[...]) + rb_ref[...]

        lane = lax.broadcasted_iota(jnp.int32, logits.shape, 1)
        big = jnp.int32(4 * LANES)

        def masked_argmax(mask):
            vmax = jnp.max(jnp.where(mask, logits, -jnp.inf), axis=-1, keepdims=True)
            idx = jnp.min(jnp.where(mask & (logits == vmax), lane, big), axis=-1, keepdims=True)
            return vmax, idx

        is_group = lane < N_GROUPS
        gmax, gidx = masked_argmax(is_group)
        gprob = 1.0 / jnp.sum(jnp.where(is_group, jnp.exp(logits - gmax), 0.0), axis=-1, keepdims=True)
        first = N_GROUPS + gidx * EXPERTS_PER_GROUP
        in_group = (lane >= first) & (lane < first + EXPERTS_PER_GROUP)
        v1, i1 = masked_argmax(in_group)
        v2, i2 = masked_argmax(in_group & (lane != i1))
        e2 = jnp.exp(v2 - v1)
        w1 = gprob / (1.0 + e2)
        w2 = gprob * e2 / (1.0 + e2)
        if not routed:
            h2_ref, cmb_ref = out_refs
            h2_ref[rows, :] = h2.astype(BF16)
            cmb_ref[rows, :] = (jnp.where(lane + N_GROUPS == i1, w1, 0.0)
                                + jnp.where(lane + N_GROUPS == i2, w2, 0.0))
            continue

        hx_ref, cls_ref, hist_ref = out_refs
        first_is_lo = i1 < i2
        lo_e = jnp.minimum(i1, i2) - first
        hi_e = jnp.maximum(i1, i2) - first
        cls = gidx * N_PAIRS + jnp.right_shift(lo_e * (7 - lo_e), 1) + hi_e - lo_e - 1
        hx_ref[rows, 0:d] = h2
        hx_ref[rows, d:d + LANES] = jnp.where(lane == 0, jnp.where(first_is_lo, w1, w2),
                                              jnp.where(lane == 1, jnp.where(first_is_lo, w2, w1), 0.0))
        cls_ref[rows, :] = cls
        hist = hist + jnp.sum(jnp.where(lane == cls, 1.0, 0.0), axis=0, keepdims=True)

    if routed:
        hist_ref = out_refs[2]

        @pl.when(pl.program_id(0) == 0)
        def _():
            hist_ref[...] = jnp.zeros_like(hist_ref)

        hist_ref[...] += hist


def _merge(ret_o, dif_o, p, x2d, gt1, sc2, sh2, g2, wr, wd, wo, rwh, rwl, rb, tm, rows_per_group, routed):
    t, d = x2d.shape
    const = lambda shape: pl.BlockSpec(shape, lambda i: (0, 0), pipeline_mode=pl.Buffered(1))
    if routed:
        out_shape = (jax.ShapeDtypeStruct((t, d), F32),
                     jax.ShapeDtypeStruct((t, d + LANES), F32),
                     jax.ShapeDtypeStruct((t, 1), jnp.int32),
                     jax.ShapeDtypeStruct((1, LANES), F32))
        out_specs = (pl.BlockSpec((tm, d), lambda i: (i, 0)),
                     pl.BlockSpec((tm, d + LANES), lambda i: (i, 0)),
                     pl.BlockSpec((tm, 1), lambda i: (i, 0)),
                     pl.BlockSpec((1, LANES), lambda i: (0, 0)))
    else:
        out_shape = (jax.ShapeDtypeStruct((t, d), F32),
                     jax.ShapeDtypeStruct((t, d), BF16),
                     jax.ShapeDtypeStruct((t, LANES), F32))
        out_specs = (pl.BlockSpec((tm, d), lambda i: (i, 0)),
                     pl.BlockSpec((tm, d), lambda i: (i, 0)),
                     pl.BlockSpec((tm, LANES), lambda i: (i, 0)))
    return pl.pallas_call(
        functools.partial(_merge_kernel, routed=routed),
        out_shape=out_shape,
        grid=(t // tm,),
        in_specs=[pl.BlockSpec((tm, ret_o.shape[1]), lambda i: (i, 0)),
                  pl.BlockSpec((tm, d), lambda i: (i, 0)),
                  pl.BlockSpec((tm, d), lambda i: (i, SLAB_GATE)),
                  pl.BlockSpec((tm, d), lambda i: (i, SLAB_GATE + 1)),
                  pl.BlockSpec((tm, d), lambda i: (i, 0)),
                  _mod_spec(gt1, tm, rows_per_group, 1),
                  _mod_spec(sc2, tm, rows_per_group, 1),
                  _mod_spec(sh2, tm, rows_per_group, 1),
                  const((1, d)),
                  const(wr.shape), const(wd.shape), const(wo.shape),
                  const(rwh.shape), const(rwl.shape), const(rb.shape)],
        out_specs=out_specs,
        compiler_params=_cparams("arbitrary"),
        name="merge",
    )(ret_o, dif_o, p, p, x2d, gt1, sc2, sh2, g2, wr, wd, wo, rwh, rwl, rb)


def _route_kernel(cls_ref, hist_ref, tri_ref, pos_ref, tinfo_ref, carry_scr, off_scr, *, tile_rows):
    i = pl.program_id(0)

    def excl_cumsum(v):
        lane8 = lax.broadcasted_iota(jnp.int32, v.shape, 1)
        inc = v
        for sh in (1, 2, 4, 8, 16, 32, 64):
            inc = inc + jnp.where(lane8 >= sh, pltpu.roll(inc, sh, 1), 0.0)
        return inc - v

    @pl.when(i == 0)
    def _():
        cnt = hist_ref[...]
        ntile = jnp.floor((cnt + (tile_rows - 1)) * (1.0 / tile_rows))
        start = excl_cumsum(jnp.broadcast_to(ntile, (SUBLANES, LANES)))[0:1, :]
        off_scr[...] = start * tile_rows
        carry_scr[...] = jnp.zeros_like(carry_scr)
        nt = tinfo_ref.shape[0]
        jrow = lax.broadcasted_iota(jnp.int32, (nt, LANES), 0).astype(F32)
        lane = lax.broadcasted_iota(jnp.int32, (nt, LANES), 1)
        is_cls = lane < N_CLASSES
        tcls = jnp.sum(jnp.where(is_cls & (start <= jrow), 1.0, 0.0), axis=-1, keepdims=True) - 1.0
        pick = lane == tcls.astype(jnp.int32)
        cnt_j = jnp.sum(jnp.where(pick, cnt, 0.0), axis=-1, keepdims=True)
        start_j = jnp.sum(jnp.where(pick, start, 0.0), axis=-1, keepdims=True)
        valid = jnp.clip(cnt_j - (jrow[:, 0:1] - start_j) * tile_rows, 0.0, float(tile_rows)).astype(jnp.int32)
        tc = tcls.astype(jnp.int32)
        grp = sum((tc >= g * N_PAIRS).astype(jnp.int32) for g in range(1, N_GROUPS))
        pair = tc - grp * N_PAIRS
        lo = (pair >= 3).astype(jnp.int32) + (pair >= 5).astype(jnp.int32)
        hi = pair - jnp.right_shift(lo * (7 - lo), 1) + lo + 1
        ea = grp * EXPERTS_PER_GROUP + lo
        eb = grp * EXPERTS_PER_GROUP + hi
        tinfo_ref[...] = jnp.where(lane == 0, ea, jnp.where(lane == 1, eb, jnp.where(lane == 2, valid, 0)))

    cls = cls_ref[...]
    lane = lax.broadcasted_iota(jnp.int32, (cls.shape[0], LANES), 1)
    onehot = lane == cls
    earlier = _dot(tri_ref[...], jnp.where(onehot, 1.0, 0.0).astype(BF16))
    base = carry_scr[...] + off_scr[...]
    pos = jnp.sum(jnp.where(onehot, earlier + base, 0.0), axis=-1, keepdims=True)
    pos_ref[...] = pos.astype(jnp.int32)
    carry_scr[...] += jnp.sum(jnp.where(onehot, 1.0, 0.0), axis=0, keepdims=True)


def _route(cls, hist, tile_rows, n_tiles, tm):
    t = cls.shape[0]
    tri = jnp.tril(jnp.ones((tm, tm), BF16), -1)
    return pl.pallas_call(
        functools.partial(_route_kernel, tile_rows=tile_rows),
        out_shape=(jax.ShapeDtypeStruct((t, 1), jnp.int32),
                   jax.ShapeDtypeStruct((n_tiles, LANES), jnp.int32)),
        grid=(t // tm,),
        in_specs=[pl.BlockSpec((tm, 1), lambda i: (i, 0)),
                  pl.BlockSpec((1, LANES), lambda i: (0, 0)),
                  pl.BlockSpec((tm, tm), lambda i: (0, 0))],
        out_specs=(pl.BlockSpec((tm, 1), lambda i: (i, 0)),
                   pl.BlockSpec((n_tiles, LANES), lambda i: (0, 0))),
        scratch_shapes=[pltpu.VMEM((1, LANES), F32), pltpu.VMEM((1, LANES), F32)],
        compiler_params=_cparams("arbitrary"),
        name="route",
    )(cls, hist, tri)


def _start_and_wait_rows(row_copy, n_rows):
    def start(g, carry):
        for u in range(DMA_UNROLL):
            row_copy(g * DMA_UNROLL + u).start(priority=u % 2)
        return carry

    def wait(r, carry):
        row_copy(r).wait()
        return carry

    lax.fori_loop(0, n_rows // DMA_UNROLL, start, 0)
    lax.fori_loop(0, n_rows, wait, 0, unroll=DMA_UNROLL)


def _dispatch_kernel(pos_ref, hx_ref, sorted_ref, sem):
    tm = hx_ref.shape[0]

    def row_copy(r):
        return pltpu.make_async_copy(hx_ref.at[pl.ds(r, 1)], sorted_ref.at[pl.ds(pos_ref[0, r], 1)], sem)

    _start_and_wait_rows(row_copy, tm)


def _dispatch(pos, hx, n_slots, tm):
    t, w = hx.shape
    return pl.pallas_call(
        _dispatch_kernel,
        out_shape=jax.ShapeDtypeStruct((n_slots, w), F32),
        grid=(t // tm,),
        in_specs=[pl.BlockSpec((None, 1, tm), lambda i: (i, 0, 0), memory_space=pltpu.SMEM),
                  pl.BlockSpec((tm, w), lambda i: (i, 0))],
        out_specs=pl.BlockSpec(memory_space=pl.ANY),
        scratch_shapes=[pltpu.SemaphoreType.DMA],
        compiler_params=pltpu.CompilerParams(dimension_semantics=("arbitrary",), vmem_limit_bytes=VMEM_LIMIT,
                                             has_side_effects=True),
        name="dispatch",
    )(pos.reshape(t // tm, 1, tm), hx)


def _moe_routed_kernel(ea_ref, eb_ref, valid_ref, hx_ref, w1a_ref, w3a_ref, w2a_ref, w1b_ref, w3b_ref, w2b_ref, y_ref):
    j = pl.program_id(0)
    nvalid = valid_ref[j]

    @pl.when(nvalid == 0)
    def _():
        y_ref[...] = jnp.zeros_like(y_ref)

    @pl.when(nvalid > 0)
    def _():
        tm, d = y_ref.shape
        row = lax.broadcasted_iota(jnp.int32, (tm, 1), 0)
        real = row < nvalid
        h = jnp.where(real, hx_ref[:, 0:d], 0.0).astype(BF16)
        wts = jnp.where(real, hx_ref[:, d:d + LANES], 0.0)

        def expert(w1_ref, w3_ref, w2_ref):
            a = _dot(h, w1_ref[...])
            g = (a * jax.nn.sigmoid(a)) * _dot(h, w3_ref[...])
            return _dot(g.astype(BF16), w2_ref[...])

        y_ref[...] = (wts[:, 0:1] * expert(w1a_ref, w3a_ref, w2a_ref)
                      + wts[:, 1:2] * expert(w1b_ref, w3b_ref, w2b_ref))


def _moe_routed(hx_sorted, ea, eb, valid, w1, w3, w2, tile_rows):
    n_slots, w = hx_sorted.shape
    ne, d, de = w1.shape
    up = lambda sel: pl.BlockSpec((None, d, de), lambda j, ea, eb, valid: ((ea, eb)[sel][j], 0, 0))
    down = lambda sel: pl.BlockSpec((None, de, d), lambda j, ea, eb, valid: ((ea, eb)[sel][j], 0, 0))
    grid_spec = pltpu.PrefetchScalarGridSpec(
        num_scalar_prefetch=3,
        grid=(n_slots // tile_rows,),
        in_specs=[pl.BlockSpec((tile_rows, w), lambda j, ea, eb, valid: (j, 0)),
                  up(0), up(0), down(0), up(1), up(1), down(1)],
        out_specs=pl.BlockSpec((tile_rows, d), lambda j, ea, eb, valid: (j, 0)),
    )
    return pl.pallas_call(
        _moe_routed_kernel,
        out_shape=jax.ShapeDtypeStruct((n_slots, d), F32),
        grid_spec=grid_spec,
        compiler_params=_cparams("arbitrary"),
        name="moe_routed",
    )(ea, eb, valid, hx_sorted, w1, w3, w2, w1, w3, w2)


def _combine_kernel(pos_ref, ys_ref, x1_ref, gt2_ref, y_ref, buf, sem):
    tm = buf.shape[0]

    def row_copy(r):
        return pltpu.make_async_copy(ys_ref.at[pl.ds(pos_ref[0, r], 1)], buf.at[pl.ds(r, 1)], sem)

    _start_and_wait_rows(row_copy, tm)
    y_ref[...] = x1_ref[...] + gt2_ref[...] * buf[...]


def _combine(pos, ys, x1, gt2, tm, rows_per_group):
    t, d = x1.shape
    return pl.pallas_call(
        _combine_kernel,
        out_shape=jax.ShapeDtypeStruct((t, d), F32),
        grid=(t // tm,),
        in_specs=[pl.BlockSpec((None, 1, tm), lambda i: (i, 0, 0), memory_space=pltpu.SMEM),
                  pl.BlockSpec(memory_space=pl.ANY),
                  pl.BlockSpec((tm, d), lambda i: (i, 0)),
                  _mod_spec(gt2, tm, rows_per_group, 1)],
        out_specs=pl.BlockSpec((tm, d), lambda i: (i, 0)),
        scratch_shapes=[pltpu.VMEM((tm, d), F32), pltpu.SemaphoreType.DMA],
        compiler_params=_cparams("arbitrary"),
        name="combine",
    )(pos.reshape(t // tm, 1, tm), ys, x1, gt2)


def _moe_kernel(h_ref, cmb_ref, x1_ref, gt2_ref, w1_ref, w3_ref, w2_ref, y_ref, acc_scr):
    e = pl.program_id(1)

    @pl.when(e == 0)
    def _():
        acc_scr[...] = jnp.zeros_like(acc_scr)

    h = h_ref[...]
    a = _dot(h, w1_ref[...])
    g = (a * jax.nn.sigmoid(a)) * _dot(h, w3_ref[...])
    ff = _dot(g.astype(BF16), w2_ref[...])
    cmb = cmb_ref[...]
    lane = lax.broadcasted_iota(jnp.int32, cmb.shape, 1)
    w = jnp.sum(jnp.where(lane == e, cmb, 0.0), axis=-1, keepdims=True)
    acc_scr[...] += w * ff

    @pl.when(e == pl.num_programs(1) - 1)
    def _():
        y_ref[...] = x1_ref[...] + gt2_ref[...] * acc_scr[...]


def _moe(h2, cmb, x1, gt2, w1, w3, w2, tm, rows_per_group):
    t, d = x1.shape
    ne, _, de = w1.shape
    gt2_spec = (pl.BlockSpec((None, 1, d), lambda i, e: (i // (rows_per_group // tm), 0, 0))
                if gt2.ndim == 3 else pl.BlockSpec((tm, d), lambda i, e: (i, 0)))
    return pl.pallas_call(
        _moe_kernel,
        out_shape=jax.ShapeDtypeStruct((t, d), F32),
        grid=(t // tm, ne),
        in_specs=[pl.BlockSpec((tm, d), lambda i, e: (i, 0)),
                  pl.BlockSpec((tm, LANES), lambda i, e: (i, 0)),
                  pl.BlockSpec((tm, d), lambda i, e: (i, 0)),
                  gt2_spec,
                  pl.BlockSpec((None, d, de), lambda i, e: (e, 0, 0)),
                  pl.BlockSpec((None, d, de), lambda i, e: (e, 0, 0)),
                  pl.BlockSpec((None, de, d), lambda i, e: (e, 0, 0))],
        out_specs=pl.BlockSpec((tm, d), lambda i, e: (i, 0)),
        scratch_shapes=[pltpu.VMEM((tm, d), F32)],
        compiler_params=_cparams("arbitrary", "arbitrary"),
        name="moe",
    )(h2, cmb, x1, gt2, w1, w3, w2)


def _rope_tables(pos, gq, gk):
    posf = pos.astype(F32)[:, None]
    inv_r = ROPE_THETA ** (-jnp.arange(0, RET_DK, 2, dtype=F32) / RET_DK)
    ang_r = posf * inv_r[None, :]
    inv_d = ROPE_THETA ** (-jnp.arange(0, DIFF_DH, 2, dtype=F32) / DIFF_DH)
    lane = jnp.arange(LANES)
    half = DIFF_DH // 2
    ang_d = posf * inv_d[lane % half][None, :]
    first_half = ((lane % DIFF_DH) < half)[None, :]
    cos_d, sin_d = jnp.cos(ang_d), jnp.sin(ang_d)

    def diff_tables(g, scale):
        g128 = jnp.tile(g, LANES // DIFF_DH)
        return (cos_d * (g128 * scale)[None, :],
                jnp.where(first_half, -sin_d, 0.0) * (jnp.roll(g128, -half) * scale)[None, :],
                jnp.where(first_half, 0.0, sin_d) * (jnp.roll(g128, half) * scale)[None, :])

    return (jnp.cos(ang_r), jnp.sin(ang_r)) + diff_tables(gq, DIFF_DH ** -0.5 * LOG2E) + diff_tables(gk, 1.0)


def _tile_rows(n, target):
    tm = min(n, target)
    assert n % tm == 0
    return tm


def _layer(x, mod, pos, s0, kc, vc, lw, li, ret_chunk):
    nb, seq, d = x.shape
    t = nb * seq
    x2d = x.reshape(t, d)
    per_token = seq < 128
    if per_token:
        mods = [jnp.repeat(mod[:, k, :], seq, axis=0) for k in range(6)]
        tabs = tuple(jnp.tile(tb, (nb, 1)) for tb in _rope_tables(pos, lw['gq'], lw['gk']))
        rows_per_group = t
    else:
        mods = [mod[:, k:k + 1, :] for k in range(6)]
        tabs = _rope_tables(pos, lw['gq'], lw['gk'])
        rows_per_group = seq
    sh1, sc1, gt1, sh2, sc2, gt2 = mods
    tm = _tile_rows(rows_per_group, 512)

    p, qd, k2d, v2d = _inproj(x2d, sc1, sh1, lw['norm1_g'], lw['w_in'], tabs, lw['bd'],
                              _tile_rows(rows_per_group, 1024), rows_per_group)
    ret_o, s_new = _retention(p, s0, nb, seq, ret_chunk)
    lam_init = 0.8 - 0.6 * math.exp(-0.3 * li)
    if kc is None:
        dif_o = _diffattn(qd, k2d, v2d, lw['lam'], lw['subln_g'], nb, seq, lam_init)
    else:
        dif_o = _diffattn_cached(qd, k2d, v2d, kc, vc, li, lw['lam'], lw['subln_g'], nb, seq, lam_init)
    merge_args = (ret_o, dif_o, p, x2d, gt1, sc2, sh2, lw['norm2_g'], lw['w_br_ret'], lw['w_br_diff'],
                  lw['w_out'], lw['rw_hi'], lw['rw_lo'], lw['rb'], tm, rows_per_group)
    if per_token:
        x1, h2, cmb = _merge(*merge_args, routed=False)
        y = _moe(h2, cmb, x1, gt2, lw['w1'], lw['w3'], lw['w2'], tm, rows_per_group)
    else:
        x1, hx, cls, hist = _merge(*merge_args, routed=True)
        n_tiles = SUBLANES * pl.cdiv(t // MOE_TILE_ROWS + N_CLASSES, SUBLANES)
        pos, tinfo = _route(cls, hist, MOE_TILE_ROWS, n_tiles, tm)
        hx_sorted = _dispatch(pos, hx, n_tiles * MOE_TILE_ROWS, tm)
        y_sorted = _moe_routed(hx_sorted, tinfo[:, 0], tinfo[:, 1], tinfo[:, 2],
                               lw['w1'], lw['w3'], lw['w2'], MOE_TILE_ROWS)
        y = _combine(pos, y_sorted, x1, gt2, tm, rows_per_group)
    return (y.reshape(nb, seq, d), s_new, k2d.reshape(nb, seq, DIFF_HEADS, 2 * DIFF_DH),
            v2d.reshape(nb, seq, DIFF_HEADS, DIFF_DV))


def kernel(x_prompt, x_sample, c_prompt, c_sample, cache_diff_k, cache_diff_v, state_ret, w_ada, b_ada, norm1_g, w_in, diff_qnorm_g, diff_knorm_g, diff_lambda_q1, diff_lambda_k1, diff_lambda_q2, diff_lambda_k2, diff_subln_g, w_br_ret, w_br_diff, w_out, norm2_g, w_group, b_group, w_expert_router, b_expert, w1, w3, w2):
    nb, seq, d = x_prompt.shape
    nbs, seqs, _ = x_sample.shape
    depth = w_in.shape[0]
    past = cache_diff_k.shape[2]
    pos_p = jnp.arange(seq, dtype=jnp.int32)
    pos_s = past + jnp.arange(seqs, dtype=jnp.int32)
    c_all = jnp.concatenate([c_prompt, c_sample], axis=0)
    col = jnp.arange(MXU_COLS)
    bd = jnp.where((col[:, None] // DIFF_DH) == (col[None, :] // DIFF_DH), 1.0 / DIFF_DH, 0.0).astype(BF16)

    xp, xs = x_prompt, x_sample
    outs = [[] for _ in range(6)]
    for li in range(depth):
        rw = jnp.zeros((d, LANES), F32)
        rw = rw.at[:, :N_GROUPS].set(w_group[li]).at[:, N_GROUPS:N_GROUPS + N_EXPERTS].set(w_expert_router[li])
        rw_hi = rw.astype(BF16)
        rb = jnp.zeros((1, LANES), F32)
        rb = rb.at[0, :N_GROUPS].set(b_group[li]).at[0, N_GROUPS:N_GROUPS + N_EXPERTS].set(b_expert[li])
        lw = {
            'norm1_g': norm1_g[li].reshape(1, d), 'norm2_g': norm2_g[li].reshape(1, d),
            'w_in': jnp.stack([w_in[li, :, blk * COL_BLOCK:(blk + 1) * COL_BLOCK] for blk in IN_BLOCK_ORDER]
                              ).astype(BF16),
            'gq': diff_qnorm_g[li], 'gk': diff_knorm_g[li],
            'bd': bd,
            'lam': jnp.stack([diff_lambda_q1[li], diff_lambda_k1[li], diff_lambda_q2[li], diff_lambda_k2[li]]),
            'subln_g': diff_subln_g[li].reshape(1, DIFF_DV),
            'w_br_ret': w_br_ret[li].astype(BF16), 'w_br_diff': w_br_diff[li].astype(BF16),
            'w_out': w_out[li].astype(BF16),
            'rw_hi': rw_hi, 'rw_lo': (rw - rw_hi.astype(F32)).astype(BF16), 'rb': rb,
            'w1': w1[li].astype(BF16), 'w3': w3[li].astype(BF16), 'w2': w2[li].astype(BF16),
        }
        mod = _adaln(c_all, w_ada[li], b_ada[li]).reshape(nb + nbs, 6, d)
        xp, sp, kp, vp = _layer(xp, mod[:nb], pos_p, None, None, None, lw, li, ret_chunk=min(seq, 256))
        xs, ss, kss, vss = _layer(xs, mod[nb:], pos_s, state_ret[li], cache_diff_k, cache_diff_v,
                                  lw, li, ret_chunk=seqs)
        for lst, val in zip(outs, (kp, vp, sp, kss, vss, ss)):
            lst.append(val)
    kp, vp, sp, kss, vss, ss = (jnp.stack(lst) for lst in outs)
    return (xp, xs, kp, vp, sp.astype(x_prompt.dtype), kss, vss, ss.astype(state_ret.dtype))
```

```python
import functools
import math

import jax
import jax.numpy as jnp
from jax import lax
from jax.experimental import pallas as pl
from jax.experimental.pallas import tpu as pltpu

F32 = jnp.float32
BF16 = jnp.bfloat16

D_MODEL = 1024
CHUNK = 64
ROPE_THETA = 10000.0
EPS = 1e-6
RET_HEADS = 4
RET_DK = D_MODEL // RET_HEADS
RET_DV = 2 * RET_DK
DIFF_HEADS = 8
DIFF_DH = D_MODEL // (2 * DIFF_HEADS)
DIFF_DV = 2 * DIFF_DH
N_GROUPS = 4
EXPERTS_PER_GROUP = 4
N_EXPERTS = N_GROUPS * EXPERTS_PER_GROUP
D_EXPERT = 512
N_PAIRS = EXPERTS_PER_GROUP * (EXPERTS_PER_GROUP - 1) // 2
N_CLASSES = N_GROUPS * N_PAIRS
MOE_TILE_ROWS = 512
ATTN_ROW_PART = 256
MERGE_ROW_PARTS = 1
DMA_UNROLL = 8

LANES = 128
SUBLANES = 8
MXU_COLS = 256
COL_BLOCK = 1024
IN_BLOCK_ORDER = (0, 1, 2, 3, 4, 5, 6, 9, 10, 7, 8)
N_SLAB_BLOCKS = 8
STRIPS = 4
SLAB_RQ, SLAB_RK, SLAB_RV, SLAB_RG, SLAB_GATE = 0, 1, 2, 4, 6
VMEM_LIMIT = 56 * 1024 * 1024
LOG2E = math.log2(math.e)


def _cparams(*sem):
    return pltpu.CompilerParams(dimension_semantics=sem, vmem_limit_bytes=VMEM_LIMIT)


def _dot(a, b):
    return jnp.dot(a, b, preferred_element_type=F32)


def _dot_nt(a, b):
    return lax.dot_general(a, b, (((1,), (1,)), ((), ())), preferred_element_type=F32)


def _adaln_kernel(c_ref, w_ref, b_ref, o_ref):
    c = c_ref[...]
    a = c * jax.nn.sigmoid(c)
    o_ref[...] = jnp.dot(a, w_ref[...], preferred_element_type=F32,
                         precision=lax.Precision.HIGHEST) + b_ref[...]


def _adaln(c, w_ada, b_ada):
    rows, d = c.shape
    n = w_ada.shape[1]
    tn = 1536
    return pl.pallas_call(
        _adaln_kernel,
        out_shape=jax.ShapeDtypeStruct((rows, n), F32),
        grid=(n // tn,),
        in_specs=[pl.BlockSpec((rows, d), lambda j: (0, 0)),
                  pl.BlockSpec((d, tn), lambda j: (0, j)),
                  pl.BlockSpec((1, tn), lambda j: (0, j))],
        out_specs=pl.BlockSpec((rows, tn), lambda j: (0, j)),
        compiler_params=_cparams("arbitrary"),
        name="adaln",
    )(c, w_ada, b_ada.reshape(1, n))


def _rms(x):
    return x * lax.rsqrt(jnp.mean(x * x, axis=-1, keepdims=True) + EPS)


def _inproj_kernel(x_ref, sc_ref, sh_ref, g1_ref, w_ref, rc_ref, rs_ref, qc_ref, qsn_ref, qsp_ref,
                   kc_ref, ksn_ref, ksp_ref, bd_ref, p_ref, qd_ref, k_ref, v_ref, h_scr):
    j = pl.program_id(1)

    @pl.when(j == 0)
    def _():
        h = _rms(x_ref[...]) * g1_ref[...]
        h = h * (1.0 + sc_ref[...]) + sh_ref[...]
        h_scr[...] = h.astype(BF16)

    def strips():
        h = h_scr[...]
        starts = list(range(0, COL_BLOCK, MXU_COLS))
        ahead = _dot(h, w_ref[:, 0:MXU_COLS])
        for n, lo in enumerate(starts):
            acc = ahead
            if n + 1 < len(starts):
                ahead = _dot(h, w_ref[:, starts[n + 1]:starts[n + 1] + MXU_COLS])
            yield lo, acc

    def ret_rope(scale):
        c = rc_ref[...]
        s = rs_ref[...]
        for lo, acc in strips():
            x1 = acc[:, 0:LANES]
            x2 = acc[:, LANES:2 * LANES]
            p_ref[lo // MXU_COLS, :, 0:LANES] = ((x1 * c - x2 * s) * scale).astype(BF16)
            p_ref[lo // MXU_COLS, :, LANES:2 * LANES] = ((x1 * s + x2 * c) * scale).astype(BF16)

    def diff_norm():
        for lo, acc in list(strips()):
            k_ref[:, lo:lo + MXU_COLS] = acc * lax.rsqrt(_dot((acc * acc).astype(BF16), bd_ref[...]) + EPS)

    def diff_rope(lo, c_ref, sn_ref, sp_ref, out_ref):
        for off in range(lo, lo + MXU_COLS, LANES):
            yg = k_ref[:, off:off + LANES]
            o = (yg * c_ref[...] + pltpu.roll(yg, LANES - DIFF_DH // 2, 1) * sn_ref[...]
                 + pltpu.roll(yg, DIFF_DH // 2, 1) * sp_ref[...])
            out_ref[:, off:off + LANES] = o.astype(out_ref.dtype)

    def elementwise(fn, out_ref, after_strip=None):
        for lo, acc in strips():
            if out_ref is p_ref:
                p_ref[lo // MXU_COLS] = fn(acc).astype(BF16)
            else:
                out_ref[:, lo:lo + MXU_COLS] = fn(acc).astype(out_ref.dtype)
            if after_strip is not None:
                after_strip(lo)

    @pl.when(j == 0)
    def _():
        ret_rope(1.0)

    @pl.when(j == 1)
    def _():
        ret_rope(RET_DK ** -0.5)

    @pl.when((j == 2) | (j == 3))
    def _():
        elementwise(lambda a: a, p_ref)

    @pl.when((j == 4) | (j == 5))
    def _():
        elementwise(lambda a: a * jax.nn.sigmoid(a), p_ref)

    @pl.when((j == 6) | (j == 9))
    def _():
        diff_norm()

    @pl.when(j == 7)
    def _():
        elementwise(jax.nn.sigmoid, p_ref, lambda lo: diff_rope(lo, qc_ref, qsn_ref, qsp_ref, qd_ref))

    @pl.when(j == 8)
    def _():
        elementwise(jax.nn.sigmoid, p_ref)

    @pl.when(j == 10)
    def _():
        elementwise(lambda a: a, v_ref, lambda lo: diff_rope(lo, kc_ref, ksn_ref, ksp_ref, k_ref))


def _mod_spec(arr, tm, rows_per_group, ngrid):
    d = arr.shape[-1]
    if arr.ndim == 3:
        tiles_per_group = rows_per_group // tm
        if ngrid == 2:
            return pl.BlockSpec((None, 1, d), lambda i, j: (i // tiles_per_group, 0, 0))
        return pl.BlockSpec((None, 1, d), lambda i: (i // tiles_per_group, 0, 0))
    if ngrid == 2:
        return pl.BlockSpec((tm, d), lambda i, j: (i, 0))
    return pl.BlockSpec((tm, d), lambda i: (i, 0))


def _inproj(x2d, sc1, sh1, g1, w_blocks, tabs, bd, tm, rows_per_group):
    t, d = x2d.shape
    nrep = tabs[0].shape[0] // tm

    def slab_map(i, j):
        return (jnp.where(j < 6, j, jnp.where(j < 9, jnp.maximum(j - 1, 5), N_SLAB_BLOCKS - 1)), i, 0)

    tab_spec = pl.BlockSpec((tm, LANES), lambda i, j: (i % nrep, 0))
    row_spec = pl.BlockSpec((tm, d), lambda i, j: (i, 0))
    return pl.pallas_call(
        _inproj_kernel,
        out_shape=(jax.ShapeDtypeStruct((N_SLAB_BLOCKS * STRIPS, t, MXU_COLS), BF16),
                   jax.ShapeDtypeStruct((t, d), BF16),
                   jax.ShapeDtypeStruct((t, d), F32),
                   jax.ShapeDtypeStruct((t, d), F32)),
        grid=(t // tm, len(IN_BLOCK_ORDER)),
        in_specs=[row_spec,
                  _mod_spec(sc1, tm, rows_per_group, 2),
                  _mod_spec(sh1, tm, rows_per_group, 2),
                  pl.BlockSpec((1, d), lambda i, j: (0, 0)),
                  pl.BlockSpec((None, d, COL_BLOCK), lambda i, j: (j, 0, 0)),
                  *([tab_spec] * len(tabs)),
                  pl.BlockSpec((MXU_COLS, MXU_COLS), lambda i, j: (0, 0))],
        out_specs=(pl.BlockSpec((STRIPS, tm, MXU_COLS), slab_map), row_spec, row_spec, row_spec),
        scratch_shapes=[pltpu.VMEM((tm, d), BF16)],
        compiler_params=_cparams("arbitrary", "arbitrary"),
        name="inproj",
    )(x2d, sc1, sh1, g1, w_blocks, *tabs, bd)


def _retention_kernel(*refs, has_state, chunk):
    if has_state:
        q_ref, k_ref, v_ref, rg_ref, dm_ref, dv_ref, s0_ref, o_ref, sout_ref = refs
        s = s0_ref[...]
    else:
        q_ref, k_ref, v_ref, rg_ref, dm_ref, dv_ref, o_ref, sout_ref = refs
        s = jnp.zeros(sout_ref.shape, F32)
    dvec = dv_ref[...]
    q_decay = dvec[:, 0:1]
    k_decay = dvec[:, 1:2]
    chunk_decay = dvec[0:1, 2:3]

    def wide(ref, rows):
        return jnp.concatenate([ref[0, rows, :], ref[1, rows, :]], axis=1)

    for c in range(q_ref.shape[0] // chunk):
        rows = slice(c * chunk, (c + 1) * chunk)
        q = q_ref[rows, :]
        k = k_ref[rows, :]
        v = wide(v_ref, rows)
        scores = (_dot_nt(q, k) * dm_ref[...]).astype(BF16)
        o = _dot(scores, v) + _dot(q, s.astype(BF16)) * q_decay
        o = _rms(o) * wide(rg_ref, rows).astype(F32)
        o_ref[rows, :] = o.astype(BF16)
        kd_t = (k.astype(F32) * k_decay).T.astype(BF16)
        s = s * chunk_decay + _dot(kd_t, v)
    sout_ref[...] = s


def _retention(p, s0, nb, seq, chunk):
    t = p.shape[1]
    idx = jnp.arange(chunk, dtype=F32)
    log_gamma = jnp.log1p(-jnp.exp2(-5.0 - jnp.arange(RET_HEADS, dtype=F32)))
    rel = idx[:, None] - idx[None, :]
    dmat = jnp.where(rel >= 0, jnp.exp(jnp.maximum(rel, 0.0)[None] * log_gamma[:, None, None]), 0.0)
    q_decay = jnp.exp((idx + 1.0)[None, :] * log_gamma[:, None])
    k_decay = jnp.exp((chunk - 1.0 - idx)[None, :] * log_gamma[:, None])
    c_decay = jnp.broadcast_to(jnp.exp(chunk * log_gamma)[:, None], (RET_HEADS, chunk))
    dvec = jnp.zeros((RET_HEADS, chunk, LANES), F32)
    dvec = dvec.at[:, :, 0].set(q_decay).at[:, :, 1].set(k_decay).at[:, :, 2].set(c_decay)

    vstrips = RET_DV // MXU_COLS
    in_specs = [pl.BlockSpec((None, seq, RET_DK), lambda b, h: (SLAB_RQ * STRIPS + h, b, 0)),
                pl.BlockSpec((None, seq, RET_DK), lambda b, h: (SLAB_RK * STRIPS + h, b, 0)),
                pl.BlockSpec((vstrips, seq, MXU_COLS), lambda b, h: (SLAB_RV * STRIPS // vstrips + h, b, 0)),
                pl.BlockSpec((vstrips, seq, MXU_COLS), lambda b, h: (SLAB_RG * STRIPS // vstrips + h, b, 0)),
                pl.BlockSpec((None, chunk, chunk), lambda b, h: (h, 0, 0)),
                pl.BlockSpec((None, chunk, LANES), lambda b, h: (h, 0, 0))]
    args = [p, p, p, p, dmat, dvec]
    state_spec = pl.BlockSpec((None, None, RET_DK, RET_DV), lambda b, h: (b, h, 0, 0))
    if s0 is not None:
        in_specs.append(state_spec)
        args.append(s0)
    return pl.pallas_call(
        functools.partial(_retention_kernel, has_state=s0 is not None, chunk=chunk),
        out_shape=(jax.ShapeDtypeStruct((RET_HEADS, t, RET_DV), BF16),
                   jax.ShapeDtypeStruct((nb, RET_HEADS, RET_DK, RET_DV), F32)),
        grid=(nb, RET_HEADS),
        in_specs=in_specs,
        out_specs=(pl.BlockSpec((None, seq, RET_DV), lambda b, h: (h, b, 0)), state_spec),
        compiler_params=_cparams("arbitrary", "arbitrary"),
        name="retention",
    )(*args)


def _lambda(lam_ref, lam_init):
    l1 = jnp.sum(lam_ref[0:1, :] * lam_ref[1:2, :], axis=-1, keepdims=True)
    l2 = jnp.sum(lam_ref[2:3, :] * lam_ref[3:4, :], axis=-1, keepdims=True)
    return jnp.exp(l1) - jnp.exp(l2) + lam_init


def _split_maps(q):
    lane = lax.broadcasted_iota(jnp.int32, q.shape, 1)
    zero = jnp.zeros_like(q)
    return jnp.where(lane < DIFF_DH, q, zero), jnp.where(lane >= DIFF_DH, q, zero)


def _diffattn_kernel(q_ref, k_ref, v_ref, lam_ref, g_ref, o_ref, kb_scr, vb_scr, *, part, lam_init):
    seq = k_ref.shape[0]
    kb_scr[...] = k_ref[...].astype(BF16)
    vb_scr[:, 0:DIFF_DV] = v_ref[...].astype(BF16)
    vb_scr[:, DIFF_DV:2 * DIFF_DV] = jnp.ones((seq, DIFF_DV), BF16)
    lam = _lambda(lam_ref, lam_init)
    row = lax.broadcasted_iota(jnp.int32, (part, part), 0)
    col = lax.broadcasted_iota(jnp.int32, (part, part), 1)
    visible = (col // CHUNK) <= (row // CHUNK)

    def scores(q0):
        qmaps = _split_maps(q_ref[q0:q0 + part, :])
        kblk = kb_scr[0:q0 + part, :]
        out = []
        for mp in range(2):
            sc = _dot_nt(qmaps[mp], kblk)
            diag = jnp.where(visible, sc[:, q0:q0 + part], -1e30)
            out.append(jnp.concatenate([sc[:, 0:q0], diag], axis=1) if q0 else diag)
        return out

    starts = list(range(0, seq, part))
    pending = scores(starts[0])
    for n, q0 in enumerate(starts):
        scs = pending
        if n + 1 < len(starts):
            pending = scores(starts[n + 1])
        vblk = vb_scr[0:q0 + part, :]
        att = []
        for mp in range(2):
            sc = scs[mp]
            pv = _dot(jnp.exp2(sc - jnp.max(sc, axis=-1, keepdims=True)).astype(BF16), vblk)
            att.append(pv[:, 0:DIFF_DV] / pv[:, DIFF_DV:2 * DIFF_DV])
        o = att[0] - lam * att[1]
        o = _rms(o) * g_ref[...] * (1.0 - lam_init)
        o_ref[q0:q0 + part, :] = o.astype(BF16)


def _diffattn(qd, k2d, v2d, lam_vecs, subln_g, nb, seq, lam_init):
    t = qd.shape[0]
    part = min(seq, ATTN_ROW_PART)
    assert seq % part == 0 and part % CHUNK == 0
    return pl.pallas_call(
        functools.partial(_diffattn_kernel, part=part, lam_init=lam_init),
        out_shape=jax.ShapeDtypeStruct((t, DIFF_HEADS * DIFF_DV), BF16),
        grid=(nb, DIFF_HEADS),
        in_specs=[pl.BlockSpec((seq, DIFF_DV), lambda b, h: (b, h)),
                  pl.BlockSpec((seq, DIFF_DV), lambda b, h: (b, h)),
                  pl.BlockSpec((seq, DIFF_DV), lambda b, h: (b, h)),
                  pl.BlockSpec((4, DIFF_DH), lambda b, h: (0, 0)),
                  pl.BlockSpec((1, DIFF_DV), lambda b, h: (0, 0))],
        out_specs=pl.BlockSpec((seq, DIFF_DV), lambda b, h: (b, h)),
        scratch_shapes=[pltpu.VMEM((seq, DIFF_DV), BF16), pltpu.VMEM((seq, 2 * DIFF_DV), BF16)],
        compiler_params=_cparams("arbitrary", "arbitrary"),
        name="diffattn",
    )(qd, k2d, v2d, lam_vecs, subln_g)


def _diffattn_cached_kernel(q_ref, kc_ref, vc_ref, kn_ref, vn_ref, lam_ref, g_ref, o_ref, k_scr, v_scr,
                            *, past, new, lam_init):
    total = k_scr.shape[0]
    pad = jnp.zeros((total - past, DIFF_DV), BF16)
    qpos = past + lax.broadcasted_iota(jnp.int32, (new, total), 0)
    kpos = lax.broadcasted_iota(jnp.int32, (new, total), 1)
    visible = ((kpos // CHUNK) <= (qpos // CHUNK)) & (kpos < past + new)
    lam = _lambda(lam_ref, lam_init)

    for h in range(DIFF_HEADS):
        cols = slice(h * DIFF_DV, (h + 1) * DIFF_DV)
        k_scr[0:past, :] = kc_ref[:, h, :].astype(BF16)
        v_scr[0:past, :] = vc_ref[:, h, :].astype(BF16)
        k_scr[past:total, :] = pad
        v_scr[past:total, :] = pad
        k_scr[past:past + new, :] = kn_ref[:, cols].astype(BF16)
        v_scr[past:past + new, :] = vn_ref[:, cols].astype(BF16)
        kall = k_scr[...]
        vall = v_scr[...]

        def softmax_av(qm):
            sc = jnp.where(visible, _dot_nt(qm, kall), -1e30)
            m = jnp.max(sc, axis=-1, keepdims=True)
            p = jnp.exp2(sc - m)
            return _dot(p.astype(BF16), vall) / jnp.sum(p, axis=-1, keepdims=True)

        qa, qb = _split_maps(q_ref[:, cols])
        o = softmax_av(qa) - lam * softmax_av(qb)
        o = _rms(o) * g_ref[...] * (1.0 - lam_init)
        o_ref[:, cols] = o.astype(BF16)


def _diffattn_cached(qd, k2d, v2d, kc, vc, li, lam_vecs, subln_g, nb, new, lam_init):
    t = qd.shape[0]
    past = kc.shape[2]
    width = DIFF_HEADS * DIFF_DV
    total = past + LANES * pl.cdiv(new, LANES)
    cache_spec = pl.BlockSpec((None, None, past, DIFF_HEADS, DIFF_DV), lambda b: (li, b, 0, 0, 0))
    return pl.pallas_call(
        functools.partial(_diffattn_cached_kernel, past=past, new=new, lam_init=lam_init),
        out_shape=jax.ShapeDtypeStruct((t, width), BF16),
        grid=(nb,),
        in_specs=[pl.BlockSpec((new, width), lambda b: (b, 0)),
                  cache_spec, cache_spec,
                  pl.BlockSpec((new, width), lambda b: (b, 0)),
                  pl.BlockSpec((new, width), lambda b: (b, 0)),
                  pl.BlockSpec((4, DIFF_DH), lambda b: (0, 0)),
                  pl.BlockSpec((1, DIFF_DV), lambda b: (0, 0))],
        out_specs=pl.BlockSpec((new, width), lambda b: (b, 0)),
        scratch_shapes=[pltpu.VMEM((total, DIFF_DV), BF16), pltpu.VMEM((total, DIFF_DV), BF16)],
        compiler_params=_cparams("arbitrary"),
        name="diffattn_cached",
    )(qd, kc, vc, k2d, v2d, lam_vecs, subln_g)


def _merge_kernel(ret_ref, dif_ref, gr_ref, gd_ref, x_ref, gt1_ref, sc2_ref, sh2_ref, g2_ref,
                  wr_ref, wd_ref, wo_ref, rwh_ref, rwl_ref, rb_ref, x1_ref, *out_refs, routed):
    tm, d = x_ref.shape
    n_parts = MERGE_ROW_PARTS if tm % (MERGE_ROW_PARTS * LANES) == 0 else 1
    part = tm // n_parts

    def mod_rows(ref, rows):
        return ref[rows, :] if ref.shape[0] > 1 else ref[...]

    hist = jnp.zeros((1, LANES), F32)
    for r in range(n_parts):
        rows = slice(r * part, (r + 1) * part)
        def lanes(ref):
            return jnp.concatenate([ref[n, rows, :] for n in range(ref.shape[0])], axis=1)

        a = _dot(lanes(ret_ref), wr_ref[...])
        b = _dot(dif_ref[rows, :], wd_ref[...])
        merged = lanes(gr_ref).astype(F32) * a + lanes(gd_ref).astype(F32) * b
        x1 = x_ref[rows, :] + mod_rows(gt1_ref, rows) * _dot(merged.astype(BF16), wo_ref[...])
        x1_ref[rows, :] = x1
        h2 = _rms(x1) * g2_ref[...]
        h2 = h2 * (1.0 + mod_rows(sc2_ref, rows)) + mod_rows(sh2_ref, rows)

        hi = h2.astype(BF16)
        lo = (h2 - hi.astype(F32)).astype(BF16)
        logits = _dot(hi, rwh_ref[...]) + _dot(lo, rwh_ref[...]) + _dot(hi, rwl_ref[...]) + rb_ref[...]

        lane = lax.broadcasted_iota(jnp.int32, logits.shape, 1)
        big = jnp.int32(4 * LANES)

        def masked_argmax(mask):
            vmax = jnp.max(jnp.where(mask, logits, -jnp.inf), axis=-1, keepdims=True)
            idx = jnp.min(jnp.where(mask & (logits == vmax), lane, big), axis=-1, keepdims=True)
            return vmax, idx

        is_group = lane < N_GROUPS
        gmax, gidx = masked_argmax(is_group)
        gprob = 1.0 / jnp.sum(jnp.where(is_group, jnp.exp(logits - gmax), 0.0), axis=-1, keepdims=True)
        first = N_GROUPS + gidx * EXPERTS_PER_GROUP
        in_group = (lane >= first) & (lane < first + EXPERTS_PER_GROUP)
        v1, i1 = masked_argmax(in_group)
        v2, i2 = masked_argmax(in_group & (lane != i1))
        e2 = jnp.exp(v2 - v1)
        w1 = gprob / (1.0 + e2)
        w2 = gprob * e2 / (1.0 + e2)
        if not routed:
            h2_ref, cmb_ref = out_refs
            h2_ref[rows, :] = h2.astype(BF16)
            cmb_ref[rows, :] = (jnp.where(lane + N_GROUPS == i1, w1, 0.0)
                                + jnp.where(lane + N_GROUPS == i2, w2, 0.0))
            continue

        hx_ref, cls_ref, hist_ref = out_refs
        first_is_lo = i1 < i2
        lo_e = jnp.minimum(i1, i2) - first
        hi_e = jnp.maximum(i1, i2) - first
        cls = gidx * N_PAIRS + jnp.right_shift(lo_e * (7 - lo_e), 1) + hi_e - lo_e - 1
        hx_ref[rows, 0:d] = h2
        hx_ref[rows, d:d + LANES] = jnp.where(lane == 0, jnp.where(first_is_lo, w1, w2),
                                              jnp.where(lane == 1, jnp.where(first_is_lo, w2, w1), 0.0))
        cls_ref[rows, :] = cls
        hist = hist + jnp.sum(jnp.where(lane == cls, 1.0, 0.0), axis=0, keepdims=True)

    if routed:
        hist_ref = out_refs[2]

        @pl.when(pl.program_id(0) == 0)
        def _():
            hist_ref[...] = jnp.zeros_like(hist_ref)

        hist_ref[...] += hist


def _merge(ret_o, dif_o, p, x2d, gt1, sc2, sh2, g2, wr, wd, wo, rwh, rwl, rb, tm, rows_per_group, routed):
    t, d = x2d.shape
    const = lambda shape: pl.BlockSpec(shape, lambda i: (0, 0), pipeline_mode=pl.Buffered(1))
    if routed:
        out_shape = (jax.ShapeDtypeStruct((t, d), F32),
                     jax.ShapeDtypeStruct((t, d + LANES), F32),
                     jax.ShapeDtypeStruct((t, 1), jnp.int32),
                     jax.ShapeDtypeStruct((1, LANES), F32))
        out_specs = (pl.BlockSpec((tm, d), lambda i: (i, 0)),
                     pl.BlockSpec((tm, d + LANES), lambda i: (i, 0)),
                     pl.BlockSpec((tm, 1), lambda i: (i, 0)),
                     pl.BlockSpec((1, LANES), lambda i: (0, 0)))
    else:
        out_shape = (jax.ShapeDtypeStruct((t, d), F32),
                     jax.ShapeDtypeStruct((t, d), BF16),
                     jax.ShapeDtypeStruct((t, LANES), F32))
        out_specs = (pl.BlockSpec((tm, d), lambda i: (i, 0)),
                     pl.BlockSpec((tm, d), lambda i: (i, 0)),
                     pl.BlockSpec((tm, LANES), lambda i: (i, 0)))
    return pl.pallas_call(
        functools.partial(_merge_kernel, routed=routed),
        out_shape=out_shape,
        grid=(t // tm,),
        in_specs=[pl.BlockSpec((RET_HEADS, tm, RET_DV), lambda i: (0, i, 0)),
                  pl.BlockSpec((tm, d), lambda i: (i, 0)),
                  pl.BlockSpec((STRIPS, tm, MXU_COLS), lambda i: (SLAB_GATE, i, 0)),
                  pl.BlockSpec((STRIPS, tm, MXU_COLS), lambda i: (SLAB_GATE + 1, i, 0)),
                  pl.BlockSpec((tm, d), lambda i: (i, 0)),
                  _mod_spec(gt1, tm, rows_per_group, 1),
                  _mod_spec(sc2, tm, rows_per_group, 1),
                  _mod_spec(sh2, tm, rows_per_group, 1),
                  const((1, d)),
                  const(wr.shape), const(wd.shape), const(wo.shape),
                  const(rwh.shape), const(rwl.shape), const(rb.shape)],
        out_specs=out_specs,
        compiler_params=_cparams("arbitrary"),
        name="merge",
    )(ret_o, dif_o, p, p, x2d, gt1, sc2, sh2, g2, wr, wd, wo, rwh, rwl, rb)


def _route_kernel(cls_ref, hist_ref, tri_ref, pos_ref, tinfo_ref, carry_scr, off_scr, *, tile_rows):
    i = pl.program_id(0)

    def excl_cumsum(v):
        lane8 = lax.broadcasted_iota(jnp.int32, v.shape, 1)
        inc = v
        for sh in (1, 2, 4, 8, 16, 32, 64):
            inc = inc + jnp.where(lane8 >= sh, pltpu.roll(inc, sh, 1), 0.0)
        return inc - v

    @pl.when(i == 0)
    def _():
        cnt = hist_ref[...]
        ntile = jnp.floor((cnt + (tile_rows - 1)) * (1.0 / tile_rows))
        start = excl_cumsum(jnp.broadcast_to(ntile, (SUBLANES, LANES)))[0:1, :]
        off_scr[...] = start * tile_rows
        carry_scr[...] = jnp.zeros_like(carry_scr)
        nt = tinfo_ref.shape[0]
        jrow = lax.broadcasted_iota(jnp.int32, (nt, LANES), 0).astype(F32)
        lane = lax.broadcasted_iota(jnp.int32, (nt, LANES), 1)
        is_cls = lane < N_CLASSES
        tcls = jnp.sum(jnp.where(is_cls & (start <= jrow), 1.0, 0.0), axis=-1, keepdims=True) - 1.0
        pick = lane == tcls.astype(jnp.int32)
        cnt_j = jnp.sum(jnp.where(pick, cnt, 0.0), axis=-1, keepdims=True)
        start_j = jnp.sum(jnp.where(pick, start, 0.0), axis=-1, keepdims=True)
        valid = jnp.clip(cnt_j - (jrow[:, 0:1] - start_j) * tile_rows, 0.0, float(tile_rows)).astype(jnp.int32)
        tc = tcls.astype(jnp.int32)
        grp = sum((tc >= g * N_PAIRS).astype(jnp.int32) for g in range(1, N_GROUPS))
        pair = tc - grp * N_PAIRS
        lo = (pair >= 3).astype(jnp.int32) + (pair >= 5).astype(jnp.int32)
        hi = pair - jnp.right_shift(lo * (7 - lo), 1) + lo + 1
        ea = grp * EXPERTS_PER_GROUP + lo
        eb = grp * EXPERTS_PER_GROUP + hi
        tinfo_ref[...] = jnp.where(lane == 0, ea, jnp.where(lane == 1, eb, jnp.where(lane == 2, valid, 0)))

    cls = cls_ref[...]
    lane = lax.broadcasted_iota(jnp.int32, (cls.shape[0], LANES), 1)
    onehot = lane == cls
    earlier = _dot(tri_ref[...], jnp.where(onehot, 1.0, 0.0).astype(BF16))
    base = carry_scr[...] + off_scr[...]
    pos = jnp.sum(jnp.where(onehot, earlier + base, 0.0), axis=-1, keepdims=True)
    pos_ref[...] = pos.astype(jnp.int32)
    carry_scr[...] += jnp.sum(jnp.where(onehot, 1.0, 0.0), axis=0, keepdims=True)


def _route(cls, hist, tile_rows, n_tiles, tm):
    t = cls.shape[0]
    tri = jnp.tril(jnp.ones((tm, tm), BF16), -1)
    return pl.pallas_call(
        functools.partial(_route_kernel, tile_rows=tile_rows),
        out_shape=(jax.ShapeDtypeStruct((t, 1), jnp.int32),
                   jax.ShapeDtypeStruct((n_tiles, LANES), jnp.int32)),
        grid=(t // tm,),
        in_specs=[pl.BlockSpec((tm, 1), lambda i: (i, 0)),
                  pl.BlockSpec((1, LANES), lambda i: (0, 0)),
                  pl.BlockSpec((tm, tm), lambda i: (0, 0))],
        out_specs=(pl.BlockSpec((tm, 1), lambda i: (i, 0)),
                   pl.BlockSpec((n_tiles, LANES), lambda i: (0, 0))),
        scratch_shapes=[pltpu.VMEM((1, LANES), F32), pltpu.VMEM((1, LANES), F32)],
        compiler_params=_cparams("arbitrary"),
        name="route",
    )(cls, hist, tri)


def _start_and_wait_rows(row_copy, n_rows):
    def start(g, carry):
        for u in range(DMA_UNROLL):
            row_copy(g * DMA_UNROLL + u).start(priority=u % 2)
        return carry

    def wait(r, carry):
        row_copy(r).wait()
        return carry

    lax.fori_loop(0, n_rows // DMA_UNROLL, start, 0)
    lax.fori_loop(0, n_rows, wait, 0, unroll=DMA_UNROLL)


def _dispatch_kernel(pos_ref, hx_ref, sorted_ref, sem):
    tm = hx_ref.shape[0]

    def row_copy(r):
        return pltpu.make_async_copy(hx_ref.at[pl.ds(r, 1)], sorted_ref.at[pl.ds(pos_ref[0, r], 1)], sem)

    _start_and_wait_rows(row_copy, tm)


def _dispatch(pos, hx, n_slots, tm):
    t, w = hx.shape
    return pl.pallas_call(
        _dispatch_kernel,
        out_shape=jax.ShapeDtypeStruct((n_slots, w), F32),
        grid=(t // tm,),
        in_specs=[pl.BlockSpec((None, 1, tm), lambda i: (i, 0, 0), memory_space=pltpu.SMEM),
                  pl.BlockSpec((tm, w), lambda i: (i, 0))],
        out_specs=pl.BlockSpec(memory_space=pl.ANY),
        scratch_shapes=[pltpu.SemaphoreType.DMA],
        compiler_params=pltpu.CompilerParams(dimension_semantics=("arbitrary",), vmem_limit_bytes=VMEM_LIMIT,
                                             has_side_effects=True),
        name="dispatch",
    )(pos.reshape(t // tm, 1, tm), hx)


def _moe_routed_kernel(ea_ref, eb_ref, valid_ref, hx_ref, w1a_ref, w3a_ref, w2a_ref, w1b_ref, w3b_ref, w2b_ref, y_ref):
    j = pl.program_id(0)
    nvalid = valid_ref[j]

    @pl.when(nvalid == 0)
    def _():
        y_ref[...] = jnp.zeros_like(y_ref)

    @pl.when(nvalid > 0)
    def _():
        tm, d = y_ref.shape
        row = lax.broadcasted_iota(jnp.int32, (tm, 1), 0)
        real = row < nvalid
        h = jnp.where(real, hx_ref[:, 0:d], 0.0).astype(BF16)
        wts = jnp.where(real, hx_ref[:, d:d + LANES], 0.0)

        def expert(w1_ref, w3_ref, w2_ref):
            a = _dot(h, w1_ref[...])
            g = (a * jax.nn.sigmoid(a)) * _dot(h, w3_ref[...])
            return _dot(g.astype(BF16), w2_ref[...])

        y_ref[...] = (wts[:, 0:1] * expert(w1a_ref, w3a_ref, w2a_ref)
                      + wts[:, 1:2] * expert(w1b_ref, w3b_ref, w2b_ref))


def _moe_routed(hx_sorted, ea, eb, valid, w1, w3, w2, tile_rows):
    n_slots, w = hx_sorted.shape
    ne, d, de = w1.shape
    up = lambda sel: pl.BlockSpec((None, d, de), lambda j, ea, eb, valid: ((ea, eb)[sel][j], 0, 0))
    down = lambda sel: pl.BlockSpec((None, de, d), lambda j, ea, eb, valid: ((ea, eb)[sel][j], 0, 0))
    grid_spec = pltpu.PrefetchScalarGridSpec(
        num_scalar_prefetch=3,
        grid=(n_slots // tile_rows,),
        in_specs=[pl.BlockSpec((tile_rows, w), lambda j, ea, eb, valid: (j, 0)),
                  up(0), up(0), down(0), up(1), up(1), down(1)],
        out_specs=pl.BlockSpec((tile_rows, d), lambda j, ea, eb, valid: (j, 0)),
    )
    return pl.pallas_call(
        _moe_routed_kernel,
        out_shape=jax.ShapeDtypeStruct((n_slots, d), F32),
        grid_spec=grid_spec,
        compiler_params=_cparams("arbitrary"),
        name="moe_routed",
    )(ea, eb, valid, hx_sorted, w1, w3, w2, w1, w3, w2)


def _combine_kernel(pos_ref, ys_ref, x1_ref, gt2_ref, y_ref, buf, sem):
    tm = buf.shape[0]

    def row_copy(r):
        return pltpu.make_async_copy(ys_ref.at[pl.ds(pos_ref[0, r], 1)], buf.at[pl.ds(r, 1)], sem)

    _start_and_wait_rows(row_copy, tm)
    y_ref[...] = x1_ref[...] + gt2_ref[...] * buf[...]


def _combine(pos, ys, x1, gt2, tm, rows_per_group):
    t, d = x1.shape
    return pl.pallas_call(
        _combine_kernel,
        out_shape=jax.ShapeDtypeStruct((t, d), F32),
        grid=(t // tm,),
        in_specs=[pl.BlockSpec((None, 1, tm), lambda i: (i, 0, 0), memory_space=pltpu.SMEM),
                  pl.BlockSpec(memory_space=pl.ANY),
                  pl.BlockSpec((tm, d), lambda i: (i, 0)),
                  _mod_spec(gt2, tm, rows_per_group, 1)],
        out_specs=pl.BlockSpec((tm, d), lambda i: (i, 0)),
        scratch_shapes=[pltpu.VMEM((tm, d), F32), pltpu.SemaphoreType.DMA],
        compiler_params=_cparams("arbitrary"),
        name="combine",
    )(pos.reshape(t // tm, 1, tm), ys, x1, gt2)


def _moe_kernel(h_ref, cmb_ref, x1_ref, gt2_ref, w1_ref, w3_ref, w2_ref, y_ref, acc_scr):
    e = pl.program_id(1)

    @pl.when(e == 0)
    def _():
        acc_scr[...] = jnp.zeros_like(acc_scr)

    h = h_ref[...]
    a = _dot(h, w1_ref[...])
    g = (a * jax.nn.sigmoid(a)) * _dot(h, w3_ref[...])
    ff = _dot(g.astype(BF16), w2_ref[...])
    cmb = cmb_ref[...]
    lane = lax.broadcasted_iota(jnp.int32, cmb.shape, 1)
    w = jnp.sum(jnp.where(lane == e, cmb, 0.0), axis=-1, keepdims=True)
    acc_scr[...] += w * ff

    @pl.when(e == pl.num_programs(1) - 1)
    def _():
        y_ref[...] = x1_ref[...] + gt2_ref[...] * acc_scr[...]


def _moe(h2, cmb, x1, gt2, w1, w3, w2, tm, rows_per_group):
    t, d = x1.shape
    ne, _, de = w1.shape
    gt2_spec = (pl.BlockSpec((None, 1, d), lambda i, e: (i // (rows_per_group // tm), 0, 0))
                if gt2.ndim == 3 else pl.BlockSpec((tm, d), lambda i, e: (i, 0)))
    return pl.pallas_call(
        _moe_kernel,
        out_shape=jax.ShapeDtypeStruct((t, d), F32),
        grid=(t // tm, ne),
        in_specs=[pl.BlockSpec((tm, d), lambda i, e: (i, 0)),
                  pl.BlockSpec((tm, LANES), lambda i, e: (i, 0)),
                  pl.BlockSpec((tm, d), lambda i, e: (i, 0)),
                  gt2_spec,
                  pl.BlockSpec((None, d, de), lambda i, e: (e, 0, 0)),
                  pl.BlockSpec((None, d, de), lambda i, e: (e, 0, 0)),
                  pl.BlockSpec((None, de, d), lambda i, e: (e, 0, 0))],
        out_specs=pl.BlockSpec((tm, d), lambda i, e: (i, 0)),
        scratch_shapes=[pltpu.VMEM((tm, d), F32)],
        compiler_params=_cparams("arbitrary", "arbitrary"),
        name="moe",
    )(h2, cmb, x1, gt2, w1, w3, w2)


def _rope_tables(pos, gq, gk):
    posf = pos.astype(F32)[:, None]
    inv_r = ROPE_THETA ** (-jnp.arange(0, RET_DK, 2, dtype=F32) / RET_DK)
    ang_r = posf * inv_r[None, :]
    inv_d = ROPE_THETA ** (-jnp.arange(0, DIFF_DH, 2, dtype=F32) / DIFF_DH)
    lane = jnp.arange(LANES)
    half = DIFF_DH // 2
    ang_d = posf * inv_d[lane % half][None, :]
    first_half = ((lane % DIFF_DH) < half)[None, :]
    cos_d, sin_d = jnp.cos(ang_d), jnp.sin(ang_d)

    def diff_tables(g, scale):
        g128 = jnp.tile(g, LANES // DIFF_DH)
        return (cos_d * (g128 * scale)[None, :],
                jnp.where(first_half, -sin_d, 0.0) * (jnp.roll(g128, -half) * scale)[None, :],
                jnp.where(first_half, 0.0, sin_d) * (jnp.roll(g128, half) * scale)[None, :])

    return (jnp.cos(ang_r), jnp.sin(ang_r)) + diff_tables(gq, DIFF_DH ** -0.5 * LOG2E) + diff_tables(gk, 1.0)


def _tile_rows(n, target):
    tm = min(n, target)
    assert n % tm == 0
    return tm


def _layer(x, mod, pos, s0, kc, vc, lw, li, ret_chunk):
    nb, seq, d = x.shape
    t = nb * seq
    x2d = x.reshape(t, d)
    per_token = seq < 128
    if per_token:
        mods = [jnp.repeat(mod[:, k, :], seq, axis=0) for k in range(6)]
        tabs = tuple(jnp.tile(tb, (nb, 1)) for tb in _rope_tables(pos, lw['gq'], lw['gk']))
        rows_per_group = t
    else:
        mods = [mod[:, k:k + 1, :] for k in range(6)]
        tabs = _rope_tables(pos, lw['gq'], lw['gk'])
        rows_per_group = seq
    sh1, sc1, gt1, sh2, sc2, gt2 = mods
    tm = _tile_rows(rows_per_group, 512)

    p, qd, k2d, v2d = _inproj(x2d, sc1, sh1, lw['norm1_g'], lw['w_in'], tabs, lw['bd'],
                              _tile_rows(rows_per_group, 1024), rows_per_group)
    ret_o, s_new = _retention(p, s0, nb, seq, ret_chunk)
    lam_init = 0.8 - 0.6 * math.exp(-0.3 * li)
    if kc is None:
        dif_o = _diffattn(qd, k2d, v2d, lw['lam'], lw['subln_g'], nb, seq, lam_init)
    else:
        dif_o = _diffattn_cached(qd, k2d, v2d, kc, vc, li, lw['lam'], lw['subln_g'], nb, seq, lam_init)
    merge_args = (ret_o, dif_o, p, x2d, gt1, sc2, sh2, lw['norm2_g'], lw['w_br_ret'], lw['w_br_diff'],
                  lw['w_out'], lw['rw_hi'], lw['rw_lo'], lw['rb'], tm, rows_per_group)
    if per_token:
        x1, h2, cmb = _merge(*merge_args, routed=False)
        y = _moe(h2, cmb, x1, gt2, lw['w1'], lw['w3'], lw['w2'], tm, rows_per_group)
    else:
        x1, hx, cls, hist = _merge(*merge_args, routed=True)
        n_tiles = SUBLANES * pl.cdiv(t // MOE_TILE_ROWS + N_CLASSES, SUBLANES)
        pos, tinfo = _route(cls, hist, MOE_TILE_ROWS, n_tiles, tm)
        hx_sorted = _dispatch(pos, hx, n_tiles * MOE_TILE_ROWS, tm)
        y_sorted = _moe_routed(hx_sorted, tinfo[:, 0], tinfo[:, 1], tinfo[:, 2],
                               lw['w1'], lw['w3'], lw['w2'], MOE_TILE_ROWS)
        y = _combine(pos, y_sorted, x1, gt2, tm, rows_per_group)
    return (y.reshape(nb, seq, d), s_new, k2d.reshape(nb, seq, DIFF_HEADS, 2 * DIFF_DH),
            v2d.reshape(nb, seq, DIFF_HEADS, DIFF_DV))


def kernel(x_prompt, x_sample, c_prompt, c_sample, cache_diff_k, cache_diff_v, state_ret, w_ada, b_ada, norm1_g, w_in, diff_qnorm_g, diff_knorm_g, diff_lambda_q1, diff_lambda_k1, diff_lambda_q2, diff_lambda_k2, diff_subln_g, w_br_ret, w_br_diff, w_out, norm2_g, w_group, b_group, w_expert_router, b_expert, w1, w3, w2):
    nb, seq, d = x_prompt.shape
    nbs, seqs, _ = x_sample.shape
    depth = w_in.shape[0]
    past = cache_diff_k.shape[2]
    pos_p = jnp.arange(seq, dtype=jnp.int32)
    pos_s = past + jnp.arange(seqs, dtype=jnp.int32)
    c_all = jnp.concatenate([c_prompt, c_sample], axis=0)
    col = jnp.arange(MXU_COLS)
    bd = jnp.where((col[:, None] // DIFF_DH) == (col[None, :] // DIFF_DH), 1.0 / DIFF_DH, 0.0).astype(BF16)

    xp, xs = x_prompt, x_sample
    outs = [[] for _ in range(6)]
    for li in range(depth):
        rw = jnp.zeros((d, LANES), F32)
        rw = rw.at[:, :N_GROUPS].set(w_group[li]).at[:, N_GROUPS:N_GROUPS + N_EXPERTS].set(w_expert_router[li])
        rw_hi = rw.astype(BF16)
        rb = jnp.zeros((1, LANES), F32)
        rb = rb.at[0, :N_GROUPS].set(b_group[li]).at[0, N_GROUPS:N_GROUPS + N_EXPERTS].set(b_expert[li])
        lw = {
            'norm1_g': norm1_g[li].reshape(1, d), 'norm2_g': norm2_g[li].reshape(1, d),
            'w_in': jnp.stack([w_in[li, :, blk * COL_BLOCK:(blk + 1) * COL_BLOCK] for blk in IN_BLOCK_ORDER]
                              ).astype(BF16),
            'gq': diff_qnorm_g[li], 'gk': diff_knorm_g[li],
            'bd': bd,
            'lam': jnp.stack([diff_lambda_q1[li], diff_lambda_k1[li], diff_lambda_q2[li], diff_lambda_k2[li]]),
            'subln_g': diff_subln_g[li].reshape(1, DIFF_DV),
            'w_br_ret': w_br_ret[li].astype(BF16), 'w_br_diff': w_br_diff[li].astype(BF16),
            'w_out': w_out[li].astype(BF16),
            'rw_hi': rw_hi, 'rw_lo': (rw - rw_hi.astype(F32)).astype(BF16), 'rb': rb,
            'w1': w1[li].astype(BF16), 'w3': w3[li].astype(BF16), 'w2': w2[li].astype(BF16),
        }
        mod = _adaln(c_all, w_ada[li], b_ada[li]).reshape(nb + nbs, 6, d)
        xp, sp, kp, vp = _layer(xp, mod[:nb], pos_p, None, None, None, lw, li, ret_chunk=min(seq, 256))
        xs, ss, kss, vss = _layer(xs, mod[nb:], pos_s, state_ret[li], cache_diff_k, cache_diff_v,
                                  lw, li, ret_chunk=seqs)
        for lst, val in zip(outs, (kp, vp, sp, kss, vss, ss)):
            lst.append(val)
    kp, vp, sp, kss, vss, ss = (jnp.stack(lst) for lst in outs)
    return (xp, xs, kp, vp, sp.astype(x_prompt.dtype), kss, vss, ss.astype(state_ret.dtype))
```

```python
import functools
import math

import jax
import jax.numpy as jnp
from jax import lax
from jax.experimental import pallas as pl
from jax.experimental.pallas import tpu as pltpu

F32 = jnp.float32
BF16 = jnp.bfloat16

D_MODEL = 1024
CHUNK = 64
ROPE_THETA = 10000.0
EPS = 1e-6
RET_HEADS = 4
RET_DK = D_MODEL // RET_HEADS
RET_DV = 2 * RET_DK
DIFF_HEADS = 8
DIFF_DH = D_MODEL // (2 * DIFF_HEADS)
DIFF_DV = 2 * DIFF_DH
N_GROUPS = 4
EXPERTS_PER_GROUP = 4
N_EXPERTS = N_GROUPS * EXPERTS_PER_GROUP
D_EXPERT = 512
N_PAIRS = EXPERTS_PER_GROUP * (EXPERTS_PER_GROUP - 1) // 2
N_CLASSES = N_GROUPS * N_PAIRS
MOE_TILE_ROWS = 512
ATTN_ROW_PART = 256
MERGE_ROW_PARTS = 1
DMA_UNROLL = 8

LANES = 128
SUBLANES = 8
MXU_COLS = 256
COL_BLOCK = 1024
IN_BLOCK_ORDER = (0, 1, 2, 3, 4, 5, 6, 9, 10, 7, 8)
N_SLAB_BLOCKS = 8
STRIPS = 4
SLAB_RQ, SLAB_RK, SLAB_RV, SLAB_RG, SLAB_GATE = 0, 1, 2, 4, 6
VMEM_LIMIT = 56 * 1024 * 1024
LOG2E = math.log2(math.e)


def _cparams(*sem):
    return pltpu.CompilerParams(dimension_semantics=sem, vmem_limit_bytes=VMEM_LIMIT)


def _dot(a, b):
    return jnp.dot(a, b, preferred_element_type=F32)


def _dot_nt(a, b):
    return lax.dot_general(a, b, (((1,), (1,)), ((), ())), preferred_element_type=F32)


def _adaln_kernel(c_ref, w_ref, b_ref, o_ref):
    c = c_ref[...]
    a = c * jax.nn.sigmoid(c)
    o_ref[...] = jnp.dot(a, w_ref[...], preferred_element_type=F32,
                         precision=lax.Precision.HIGHEST) + b_ref[...]


def _adaln(c, w_ada, b_ada):
    rows, d = c.shape
    n = w_ada.shape[1]
    tn = 1536
    return pl.pallas_call(
        _adaln_kernel,
        out_shape=jax.ShapeDtypeStruct((rows, n), F32),
        grid=(n // tn,),
        in_specs=[pl.BlockSpec((rows, d), lambda j: (0, 0)),
                  pl.BlockSpec((d, tn), lambda j: (0, j)),
                  pl.BlockSpec((1, tn), lambda j: (0, j))],
        out_specs=pl.BlockSpec((rows, tn), lambda j: (0, j)),
        compiler_params=_cparams("arbitrary"),
        name="adaln",
    )(c, w_ada, b_ada.reshape(1, n))


def _rms(x):
    return x * lax.rsqrt(jnp.mean(x * x, axis=-1, keepdims=True) + EPS)


def _inproj_kernel(x_ref, sc_ref, sh_ref, g1_ref, w_ref, rc_ref, rs_ref, qc_ref, qsn_ref, qsp_ref,
                   kc_ref, ksn_ref, ksp_ref, bd_ref, p_ref, qd_ref, k_ref, v_ref, h_scr):
    j = pl.program_id(1)

    @pl.when(j == 0)
    def _():
        h = _rms(x_ref[...]) * g1_ref[...]
        h = h * (1.0 + sc_ref[...]) + sh_ref[...]
        h_scr[...] = h.astype(BF16)

    def strips():
        h = h_scr[...]
        starts = list(range(0, COL_BLOCK, MXU_COLS))
        ahead = _dot(h, w_ref[:, 0:MXU_COLS])
        for n, lo in enumerate(starts):
            acc = ahead
            if n + 1 < len(starts):
                ahead = _dot(h, w_ref[:, starts[n + 1]:starts[n + 1] + MXU_COLS])
            yield lo, acc

    def ret_rope(scale):
        c = rc_ref[...]
        s = rs_ref[...]
        for lo, acc in strips():
            x1 = acc[:, 0:LANES]
            x2 = acc[:, LANES:2 * LANES]
            p_ref[lo // MXU_COLS, :, 0:LANES] = ((x1 * c - x2 * s) * scale).astype(BF16)
            p_ref[lo // MXU_COLS, :, LANES:2 * LANES] = ((x1 * s + x2 * c) * scale).astype(BF16)

    def diff_norm():
        for lo, acc in list(strips()):
            k_ref[:, lo:lo + MXU_COLS] = acc * lax.rsqrt(_dot((acc * acc).astype(BF16), bd_ref[...]) + EPS)

    def diff_rope(lo, c_ref, sn_ref, sp_ref, out_ref):
        for off in range(lo, lo + MXU_COLS, LANES):
            yg = k_ref[:, off:off + LANES]
            o = (yg * c_ref[...] + pltpu.roll(yg, LANES - DIFF_DH // 2, 1) * sn_ref[...]
                 + pltpu.roll(yg, DIFF_DH // 2, 1) * sp_ref[...])
            out_ref[:, off:off + LANES] = o.astype(out_ref.dtype)

    def elementwise(fn, out_ref, after_strip=None):
        for lo, acc in strips():
            if out_ref is p_ref:
                p_ref[lo // MXU_COLS] = fn(acc).astype(BF16)
            else:
                out_ref[:, lo:lo + MXU_COLS] = fn(acc).astype(out_ref.dtype)
            if after_strip is not None:
                after_strip(lo)

    @pl.when(j == 0)
    def _():
        ret_rope(1.0)

    @pl.when(j == 1)
    def _():
        ret_rope(RET_DK ** -0.5)

    @pl.when((j == 2) | (j == 3))
    def _():
        elementwise(lambda a: a, p_ref)

    @pl.when((j == 4) | (j == 5))
    def _():
        elementwise(lambda a: a * jax.nn.sigmoid(a), p_ref)

    @pl.when((j == 6) | (j == 9))
    def _():
        diff_norm()

    @pl.when(j == 7)
    def _():
        elementwise(jax.nn.sigmoid, p_ref, lambda lo: diff_rope(lo, qc_ref, qsn_ref, qsp_ref, qd_ref))

    @pl.when(j == 8)
    def _():
        elementwise(jax.nn.sigmoid, p_ref)

    @pl.when(j == 10)
    def _():
        elementwise(lambda a: a, v_ref, lambda lo: diff_rope(lo, kc_ref, ksn_ref, ksp_ref, k_ref))


def _mod_spec(arr, tm, rows_per_group, ngrid):
    d = arr.shape[-1]
    if arr.ndim == 3:
        tiles_per_group = rows_per_group // tm
        if ngrid == 2:
            return pl.BlockSpec((None, 1, d), lambda i, j: (i // tiles_per_group, 0, 0))
        return pl.BlockSpec((None, 1, d), lambda i: (i // tiles_per_group, 0, 0))
    if ngrid == 2:
        return pl.BlockSpec((tm, d), lambda i, j: (i, 0))
    return pl.BlockSpec((tm, d), lambda i: (i, 0))


def _inproj(x2d, sc1, sh1, g1, w_blocks, tabs, bd, tm, rows_per_group):
    t, d = x2d.shape
    nrep = tabs[0].shape[0] // tm

    def slab_map(i, j):
        return (jnp.where(j < 6, j, jnp.where(j < 9, jnp.maximum(j - 1, 5), N_SLAB_BLOCKS - 1)), i, 0)

    tab_spec = pl.BlockSpec((tm, LANES), lambda i, j: (i % nrep, 0))
    row_spec = pl.BlockSpec((tm, d), lambda i, j: (i, 0))
    return pl.pallas_call(
        _inproj_kernel,
        out_shape=(jax.ShapeDtypeStruct((N_SLAB_BLOCKS * STRIPS, t, MXU_COLS), BF16),
                   jax.ShapeDtypeStruct((t, d), BF16),
                   jax.ShapeDtypeStruct((t, d), F32),
                   jax.ShapeDtypeStruct((t, d), F32)),
        grid=(t // tm, len(IN_BLOCK_ORDER)),
        in_specs=[row_spec,
                  _mod_spec(sc1, tm, rows_per_group, 2),
                  _mod_spec(sh1, tm, rows_per_group, 2),
                  pl.BlockSpec((1, d), lambda i, j: (0, 0)),
                  pl.BlockSpec((None, d, COL_BLOCK), lambda i, j: (j, 0, 0)),
                  *([tab_spec] * len(tabs)),
                  pl.BlockSpec((MXU_COLS, MXU_COLS), lambda i, j: (0, 0))],
        out_specs=(pl.BlockSpec((STRIPS, tm, MXU_COLS), slab_map), row_spec, row_spec, row_spec),
        scratch_shapes=[pltpu.VMEM((tm, d), BF16)],
        compiler_params=_cparams("arbitrary", "arbitrary"),
        name="inproj",
    )(x2d, sc1, sh1, g1, w_blocks, *tabs, bd)


def _retention_kernel(*refs, has_state, chunk):
    if has_state:
        q_ref, k_ref, v_ref, rg_ref, dm_ref, dv_ref, s0_ref, o_ref, sout_ref = refs
        s = s0_ref[...]
    else:
        q_ref, k_ref, v_ref, rg_ref, dm_ref, dv_ref, o_ref, sout_ref = refs
        s = jnp.zeros(sout_ref.shape, F32)
    dvec = dv_ref[...]
    q_decay = dvec[:, 0:1]
    k_decay = dvec[:, 1:2]
    chunk_decay = dvec[0:1, 2:3]

    def wide(ref, rows):
        return jnp.concatenate([ref[0, rows, :], ref[1, rows, :]], axis=1)

    for c in range(q_ref.shape[0] // chunk):
        rows = slice(c * chunk, (c + 1) * chunk)
        q = q_ref[rows, :]
        k = k_ref[rows, :]
        v = wide(v_ref, rows)
        scores = (_dot_nt(q, k) * dm_ref[...]).astype(BF16)
        o = _dot(scores, v) + _dot(q, s.astype(BF16)) * q_decay
        o = _rms(o) * wide(rg_ref, rows).astype(F32)
        o_ref[rows, :] = o.astype(BF16)
        kd_t = (k.astype(F32) * k_decay).T.astype(BF16)
        s = s * chunk_decay + _dot(kd_t, v)
    sout_ref[...] = s


def _retention(p, s0, nb, seq, chunk):
    t = p.shape[1]
    idx = jnp.arange(chunk, dtype=F32)
    log_gamma = jnp.log1p(-jnp.exp2(-5.0 - jnp.arange(RET_HEADS, dtype=F32)))
    rel = idx[:, None] - idx[None, :]
    dmat = jnp.where(rel >= 0, jnp.exp(jnp.maximum(rel, 0.0)[None] * log_gamma[:, None, None]), 0.0)
    q_decay = jnp.exp((idx + 1.0)[None, :] * log_gamma[:, None])
    k_decay = jnp.exp((chunk - 1.0 - idx)[None, :] * log_gamma[:, None])
    c_decay = jnp.broadcast_to(jnp.exp(chunk * log_gamma)[:, None], (RET_HEADS, chunk))
    dvec = jnp.zeros((RET_HEADS, chunk, LANES), F32)
    dvec = dvec.at[:, :, 0].set(q_decay).at[:, :, 1].set(k_decay).at[:, :, 2].set(c_decay)

    vstrips = RET_DV // MXU_COLS
    in_specs = [pl.BlockSpec((None, seq, RET_DK), lambda b, h: (SLAB_RQ * STRIPS + h, b, 0)),
                pl.BlockSpec((None, seq, RET_DK), lambda b, h: (SLAB_RK * STRIPS + h, b, 0)),
                pl.BlockSpec((vstrips, seq, MXU_COLS), lambda b, h: (SLAB_RV * STRIPS // vstrips + h, b, 0)),
                pl.BlockSpec((vstrips, seq, MXU_COLS), lambda b, h: (SLAB_RG * STRIPS // vstrips + h, b, 0)),
                pl.BlockSpec((None, chunk, chunk), lambda b, h: (h, 0, 0)),
                pl.BlockSpec((None, chunk, LANES), lambda b, h: (h, 0, 0))]
    args = [p, p, p, p, dmat, dvec]
    state_spec = pl.BlockSpec((None, None, RET_DK, RET_DV), lambda b, h: (b, h, 0, 0))
    if s0 is not None:
        in_specs.append(state_spec)
        args.append(s0)
    return pl.pallas_call(
        functools.partial(_retention_kernel, has_state=s0 is not None, chunk=chunk),
        out_shape=(jax.ShapeDtypeStruct((RET_HEADS, t, RET_DV), BF16),
                   jax.ShapeDtypeStruct((nb, RET_HEADS, RET_DK, RET_DV), F32)),
        grid=(nb, RET_HEADS),
        in_specs=in_specs,
        out_specs=(pl.BlockSpec((None, seq, RET_DV), lambda b, h: (h, b, 0)), state_spec),
        compiler_params=_cparams("arbitrary", "arbitrary"),
        name="retention",
    )(*args)


def _lambda(lam_ref, lam_init):
    l1 = jnp.sum(lam_ref[0:1, :] * lam_ref[1:2, :], axis=-1, keepdims=True)
    l2 = jnp.sum(lam_ref[2:3, :] * lam_ref[3:4, :], axis=-1, keepdims=True)
    return jnp.exp(l1) - jnp.exp(l2) + lam_init


def _split_maps(q):
    lane = lax.broadcasted_iota(jnp.int32, q.shape, 1)
    zero = jnp.zeros_like(q)
    return jnp.where(lane < DIFF_DH, q, zero), jnp.where(lane >= DIFF_DH, q, zero)


def _diffattn_kernel(q_ref, k_ref, v_ref, lam_ref, g_ref, o_ref, kb_scr, vb_scr, *, part, lam_init):
    kb_scr[...] = k_ref[...].astype(BF16)
    vb_scr[:, 0:DIFF_DV] = v_ref[...].astype(BF16)
    vb_scr[:, DIFF_DV:2 * DIFF_DV] = jnp.ones((k_ref.shape[0], DIFF_DV), BF16)
    lam = _lambda(lam_ref, lam_init)
    row = lax.broadcasted_iota(jnp.int32, (part, part), 0)
    col = lax.broadcasted_iota(jnp.int32, (part, part), 1)
    visible = (col // CHUNK) <= (row // CHUNK)

    def scores(q0):
        qmaps = _split_maps(q_ref[q0:q0 + part, :])
        kblk = kb_scr[0:q0 + part, :]
        out = []
        for mp in range(2):
            sc = _dot_nt(qmaps[mp], kblk)
            diag = jnp.where(visible, sc[:, q0:q0 + part], -1e30)
            out.append(jnp.concatenate([sc[:, 0:q0], diag], axis=1) if q0 else diag)
        return out

    starts = list(range(0, k_ref.shape[0], part))
    pending = scores(starts[0])
    for n, q0 in enumerate(starts):
        scs = pending
        if n + 1 < len(starts):
            pending = scores(starts[n + 1])
        vblk = vb_scr[0:q0 + part, :]
        att = []
        for mp in range(2):
            sc = scs[mp]
            pv = _dot(jnp.exp2(sc - jnp.max(sc, axis=-1, keepdims=True)).astype(BF16), vblk)
            att.append(pv[:, 0:DIFF_DV] / pv[:, DIFF_DV:2 * DIFF_DV])
        o = att[0] - lam * att[1]
        o = _rms(o) * g_ref[...] * (1.0 - lam_init)
        o_ref[q0:q0 + part, :] = o.astype(BF16)


def _diffattn(qd, k2d, v2d, lam_vecs, subln_g, nb, seq, lam_init):
    t = qd.shape[0]
    part = min(seq, ATTN_ROW_PART)
    assert seq % part == 0 and part % CHUNK == 0
    return pl.pallas_call(
        functools.partial(_diffattn_kernel, part=part, lam_init=lam_init),
        out_shape=jax.ShapeDtypeStruct((t, DIFF_HEADS * DIFF_DV), BF16),
        grid=(nb, DIFF_HEADS),
        in_specs=[pl.BlockSpec((seq, DIFF_DV), lambda b, h: (b, h)),
                  pl.BlockSpec((seq, DIFF_DV), lambda b, h: (b, h)),
                  pl.BlockSpec((seq, DIFF_DV), lambda b, h: (b, h)),
                  pl.BlockSpec((4, DIFF_DH), lambda b, h: (0, 0)),
                  pl.BlockSpec((1, DIFF_DV), lambda b, h: (0, 0))],
        out_specs=pl.BlockSpec((seq, DIFF_DV), lambda b, h: (b, h)),
        scratch_shapes=[pltpu.VMEM((seq, DIFF_DV), BF16), pltpu.VMEM((seq, 2 * DIFF_DV), BF16)],
        compiler_params=_cparams("arbitrary", "arbitrary"),
        name="diffattn",
    )(qd, k2d, v2d, lam_vecs, subln_g)


def _diffattn_cached_kernel(q_ref, kc_ref, vc_ref, kn_ref, vn_ref, lam_ref, g_ref, o_ref, k_scr, v_scr,
                            *, past, new, lam_init):
    total = k_scr.shape[0]
    pad = jnp.zeros((total - past, DIFF_DV), BF16)
    qpos = past + lax.broadcasted_iota(jnp.int32, (new, total), 0)
    kpos = lax.broadcasted_iota(jnp.int32, (new, total), 1)
    visible = ((kpos // CHUNK) <= (qpos // CHUNK)) & (kpos < past + new)
    lam = _lambda(lam_ref, lam_init)

    for h in range(DIFF_HEADS):
        cols = slice(h * DIFF_DV, (h + 1) * DIFF_DV)
        k_scr[0:past, :] = kc_ref[:, h, :].astype(BF16)
        v_scr[0:past, :] = vc_ref[:, h, :].astype(BF16)
        k_scr[past:total, :] = pad
        v_scr[past:total, :] = pad
        k_scr[past:past + new, :] = kn_ref[:, cols].astype(BF16)
        v_scr[past:past + new, :] = vn_ref[:, cols].astype(BF16)
        kall = k_scr[...]
        vall = v_scr[...]

        def softmax_av(qm):
            sc = jnp.where(visible, _dot_nt(qm, kall), -1e30)
            m = jnp.max(sc, axis=-1, keepdims=True)
            p = jnp.exp2(sc - m)
            return _dot(p.astype(BF16), vall) / jnp.sum(p, axis=-1, keepdims=True)

        qa, qb = _split_maps(q_ref[:, cols])
        o = softmax_av(qa) - lam * softmax_av(qb)
        o = _rms(o) * g_ref[...] * (1.0 - lam_init)
        o_ref[:, cols] = o.astype(BF16)


def _diffattn_cached(qd, k2d, v2d, kc, vc, li, lam_vecs, subln_g, nb, new, lam_init):
    t = qd.shape[0]
    past = kc.shape[2]
    width = DIFF_HEADS * DIFF_DV
    total = past + LANES * pl.cdiv(new, LANES)
    cache_spec = pl.BlockSpec((None, None, past, DIFF_HEADS, DIFF_DV), lambda b: (li, b, 0, 0, 0))
    return pl.pallas_call(
        functools.partial(_diffattn_cached_kernel, past=past, new=new, lam_init=lam_init),
        out_shape=jax.ShapeDtypeStruct((t, width), BF16),
        grid=(nb,),
        in_specs=[pl.BlockSpec((new, width), lambda b: (b, 0)),
                  cache_spec, cache_spec,
                  pl.BlockSpec((new, width), lambda b: (b, 0)),
                  pl.BlockSpec((new, width), lambda b: (b, 0)),
                  pl.BlockSpec((4, DIFF_DH), lambda b: (0, 0)),
                  pl.BlockSpec((1, DIFF_DV), lambda b: (0, 0))],
        out_specs=pl.BlockSpec((new, width), lambda b: (b, 0)),
        scratch_shapes=[pltpu.VMEM((total, DIFF_DV), BF16), pltpu.VMEM((total, DIFF_DV), BF16)],
        compiler_params=_cparams("arbitrary"),
        name="diffattn_cached",
    )(qd, kc, vc, k2d, v2d, lam_vecs, subln_g)


def _merge_kernel(ret_ref, dif_ref, gr_ref, gd_ref, x_ref, gt1_ref, sc2_ref, sh2_ref, g2_ref,
                  wr_ref, wd_ref, wo_ref, rwh_ref, rwl_ref, rb_ref, x1_ref, *out_refs, routed):
    tm, d = x_ref.shape
    n_parts = MERGE_ROW_PARTS if tm % (MERGE_ROW_PARTS * LANES) == 0 else 1
    part = tm // n_parts

    def mod_rows(ref, rows):
        return ref[rows, :] if ref.shape[0] > 1 else ref[...]

    hist = jnp.zeros((1, LANES), F32)
    for r in range(n_parts):
        rows = slice(r * part, (r + 1) * part)
        def lanes(ref):
            return jnp.concatenate([ref[n, rows, :] for n in range(ref.shape[0])], axis=1)

        a = _dot(lanes(ret_ref), wr_ref[...])
        b = _dot(dif_ref[rows, :], wd_ref[...])
        merged = lanes(gr_ref).astype(F32) * a + lanes(gd_ref).astype(F32) * b
        x1 = x_ref[rows, :] + mod_rows(gt1_ref, rows) * _dot(merged.astype(BF16), wo_ref[...])
        x1_ref[rows, :] = x1
        h2 = _rms(x1) * g2_ref[...]
        h2 = h2 * (1.0 + mod_rows(sc2_ref, rows)) + mod_rows(sh2_ref, rows)

        hi = h2.astype(BF16)
        lo = (h2 - hi.astype(F32)).astype(BF16)
        logits = _dot(hi, rwh_ref[...]) + _dot(lo, rwh_ref[...]) + _dot(hi, rwl_ref[...]) + rb_ref[...]

        lane = lax.broadcasted_iota(jnp.int32, logits.shape, 1)
        big = jnp.int32(4 * LANES)

        def masked_argmax(mask):
            vmax = jnp.max(jnp.where(mask, logits, -jnp.inf), axis=-1, keepdims=True)
            idx = jnp.min(jnp.where(mask & (logits == vmax), lane, big), axis=-1, keepdims=True)
            return vmax, idx

        is_group = lane < N_GROUPS
        gmax, gidx = masked_argmax(is_group)
        gprob = 1.0 / jnp.sum(jnp.where(is_group, jnp.exp(logits - gmax), 0.0), axis=-1, keepdims=True)
        first = N_GROUPS + gidx * EXPERTS_PER_GROUP
        in_group = (lane >= first) & (lane < first + EXPERTS_PER_GROUP)
        v1, i1 = masked_argmax(in_group)
        v2, i2 = masked_argmax(in_group & (lane != i1))
        e2 = jnp.exp(v2 - v1)
        w1 = gprob / (1.0 + e2)
        w2 = gprob * e2 / (1.0 + e2)
        if not routed:
            h2_ref, cmb_ref = out_refs
            h2_ref[rows, :] = h2.astype(BF16)
            cmb_ref[rows, :] = (jnp.where(lane + N_GROUPS == i1, w1, 0.0)
                                + jnp.where(lane + N_GROUPS == i2, w2, 0.0))
            continue

        hx_ref, cls_ref, hist_ref = out_refs
        first_is_lo = i1 < i2
        lo_e = jnp.minimum(i1, i2) - first
        hi_e = jnp.maximum(i1, i2) - first
        cls = gidx * N_PAIRS + jnp.right_shift(lo_e * (7 - lo_e), 1) + hi_e - lo_e - 1
        hx_ref[rows, 0:d] = h2
        hx_ref[rows, d:d + LANES] = jnp.where(lane == 0, jnp.where(first_is_lo, w1, w2),
                                              jnp.where(lane == 1, jnp.where(first_is_lo, w2, w1), 0.0))
        cls_ref[rows, :] = cls
        hist = hist + jnp.sum(jnp.where(lane == cls, 1.0, 0.0), axis=0, keepdims=True)

    if routed:
        hist_ref = out_refs[2]

        @pl.when(pl.program_id(0) == 0)
        def _():
            hist_ref[...] = jnp.zeros_like(hist_ref)

        hist_ref[...] += hist


def _merge(ret_o, dif_o, p, x2d, gt1, sc2, sh2, g2, wr, wd, wo, rwh, rwl, rb, tm, rows_per_group, routed):
    t, d = x2d.shape
    const = lambda shape: pl.BlockSpec(shape, lambda i: (0, 0), pipeline_mode=pl.Buffered(1))
    if routed:
        out_shape = (jax.ShapeDtypeStruct((t, d), F32),
                     jax.ShapeDtypeStruct((t, d + LANES), F32),
                     jax.ShapeDtypeStruct((t, 1), jnp.int32),
                     jax.ShapeDtypeStruct((1, LANES), F32))
        out_specs = (pl.BlockSpec((tm, d), lambda i: (i, 0)),
                     pl.BlockSpec((tm, d + LANES), lambda i: (i, 0)),
                     pl.BlockSpec((tm, 1), lambda i: (i, 0)),
                     pl.BlockSpec((1, LANES), lambda i: (0, 0)))
    else:
        out_shape = (jax.ShapeDtypeStruct((t, d), F32),
                     jax.ShapeDtypeStruct((t, d), BF16),
                     jax.ShapeDtypeStruct((t, LANES), F32))
        out_specs = (pl.BlockSpec((tm, d), lambda i: (i, 0)),
                     pl.BlockSpec((tm, d), lambda i: (i, 0)),
                     pl.BlockSpec((tm, LANES), lambda i: (i, 0)))
    return pl.pallas_call(
        functools.partial(_merge_kernel, routed=routed),
        out_shape=out_shape,
        grid=(t // tm,),
        in_specs=[pl.BlockSpec((RET_HEADS, tm, RET_DV), lambda i: (0, i, 0)),
                  pl.BlockSpec((tm, d), lambda i: (i, 0)),
                  pl.BlockSpec((STRIPS, tm, MXU_COLS), lambda i: (SLAB_GATE, i, 0)),
                  pl.BlockSpec((STRIPS, tm, MXU_COLS), lambda i: (SLAB_GATE + 1, i, 0)),
                  pl.BlockSpec((tm, d), lambda i: (i, 0)),
                  _mod_spec(gt1, tm, rows_per_group, 1),
                  _mod_spec(sc2, tm, rows_per_group, 1),
                  _mod_spec(sh2, tm, rows_per_group, 1),
                  const((1, d)),
                  const(wr.shape), const(wd.shape), const(wo.shape),
                  const(rwh.shape), const(rwl.shape), const(rb.shape)],
        out_specs=out_specs,
        compiler_params=_cparams("arbitrary"),
        name="merge",
    )(ret_o, dif_o, p, p, x2d, gt1, sc2, sh2, g2, wr, wd, wo, rwh, rwl, rb)


def _route_kernel(cls_ref, hist_ref, tri_ref, pos_ref, tinfo_ref, carry_scr, off_scr, *, tile_rows):
    i = pl.program_id(0)

    def excl_cumsum(v):
        lane8 = lax.broadcasted_iota(jnp.int32, v.shape, 1)
        inc = v
        for sh in (1, 2, 4, 8, 16, 32, 64):
            inc = inc + jnp.where(lane8 >= sh, pltpu.roll(inc, sh, 1), 0.0)
        return inc - v

    @pl.when(i == 0)
    def _():
        cnt = hist_ref[...]
        ntile = jnp.floor((cnt + (tile_rows - 1)) * (1.0 / tile_rows))
        start = excl_cumsum(jnp.broadcast_to(ntile, (SUBLANES, LANES)))[0:1, :]
        off_scr[...] = start * tile_rows
        carry_scr[...] = jnp.zeros_like(carry_scr)
        nt = tinfo_ref.shape[0]
        jrow = lax.broadcasted_iota(jnp.int32, (nt, LANES), 0).astype(F32)
        lane = lax.broadcasted_iota(jnp.int32, (nt, LANES), 1)
        is_cls = lane < N_CLASSES
        tcls = jnp.sum(jnp.where(is_cls & (start <= jrow), 1.0, 0.0), axis=-1, keepdims=True) - 1.0
        pick = lane == tcls.astype(jnp.int32)
        cnt_j = jnp.sum(jnp.where(pick, cnt, 0.0), axis=-1, keepdims=True)
        start_j = jnp.sum(jnp.where(pick, start, 0.0), axis=-1, keepdims=True)
        valid = jnp.clip(cnt_j - (jrow[:, 0:1] - start_j) * tile_rows, 0.0, float(tile_rows)).astype(jnp.int32)
        tc = tcls.astype(jnp.int32)
        grp = sum((tc >= g * N_PAIRS).astype(jnp.int32) for g in range(1, N_GROUPS))
        pair = tc - grp * N_PAIRS
        lo = (pair >= 3).astype(jnp.int32) + (pair >= 5).astype(jnp.int32)
        hi = pair - jnp.right_shift(lo * (7 - lo), 1) + lo + 1
        ea = grp * EXPERTS_PER_GROUP + lo
        eb = grp * EXPERTS_PER_GROUP + hi
        tinfo_ref[...] = jnp.where(lane == 0, ea, jnp.where(lane == 1, eb, jnp.where(lane == 2, valid, 0)))

    cls = cls_ref[...]
    lane = lax.broadcasted_iota(jnp.int32, (cls.shape[0], LANES), 1)
    onehot = lane == cls
    earlier = _dot(tri_ref[...], jnp.where(onehot, 1.0, 0.0).astype(BF16))
    base = carry_scr[...] + off_scr[...]
    pos = jnp.sum(jnp.where(onehot, earlier + base, 0.0), axis=-1, keepdims=True)
    pos_ref[...] = pos.astype(jnp.int32)
    carry_scr[...] += jnp.sum(jnp.where(onehot, 1.0, 0.0), axis=0, keepdims=True)


def _route(cls, hist, tile_rows, n_tiles, tm):
    t = cls.shape[0]
    tri = jnp.tril(jnp.ones((tm, tm), BF16), -1)
    return pl.pallas_call(
        functools.partial(_route_kernel, tile_rows=tile_rows),
        out_shape=(jax.ShapeDtypeStruct((t, 1), jnp.int32),
                   jax.ShapeDtypeStruct((n_tiles, LANES), jnp.int32)),
        grid=(t // tm,),
        in_specs=[pl.BlockSpec((tm, 1), lambda i: (i, 0)),
                  pl.BlockSpec((1, LANES), lambda i: (0, 0)),
                  pl.BlockSpec((tm, tm), lambda i: (0, 0))],
        out_specs=(pl.BlockSpec((tm, 1), lambda i: (i, 0)),
                   pl.BlockSpec((n_tiles, LANES), lambda i: (0, 0))),
        scratch_shapes=[pltpu.VMEM((1, LANES), F32), pltpu.VMEM((1, LANES), F32)],
        compiler_params=_cparams("arbitrary"),
        name="route",
    )(cls, hist, tri)


def _start_and_wait_rows(row_copy, n_rows):
    def start(g, carry):
        for u in range(DMA_UNROLL):
            row_copy(g * DMA_UNROLL + u).start(priority=u % 2)
        return carry

    def wait(r, carry):
        row_copy(r).wait()
        return carry

    lax.fori_loop(0, n_rows // DMA_UNROLL, start, 0)
    lax.fori_loop(0, n_rows, wait, 0, unroll=DMA_UNROLL)


def _dispatch_kernel(pos_ref, hx_ref, sorted_ref, sem):
    tm = hx_ref.shape[0]

    def row_copy(r):
        return pltpu.make_async_copy(hx_ref.at[pl.ds(r, 1)], sorted_ref.at[pl.ds(pos_ref[0, r], 1)], sem)

    _start_and_wait_rows(row_copy, tm)


def _dispatch(pos, hx, n_slots, tm):
    t, w = hx.shape
    return pl.pallas_call(
        _dispatch_kernel,
        out_shape=jax.ShapeDtypeStruct((n_slots, w), F32),
        grid=(t // tm,),
        in_specs=[pl.BlockSpec((None, 1, tm), lambda i: (i, 0, 0), memory_space=pltpu.SMEM),
                  pl.BlockSpec((tm, w), lambda i: (i, 0))],
        out_specs=pl.BlockSpec(memory_space=pl.ANY),
        scratch_shapes=[pltpu.SemaphoreType.DMA],
        compiler_params=pltpu.CompilerParams(dimension_semantics=("arbitrary",), vmem_limit_bytes=VMEM_LIMIT,
                                             has_side_effects=True),
        name="dispatch",
    )(pos.reshape(t // tm, 1, tm), hx)


def _moe_routed_kernel(ea_ref, eb_ref, valid_ref, hx_ref, w1a_ref, w3a_ref, w2a_ref, w1b_ref, w3b_ref, w2b_ref, y_ref):
    j = pl.program_id(0)
    nvalid = valid_ref[j]

    @pl.when(nvalid == 0)
    def _():
        y_ref[...] = jnp.zeros_like(y_ref)

    @pl.when(nvalid > 0)
    def _():
        tm, d = y_ref.shape
        row = lax.broadcasted_iota(jnp.int32, (tm, 1), 0)
        real = row < nvalid
        h = jnp.where(real, hx_ref[:, 0:d], 0.0).astype(BF16)
        wts = jnp.where(real, hx_ref[:, d:d + LANES], 0.0)

        def expert(w1_ref, w3_ref, w2_ref):
            a = _dot(h, w1_ref[...])
            g = (a * jax.nn.sigmoid(a)) * _dot(h, w3_ref[...])
            return _dot(g.astype(BF16), w2_ref[...])

        y_ref[...] = (wts[:, 0:1] * expert(w1a_ref, w3a_ref, w2a_ref)
                      + wts[:, 1:2] * expert(w1b_ref, w3b_ref, w2b_ref))


def _moe_routed(hx_sorted, ea, eb, valid, w1, w3, w2, tile_rows):
    n_slots, w = hx_sorted.shape
    ne, d, de = w1.shape
    up = lambda sel: pl.BlockSpec((None, d, de), lambda j, ea, eb, valid: ((ea, eb)[sel][j], 0, 0))
    down = lambda sel: pl.BlockSpec((None, de, d), lambda j, ea, eb, valid: ((ea, eb)[sel][j], 0, 0))
    grid_spec = pltpu.PrefetchScalarGridSpec(
        num_scalar_prefetch=3,
        grid=(n_slots // tile_rows,),
        in_specs=[pl.BlockSpec((tile_rows, w), lambda j, ea, eb, valid: (j, 0)),
                  up(0), up(0), down(0), up(1), up(1), down(1)],
        out_specs=pl.BlockSpec((tile_rows, d), lambda j, ea, eb, valid: (j, 0)),
    )
    return pl.pallas_call(
        _moe_routed_kernel,
        out_shape=jax.ShapeDtypeStruct((n_slots, d), F32),
        grid_spec=grid_spec,
        compiler_params=_cparams("arbitrary"),
        name="moe_routed",
    )(ea, eb, valid, hx_sorted, w1, w3, w2, w1, w3, w2)


def _combine_kernel(pos_ref, ys_ref, x1_ref, gt2_ref, y_ref, buf, sem):
    tm = buf.shape[0]

    def row_copy(r):
        return pltpu.make_async_copy(ys_ref.at[pl.ds(pos_ref[0, r], 1)], buf.at[pl.ds(r, 1)], sem)

    _start_and_wait_rows(row_copy, tm)
    y_ref[...] = x1_ref[...] + gt2_ref[...] * buf[...]


def _combine(pos, ys, x1, gt2, tm, rows_per_group):
    t, d = x1.shape
    return pl.pallas_call(
        _combine_kernel,
        out_shape=jax.ShapeDtypeStruct((t, d), F32),
        grid=(t // tm,),
        in_specs=[pl.BlockSpec((None, 1, tm), lambda i: (i, 0, 0), memory_space=pltpu.SMEM),
                  pl.BlockSpec(memory_space=pl.ANY),
                  pl.BlockSpec((tm, d), lambda i: (i, 0)),
                  _mod_spec(gt2, tm, rows_per_group, 1)],
        out_specs=pl.BlockSpec((tm, d), lambda i: (i, 0)),
        scratch_shapes=[pltpu.VMEM((tm, d), F32), pltpu.SemaphoreType.DMA],
        compiler_params=_cparams("arbitrary"),
        name="combine",
    )(pos.reshape(t // tm, 1, tm), ys, x1, gt2)


def _moe_kernel(h_ref, cmb_ref, x1_ref, gt2_ref, w1_ref, w3_ref, w2_ref, y_ref, acc_scr):
    e = pl.program_id(1)

    @pl.when(e == 0)
    def _():
        acc_scr[...] = jnp.zeros_like(acc_scr)

    h = h_ref[...]
    a = _dot(h, w1_ref[...])
    g = (a * jax.nn.sigmoid(a)) * _dot(h, w3_ref[...])
    ff = _dot(g.astype(BF16), w2_ref[...])
    cmb = cmb_ref[...]
    lane = lax.broadcasted_iota(jnp.int32, cmb.shape, 1)
    w = jnp.sum(jnp.where(lane == e, cmb, 0.0), axis=-1, keepdims=True)
    acc_scr[...] += w * ff

    @pl.when(e == pl.num_programs(1) - 1)
    def _():
        y_ref[...] = x1_ref[...] + gt2_ref[...] * acc_scr[...]


def _moe(h2, cmb, x1, gt2, w1, w3, w2, tm, rows_per_group):
    t, d = x1.shape
    ne, _, de = w1.shape
    gt2_spec = (pl.BlockSpec((None, 1, d), lambda i, e: (i // (rows_per_group // tm), 0, 0))
                if gt2.ndim == 3 else pl.BlockSpec((tm, d), lambda i, e: (i, 0)))
    return pl.pallas_call(
        _moe_kernel,
        out_shape=jax.ShapeDtypeStruct((t, d), F32),
        grid=(t // tm, ne),
        in_specs=[pl.BlockSpec((tm, d), lambda i, e: (i, 0)),
                  pl.BlockSpec((tm, LANES), lambda i, e: (i, 0)),
                  pl.BlockSpec((tm, d), lambda i, e: (i, 0)),
                  gt2_spec,
                  pl.BlockSpec((None, d, de), lambda i, e: (e, 0, 0)),
                  pl.BlockSpec((None, d, de), lambda i, e: (e, 0, 0)),
                  pl.BlockSpec((None, de, d), lambda i, e: (e, 0, 0))],
        out_specs=pl.BlockSpec((tm, d), lambda i, e: (i, 0)),
        scratch_shapes=[pltpu.VMEM((tm, d), F32)],
        compiler_params=_cparams("arbitrary", "arbitrary"),
        name="moe",
    )(h2, cmb, x1, gt2, w1, w3, w2)


def _rope_tables(pos, gq, gk):
    posf = pos.astype(F32)[:, None]
    inv_r = ROPE_THETA ** (-jnp.arange(0, RET_DK, 2, dtype=F32) / RET_DK)
    ang_r = posf * inv_r[None, :]
    inv_d = ROPE_THETA ** (-jnp.arange(0, DIFF_DH, 2, dtype=F32) / DIFF_DH)
    lane = jnp.arange(LANES)
    half = DIFF_DH // 2
    ang_d = posf * inv_d[lane % half][None, :]
    first_half = ((lane % DIFF_DH) < half)[None, :]
    cos_d, sin_d = jnp.cos(ang_d), jnp.sin(ang_d)

    def diff_tables(g, scale):
        g128 = jnp.tile(g, LANES // DIFF_DH)
        return (cos_d * (g128 * scale)[None, :],
                jnp.where(first_half, -sin_d, 0.0) * (jnp.roll(g128, -half) * scale)[None, :],
                jnp.where(first_half, 0.0, sin_d) * (jnp.roll(g128, half) * scale)[None, :])

    return (jnp.cos(ang_r), jnp.sin(ang_r)) + diff_tables(gq, DIFF_DH ** -0.5 * LOG2E) + diff_tables(gk, 1.0)


def _tile_rows(n, target):
    tm = min(n, target)
    assert n % tm == 0
    return tm


def _layer(x, mod, pos, s0, kc, vc, lw, li, ret_chunk):
    nb, seq, d = x.shape
    t = nb * seq
    x2d = x.reshape(t, d)
    per_token = seq < 128
    if per_token:
        mods = [jnp.repeat(mod[:, k, :], seq, axis=0) for k in range(6)]
        tabs = tuple(jnp.tile(tb, (nb, 1)) for tb in _rope_tables(pos, lw['gq'], lw['gk']))
        rows_per_group = t
    else:
        mods = [mod[:, k:k + 1, :] for k in range(6)]
        tabs = _rope_tables(pos, lw['gq'], lw['gk'])
        rows_per_group = seq
    sh1, sc1, gt1, sh2, sc2, gt2 = mods
    tm = _tile_rows(rows_per_group, 512)

    p, qd, k2d, v2d = _inproj(x2d, sc1, sh1, lw['norm1_g'], lw['w_in'], tabs, lw['bd'],
                              _tile_rows(rows_per_group, 1024), rows_per_group)
    lam_init = 0.8 - 0.6 * math.exp(-0.3 * li)
    if kc is None:
        dif_o = _diffattn(qd, k2d, v2d, lw['lam'], lw['subln_g'], nb, seq, lam_init)
    else:
        dif_o = _diffattn_cached(qd, k2d, v2d, kc, vc, li, lw['lam'], lw['subln_g'], nb, seq, lam_init)
    ret_o, s_new = _retention(p, s0, nb, seq, ret_chunk)
    merge_args = (ret_o, dif_o, p, x2d, gt1, sc2, sh2, lw['norm2_g'], lw['w_br_ret'], lw['w_br_diff'],
                  lw['w_out'], lw['rw_hi'], lw['rw_lo'], lw['rb'], tm, rows_per_group)
    if per_token:
        x1, h2, cmb = _merge(*merge_args, routed=False)
        y = _moe(h2, cmb, x1, gt2, lw['w1'], lw['w3'], lw['w2'], tm, rows_per_group)
    else:
        x1, hx, cls, hist = _merge(*merge_args, routed=True)
        n_tiles = SUBLANES * pl.cdiv(t // MOE_TILE_ROWS + N_CLASSES, SUBLANES)
        pos, tinfo = _route(cls, hist, MOE_TILE_ROWS, n_tiles, tm)
        hx_sorted = _dispatch(pos, hx, n_tiles * MOE_TILE_ROWS, tm)
        y_sorted = _moe_routed(hx_sorted, tinfo[:, 0], tinfo[:, 1], tinfo[:, 2],
                               lw['w1'], lw['w3'], lw['w2'], MOE_TILE_ROWS)
        y = _combine(pos, y_sorted, x1, gt2, tm, rows_per_group)
    return (y.reshape(nb, seq, d), s_new, k2d.reshape(nb, seq, DIFF_HEADS, 2 * DIFF_DH),
            v2d.reshape(nb, seq, DIFF_HEADS, DIFF_DV))


def kernel(x_prompt, x_sample, c_prompt, c_sample, cache_diff_k, cache_diff_v, state_ret, w_ada, b_ada, norm1_g, w_in, diff_qnorm_g, diff_knorm_g, diff_lambda_q1, diff_lambda_k1, diff_lambda_q2, diff_lambda_k2, diff_subln_g, w_br_ret, w_br_diff, w_out, norm2_g, w_group, b_group, w_expert_router, b_expert, w1, w3, w2):
    nb, seq, d = x_prompt.shape
    nbs, seqs, _ = x_sample.shape
    depth = w_in.shape[0]
    past = cache_diff_k.shape[2]
    pos_p = jnp.arange(seq, dtype=jnp.int32)
    pos_s = past + jnp.arange(seqs, dtype=jnp.int32)
    c_all = jnp.concatenate([c_prompt, c_sample], axis=0)
    col = jnp.arange(MXU_COLS)
    bd = jnp.where((col[:, None] // DIFF_DH) == (col[None, :] // DIFF_DH), 1.0 / DIFF_DH, 0.0).astype(BF16)

    xp, xs = x_prompt, x_sample
    outs = [[] for _ in range(6)]
    for li in range(depth):
        rw = jnp.zeros((d, LANES), F32)
        rw = rw.at[:, :N_GROUPS].set(w_group[li]).at[:, N_GROUPS:N_GROUPS + N_EXPERTS].set(w_expert_router[li])
        rw_hi = rw.astype(BF16)
        rb = jnp.zeros((1, LANES), F32)
        rb = rb.at[0, :N_GROUPS].set(b_group[li]).at[0, N_GROUPS:N_GROUPS + N_EXPERTS].set(b_expert[li])
        lw = {
            'norm1_g': norm1_g[li].reshape(1, d), 'norm2_g': norm2_g[li].reshape(1, d),
            'w_in': jnp.stack([w_in[li, :, blk * COL_BLOCK:(blk + 1) * COL_BLOCK] for blk in IN_BLOCK_ORDER]
                              ).astype(BF16),
            'gq': diff_qnorm_g[li], 'gk': diff_knorm_g[li],
            'bd': bd,
            'lam': jnp.stack([diff_lambda_q1[li], diff_lambda_k1[li], diff_lambda_q2[li], diff_lambda_k2[li]]),
            'subln_g': diff_subln_g[li].reshape(1, DIFF_DV),
            'w_br_ret': w_br_ret[li].astype(BF16), 'w_br_diff': w_br_diff[li].astype(BF16),
            'w_out': w_out[li].astype(BF16),
            'rw_hi': rw_hi, 'rw_lo': (rw - rw_hi.astype(F32)).astype(BF16), 'rb': rb,
            'w1': w1[li].astype(BF16), 'w3': w3[li].astype(BF16), 'w2': w2[li].astype(BF16),
        }
        mod = _adaln(c_all, w_ada[li], b_ada[li]).reshape(nb + nbs, 6, d)
        xp, sp, kp, vp = _layer(xp, mod[:nb], pos_p, None, None, None, lw, li, ret_chunk=min(seq, 256))
        xs, ss, kss, vss = _layer(xs, mod[nb:], pos_s, state_ret[li], cache_diff_k, cache_diff_v,
                                  lw, li, ret_chunk=seqs)
        for lst, val in zip(outs, (kp, vp, sp, kss, vss, ss)):
            lst.append(val)
    kp, vp, sp, kss, vss, ss = (jnp.stack(lst) for lst in outs)
    return (xp, xs, kp, vp, sp.astype(x_prompt.dtype), kss, vss, ss.astype(state_ret.dtype))
```

```python
import functools
import math

import jax
import jax.numpy as jnp
from jax import lax
from jax.experimental import pallas as pl
from jax.experimental.pallas import tpu as pltpu

F32 = jnp.float32
BF16 = jnp.bfloat16

D_MODEL = 1024
CHUNK = 64
ROPE_THETA = 10000.0
EPS = 1e-6
RET_HEADS = 4
RET_DK = D_MODEL // RET_HEADS
RET_DV = 2 * RET_DK
DIFF_HEADS = 8
DIFF_DH = D_MODEL // (2 * DIFF_HEADS)
DIFF_DV = 2 * DIFF_DH
N_GROUPS = 4
EXPERTS_PER_GROUP = 4
N_EXPERTS = N_GROUPS * EXPERTS_PER_GROUP
D_EXPERT = 512
N_PAIRS = EXPERTS_PER_GROUP * (EXPERTS_PER_GROUP - 1) // 2
N_CLASSES = N_GROUPS * N_PAIRS
MOE_TILE_ROWS = 512
ATTN_ROW_PART = 256
MERGE_ROW_PARTS = 1
DMA_UNROLL = 8

LANES = 128
SUBLANES = 8
MXU_COLS = 256
COL_BLOCK = 1024
IN_BLOCK_ORDER = (0, 1, 2, 3, 4, 5, 6, 9, 10, 7, 8)
N_SLAB_BLOCKS = 8
STRIPS = 4
SLAB_RQ, SLAB_RK, SLAB_RV, SLAB_RG, SLAB_GATE = 0, 1, 2, 4, 6
VMEM_LIMIT = 56 * 1024 * 1024
LOG2E = math.log2(math.e)


def _cparams(*sem):
    return pltpu.CompilerParams(dimension_semantics=sem, vmem_limit_bytes=VMEM_LIMIT)


def _dot(a, b):
    return jnp.dot(a, b, preferred_element_type=F32)


def _dot_nt(a, b):
    return lax.dot_general(a, b, (((1,), (1,)), ((), ())), preferred_element_type=F32)


def _adaln_kernel(c_ref, w_ref, b_ref, o_ref):
    c = c_ref[...]
    a = c * jax.nn.sigmoid(c)
    o_ref[...] = jnp.dot(a, w_ref[...], preferred_element_type=F32,
                         precision=lax.Precision.HIGHEST) + b_ref[...]


def _adaln(c, w_ada, b_ada):
    rows, d = c.shape
    n = w_ada.shape[1]
    tn = 1536
    return pl.pallas_call(
        _adaln_kernel,
        out_shape=jax.ShapeDtypeStruct((rows, n), F32),
        grid=(n // tn,),
        in_specs=[pl.BlockSpec((rows, d), lambda j: (0, 0)),
                  pl.BlockSpec((d, tn), lambda j: (0, j)),
                  pl.BlockSpec((1, tn), lambda j: (0, j))],
        out_specs=pl.BlockSpec((rows, tn), lambda j: (0, j)),
        compiler_params=_cparams("arbitrary"),
        name="adaln",
    )(c, w_ada, b_ada.reshape(1, n))


def _rms(x):
    return x * lax.rsqrt(jnp.mean(x * x, axis=-1, keepdims=True) + EPS)


def _inproj_kernel(x_ref, sc_ref, sh_ref, g1_ref, w_ref, rc_ref, rs_ref, qc_ref, qsn_ref, qsp_ref,
                   kc_ref, ksn_ref, ksp_ref, bd_ref, p_ref, qd_ref, k_ref, v_ref, h_scr):
    j = pl.program_id(1)

    @pl.when(j == 0)
    def _():
        h = _rms(x_ref[...]) * g1_ref[...]
        h = h * (1.0 + sc_ref[...]) + sh_ref[...]
        h_scr[...] = h.astype(BF16)

    def strips():
        h = h_scr[...]
        starts = list(range(0, COL_BLOCK, MXU_COLS))
        ahead = _dot(h, w_ref[:, 0:MXU_COLS])
        for n, lo in enumerate(starts):
            acc = ahead
            if n + 1 < len(starts):
                ahead = _dot(h, w_ref[:, starts[n + 1]:starts[n + 1] + MXU_COLS])
            yield lo, acc

    def ret_rope(scale):
        c = rc_ref[...]
        s = rs_ref[...]
        for lo, acc in strips():
            x1 = acc[:, 0:LANES]
            x2 = acc[:, LANES:2 * LANES]
            p_ref[lo // MXU_COLS, :, 0:LANES] = ((x1 * c - x2 * s) * scale).astype(BF16)
            p_ref[lo // MXU_COLS, :, LANES:2 * LANES] = ((x1 * s + x2 * c) * scale).astype(BF16)

    def diff_norm():
        for lo, acc in list(strips()):
            k_ref[:, lo:lo + MXU_COLS] = acc * lax.rsqrt(_dot((acc * acc).astype(BF16), bd_ref[...]) + EPS)

    def diff_rope(lo, c_ref, sn_ref, sp_ref, out_ref):
        for off in range(lo, lo + MXU_COLS, LANES):
            yg = k_ref[:, off:off + LANES]
            o = (yg * c_ref[...] + pltpu.roll(yg, LANES - DIFF_DH // 2, 1) * sn_ref[...]
                 + pltpu.roll(yg, DIFF_DH // 2, 1) * sp_ref[...])
            out_ref[:, off:off + LANES] = o.astype(out_ref.dtype)

    def elementwise(fn, out_ref, after_strip=None):
        for lo, acc in strips():
            if out_ref is p_ref:
                p_ref[lo // MXU_COLS] = fn(acc).astype(BF16)
            else:
                out_ref[:, lo:lo + MXU_COLS] = fn(acc).astype(out_ref.dtype)
            if after_strip is not None:
                after_strip(lo)

    @pl.when(j == 0)
    def _():
        ret_rope(1.0)

    @pl.when(j == 1)
    def _():
        ret_rope(RET_DK ** -0.5)

    @pl.when((j == 2) | (j == 3))
    def _():
        elementwise(lambda a: a, p_ref)

    @pl.when((j == 4) | (j == 5))
    def _():
        elementwise(lambda a: a * jax.nn.sigmoid(a), p_ref)

    @pl.when((j == 6) | (j == 9))
    def _():
        diff_norm()

    @pl.when(j == 7)
    def _():
        elementwise(jax.nn.sigmoid, p_ref, lambda lo: diff_rope(lo, qc_ref, qsn_ref, qsp_ref, qd_ref))

    @pl.when(j == 8)
    def _():
        elementwise(jax.nn.sigmoid, p_ref)

    @pl.when(j == 10)
    def _():
        elementwise(lambda a: a, v_ref, lambda lo: diff_rope(lo, kc_ref, ksn_ref, ksp_ref, k_ref))


def _mod_spec(arr, tm, rows_per_group, ngrid):
    d = arr.shape[-1]
    if arr.ndim == 3:
        tiles_per_group = rows_per_group // tm
        if ngrid == 2:
            return pl.BlockSpec((None, 1, d), lambda i, j: (i // tiles_per_group, 0, 0))
        return pl.BlockSpec((None, 1, d), lambda i: (i // tiles_per_group, 0, 0))
    if ngrid == 2:
        return pl.BlockSpec((tm, d), lambda i, j: (i, 0))
    return pl.BlockSpec((tm, d), lambda i: (i, 0))


def _inproj(x2d, sc1, sh1, g1, w_blocks, tabs, bd, tm, rows_per_group):
    t, d = x2d.shape
    nrep = tabs[0].shape[0] // tm

    def slab_map(i, j):
        return (jnp.where(j < 6, j, jnp.where(j < 9, jnp.maximum(j - 1, 5), N_SLAB_BLOCKS - 1)), i, 0)

    tab_spec = pl.BlockSpec((tm, LANES), lambda i, j: (i % nrep, 0))
    row_spec = pl.BlockSpec((tm, d), lambda i, j: (i, 0))
    return pl.pallas_call(
        _inproj_kernel,
        out_shape=(jax.ShapeDtypeStruct((N_SLAB_BLOCKS * STRIPS, t, MXU_COLS), BF16),
                   jax.ShapeDtypeStruct((t, d), BF16),
                   jax.ShapeDtypeStruct((t, d), F32),
                   jax.ShapeDtypeStruct((t, d), F32)),
        grid=(t // tm, len(IN_BLOCK_ORDER)),
        in_specs=[row_spec,
                  _mod_spec(sc1, tm, rows_per_group, 2),
                  _mod_spec(sh1, tm, rows_per_group, 2),
                  pl.BlockSpec((1, d), lambda i, j: (0, 0)),
                  pl.BlockSpec((None, d, COL_BLOCK), lambda i, j: (j, 0, 0)),
                  *([tab_spec] * len(tabs)),
                  pl.BlockSpec((MXU_COLS, MXU_COLS), lambda i, j: (0, 0))],
        out_specs=(pl.BlockSpec((STRIPS, tm, MXU_COLS), slab_map), row_spec, row_spec, row_spec),
        scratch_shapes=[pltpu.VMEM((tm, d), BF16)],
        compiler_params=_cparams("arbitrary", "arbitrary"),
        name="inproj",
    )(x2d, sc1, sh1, g1, w_blocks, *tabs, bd)


def _retention_kernel(*refs, has_state, chunk):
    if has_state:
        q_ref, k_ref, v_ref, rg_ref, dm_ref, dv_ref, s0_ref, o_ref, sout_ref = refs
        s = s0_ref[...]
    else:
        q_ref, k_ref, v_ref, rg_ref, dm_ref, dv_ref, o_ref, sout_ref = refs
        s = jnp.zeros(sout_ref.shape, F32)
    dvec = dv_ref[...]
    q_decay = dvec[:, 0:1]
    k_decay = dvec[:, 1:2]
    chunk_decay = dvec[0:1, 2:3]

    def wide(ref, rows):
        return jnp.concatenate([ref[0, rows, :], ref[1, rows, :]], axis=1)

    for c in range(q_ref.shape[0] // chunk):
        rows = slice(c * chunk, (c + 1) * chunk)
        q = q_ref[rows, :]
        k = k_ref[rows, :]
        v = wide(v_ref, rows)
        scores = (_dot_nt(q, k) * dm_ref[...]).astype(BF16)
        o = _dot(scores, v) + _dot(q, s.astype(BF16)) * q_decay
        o = _rms(o) * wide(rg_ref, rows).astype(F32)
        o_ref[rows, :] = o.astype(BF16)
        kd_t = (k.astype(F32) * k_decay).T.astype(BF16)
        s = s * chunk_decay + _dot(kd_t, v)
    sout_ref[...] = s


def _retention(p, s0, nb, seq, chunk):
    t = p.shape[1]
    idx = jnp.arange(chunk, dtype=F32)
    log_gamma = jnp.log1p(-jnp.exp2(-5.0 - jnp.arange(RET_HEADS, dtype=F32)))
    rel = idx[:, None] - idx[None, :]
    dmat = jnp.where(rel >= 0, jnp.exp(jnp.maximum(rel, 0.0)[None] * log_gamma[:, None, None]), 0.0)
    q_decay = jnp.exp((idx + 1.0)[None, :] * log_gamma[:, None])
    k_decay = jnp.exp((chunk - 1.0 - idx)[None, :] * log_gamma[:, None])
    c_decay = jnp.broadcast_to(jnp.exp(chunk * log_gamma)[:, None], (RET_HEADS, chunk))
    dvec = jnp.zeros((RET_HEADS, chunk, LANES), F32)
    dvec = dvec.at[:, :, 0].set(q_decay).at[:, :, 1].set(k_decay).at[:, :, 2].set(c_decay)

    vstrips = RET_DV // MXU_COLS
    in_specs = [pl.BlockSpec((None, seq, RET_DK), lambda b, h: (SLAB_RQ * STRIPS + h, b, 0)),
                pl.BlockSpec((None, seq, RET_DK), lambda b, h: (SLAB_RK * STRIPS + h, b, 0)),
                pl.BlockSpec((vstrips, seq, MXU_COLS), lambda b, h: (SLAB_RV * STRIPS // vstrips + h, b, 0)),
                pl.BlockSpec((vstrips, seq, MXU_COLS), lambda b, h: (SLAB_RG * STRIPS // vstrips + h, b, 0)),
                pl.BlockSpec((None, chunk, chunk), lambda b, h: (h, 0, 0)),
                pl.BlockSpec((None, chunk, LANES), lambda b, h: (h, 0, 0))]
    args = [p, p, p, p, dmat, dvec]
    state_spec = pl.BlockSpec((None, None, RET_DK, RET_DV), lambda b, h: (b, h, 0, 0))
    if s0 is not None:
        in_specs.append(state_spec)
        args.append(s0)
    return pl.pallas_call(
        functools.partial(_retention_kernel, has_state=s0 is not None, chunk=chunk),
        out_shape=(jax.ShapeDtypeStruct((RET_HEADS, t, RET_DV), BF16),
                   jax.ShapeDtypeStruct((nb, RET_HEADS, RET_DK, RET_DV), F32)),
        grid=(nb, RET_HEADS),
        in_specs=in_specs,
        out_specs=(pl.BlockSpec((None, seq, RET_DV), lambda b, h: (h, b, 0)), state_spec),
        compiler_params=_cparams("arbitrary", "arbitrary"),
        name="retention",
    )(*args)


def _lambda(lam_ref, lam_init):
    l1 = jnp.sum(lam_ref[0:1, :] * lam_ref[1:2, :], axis=-1, keepdims=True)
    l2 = jnp.sum(lam_ref[2:3, :] * lam_ref[3:4, :], axis=-1, keepdims=True)
    return jnp.exp(l1) - jnp.exp(l2) + lam_init


def _split_maps(q):
    lane = lax.broadcasted_iota(jnp.int32, q.shape, 1)
    zero = jnp.zeros_like(q)
    return jnp.where(lane < DIFF_DH, q, zero), jnp.where(lane >= DIFF_DH, q, zero)


def _diffattn_kernel(q_ref, k_ref, v_ref, lam_ref, g_ref, o_ref, kb_scr, vb_scr, *, part, lam_init):
    kb_scr[...] = k_ref[...].astype(BF16)
    vb_scr[:, 0:DIFF_DV] = v_ref[...].astype(BF16)
    vb_scr[:, DIFF_DV:2 * DIFF_DV] = jnp.ones((k_ref.shape[0], DIFF_DV), BF16)
    lam = _lambda(lam_ref, lam_init)
    row = lax.broadcasted_iota(jnp.int32, (part, part), 0)
    col = lax.broadcasted_iota(jnp.int32, (part, part), 1)
    visible = (col // CHUNK) <= (row // CHUNK)

    def scores(q0):
        qmaps = _split_maps(q_ref[q0:q0 + part, :])
        kblk = kb_scr[0:q0 + part, :]
        out = []
        for mp in range(2):
            sc = _dot_nt(qmaps[mp], kblk)
            diag = jnp.where(visible, sc[:, q0:q0 + part], -1e30)
            out.append(jnp.concatenate([sc[:, 0:q0], diag], axis=1) if q0 else diag)
        return out

    starts = list(range(0, k_ref.shape[0], part))
    pending = scores(starts[0])
    for n, q0 in enumerate(starts):
        scs = pending
        if n + 1 < len(starts):
            pending = scores(starts[n + 1])
        vblk = vb_scr[0:q0 + part, :]
        att = []
        for mp in range(2):
            sc = scs[mp]
            pv = _dot(jnp.exp2(sc - jnp.max(sc, axis=-1, keepdims=True)).astype(BF16), vblk)
            att.append(pv[:, 0:DIFF_DV] / pv[:, DIFF_DV:2 * DIFF_DV])
        o = att[0] - lam * att[1]
        o = _rms(o) * g_ref[...] * (1.0 - lam_init)
        o_ref[q0:q0 + part, :] = o.astype(BF16)


def _diffattn(qd, k2d, v2d, lam_vecs, subln_g, nb, seq, lam_init):
    t = qd.shape[0]
    part = min(seq, ATTN_ROW_PART)
    assert seq % part == 0 and part % CHUNK == 0
    return pl.pallas_call(
        functools.partial(_diffattn_kernel, part=part, lam_init=lam_init),
        out_shape=jax.ShapeDtypeStruct((t, DIFF_HEADS * DIFF_DV), BF16),
        grid=(nb, DIFF_HEADS),
        in_specs=[pl.BlockSpec((seq, DIFF_DV), lambda b, h: (b, h)),
                  pl.BlockSpec((seq, DIFF_DV), lambda b, h: (b, h)),
                  pl.BlockSpec((seq, DIFF_DV), lambda b, h: (b, h)),
                  pl.BlockSpec((4, DIFF_DH), lambda b, h: (0, 0)),
                  pl.BlockSpec((1, DIFF_DV), lambda b, h: (0, 0))],
        out_specs=pl.BlockSpec((seq, DIFF_DV), lambda b, h: (b, h)),
        scratch_shapes=[pltpu.VMEM((seq, DIFF_DV), BF16), pltpu.VMEM((seq, 2 * DIFF_DV), BF16)],
        compiler_params=_cparams("arbitrary", "arbitrary"),
        name="diffattn",
    )(qd, k2d, v2d, lam_vecs, subln_g)


def _diffattn_cached_kernel(q_ref, kc_ref, vc_ref, kn_ref, vn_ref, lam_ref, g_ref, o_ref, k_scr, v_scr,
                            *, past, new, lam_init):
    total = k_scr.shape[0]
    pad = jnp.zeros((total - past, DIFF_DV), BF16)
    qpos = past + lax.broadcasted_iota(jnp.int32, (new, total), 0)
    kpos = lax.broadcasted_iota(jnp.int32, (new, total), 1)
    visible = ((kpos // CHUNK) <= (qpos // CHUNK)) & (kpos < past + new)
    lam = _lambda(lam_ref, lam_init)

    for h in range(DIFF_HEADS):
        cols = slice(h * DIFF_DV, (h + 1) * DIFF_DV)
        k_scr[0:past, :] = kc_ref[:, h, :].astype(BF16)
        v_scr[0:past, :] = vc_ref[:, h, :].astype(BF16)
        k_scr[past:total, :] = pad
        v_scr[past:total, :] = pad
        k_scr[past:past + new, :] = kn_ref[:, cols].astype(BF16)
        v_scr[past:past + new, :] = vn_ref[:, cols].astype(BF16)
        kall = k_scr[...]
        vall = v_scr[...]

        def softmax_av(qm):
            sc = jnp.where(visible, _dot_nt(qm, kall), -1e30)
            m = jnp.max(sc, axis=-1, keepdims=True)
            p = jnp.exp2(sc - m)
            return _dot(p.astype(BF16), vall) / jnp.sum(p, axis=-1, keepdims=True)

        qa, qb = _split_maps(q_ref[:, cols])
        o = softmax_av(qa) - lam * softmax_av(qb)
        o = _rms(o) * g_ref[...] * (1.0 - lam_init)
        o_ref[:, cols] = o.astype(BF16)


def _diffattn_cached(qd, k2d, v2d, kc, vc, li, lam_vecs, subln_g, nb, new, lam_init):
    t = qd.shape[0]
    past = kc.shape[2]
    width = DIFF_HEADS * DIFF_DV
    total = past + LANES * pl.cdiv(new, LANES)
    cache_spec = pl.BlockSpec((None, None, past, DIFF_HEADS, DIFF_DV), lambda b: (li, b, 0, 0, 0))
    return pl.pallas_call(
        functools.partial(_diffattn_cached_kernel, past=past, new=new, lam_init=lam_init),
        out_shape=jax.ShapeDtypeStruct((t, width), BF16),
        grid=(nb,),
        in_specs=[pl.BlockSpec((new, width), lambda b: (b, 0)),
                  cache_spec, cache_spec,
                  pl.BlockSpec((new, width), lambda b: (b, 0)),
                  pl.BlockSpec((new, width), lambda b: (b, 0)),
                  pl.BlockSpec((4, DIFF_DH), lambda b: (0, 0)),
                  pl.BlockSpec((1, DIFF_DV), lambda b: (0, 0))],
        out_specs=pl.BlockSpec((new, width), lambda b: (b, 0)),
        scratch_shapes=[pltpu.VMEM((total, DIFF_DV), BF16), pltpu.VMEM((total, DIFF_DV), BF16)],
        compiler_params=_cparams("arbitrary"),
        name="diffattn_cached",
    )(qd, kc, vc, k2d, v2d, lam_vecs, subln_g)


def _merge_kernel(ret_ref, dif_ref, gr_ref, gd_ref, x_ref, gt1_ref, sc2_ref, sh2_ref, g2_ref,
                  wr_ref, wd_ref, wo_ref, rwh_ref, rwl_ref, rb_ref, x1_ref, *out_refs, routed):
    tm, d = x_ref.shape
    n_parts = MERGE_ROW_PARTS if tm % (MERGE_ROW_PARTS * LANES) == 0 else 1
    part = tm // n_parts

    def mod_rows(ref, rows):
        return ref[rows, :] if ref.shape[0] > 1 else ref[...]

    hist = jnp.zeros((1, LANES), F32)
    for r in range(n_parts):
        rows = slice(r * part, (r + 1) * part)
        def lanes(ref):
            return jnp.concatenate([ref[n, rows, :] for n in range(ref.shape[0])], axis=1)

        a = _dot(lanes(ret_ref), wr_ref[...])
        b = _dot(dif_ref[rows, :], wd_ref[...])
        merged = lanes(gr_ref).astype(F32) * a + lanes(gd_ref).astype(F32) * b
        x1 = x_ref[rows, :] + mod_rows(gt1_ref, rows) * _dot(merged.astype(BF16), wo_ref[...])
        x1_ref[rows, :] = x1
        h2 = _rms(x1) * g2_ref[...]
        h2 = h2 * (1.0 + mod_rows(sc2_ref, rows)) + mod_rows(sh2_ref, rows)

        hi = h2.astype(BF16)
        lo = (h2 - hi.astype(F32)).astype(BF16)
        logits = _dot(hi, rwh_ref[...]) + _dot(lo, rwh_ref[...]) + _dot(hi, rwl_ref[...]) + rb_ref[...]

        lane = lax.broadcasted_iota(jnp.int32, logits.shape, 1)
        big = jnp.int32(4 * LANES)

        def masked_argmax(mask):
            vmax = jnp.max(jnp.where(mask, logits, -jnp.inf), axis=-1, keepdims=True)
            idx = jnp.min(jnp.where(mask & (logits == vmax), lane, big), axis=-1, keepdims=True)
            return vmax, idx

        is_group = lane < N_GROUPS
        gmax, gidx = masked_argmax(is_group)
        gprob = 1.0 / jnp.sum(jnp.where(is_group, jnp.exp(logits - gmax), 0.0), axis=-1, keepdims=True)
        first = N_GROUPS + gidx * EXPERTS_PER_GROUP
        in_group = (lane >= first) & (lane < first + EXPERTS_PER_GROUP)
        v1, i1 = masked_argmax(in_group)
        v2, i2 = masked_argmax(in_group & (lane != i1))
        e2 = jnp.exp(v2 - v1)
        w1 = gprob / (1.0 + e2)
        w2 = gprob * e2 / (1.0 + e2)
        if not routed:
            h2_ref, cmb_ref = out_refs
            h2_ref[rows, :] = h2.astype(BF16)
            cmb_ref[rows, :] = (jnp.where(lane + N_GROUPS == i1, w1, 0.0)
                                + jnp.where(lane + N_GROUPS == i2, w2, 0.0))
            continue

        hx_ref, cls_ref, hist_ref = out_refs
        first_is_lo = i1 < i2
        lo_e = jnp.minimum(i1, i2) - first
        hi_e = jnp.maximum(i1, i2) - first
        cls = gidx * N_PAIRS + jnp.right_shift(lo_e * (7 - lo_e), 1) + hi_e - lo_e - 1
        hx_ref[rows, 0:d] = h2
        hx_ref[rows, d:d + LANES] = jnp.where(lane == 0, jnp.where(first_is_lo, w1, w2),
                                              jnp.where(lane == 1, jnp.where(first_is_lo, w2, w1), 0.0))
        cls_ref[rows, :] = cls
        hist = hist + jnp.sum(jnp.where(lane == cls, 1.0, 0.0), axis=0, keepdims=True)

    if routed:
        hist_ref = out_refs[2]

        @pl.when(pl.program_id(0) == 0)
        def _():
            hist_ref[...] = jnp.zeros_like(hist_ref)

        hist_ref[...] += hist


def _merge(ret_o, dif_o, p, x2d, gt1, sc2, sh2, g2, wr, wd, wo, rwh, rwl, rb, tm, rows_per_group, routed):
    t, d = x2d.shape
    const = lambda shape: pl.BlockSpec(shape, lambda i: (0, 0), pipeline_mode=pl.Buffered(1))
    if routed:
        out_shape = (jax.ShapeDtypeStruct((t, d), F32),
                     jax.ShapeDtypeStruct((t, d + LANES), F32),
                     jax.ShapeDtypeStruct((t, 1), jnp.int32),
                     jax.ShapeDtypeStruct((1, LANES), F32))
        out_specs = (pl.BlockSpec((tm, d), lambda i: (i, 0)),
                     pl.BlockSpec((tm, d + LANES), lambda i: (i, 0)),
                     pl.BlockSpec((tm, 1), lambda i: (i, 0)),
                     pl.BlockSpec((1, LANES), lambda i: (0, 0)))
    else:
        out_shape = (jax.ShapeDtypeStruct((t, d), F32),
                     jax.ShapeDtypeStruct((t, d), BF16),
                     jax.ShapeDtypeStruct((t, LANES), F32))
        out_specs = (pl.BlockSpec((tm, d), lambda i: (i, 0)),
                     pl.BlockSpec((tm, d), lambda i: (i, 0)),
                     pl.BlockSpec((tm, LANES), lambda i: (i, 0)))
    return pl.pallas_call(
        functools.partial(_merge_kernel, routed=routed),
        out_shape=out_shape,
        grid=(t // tm,),
        in_specs=[pl.BlockSpec((RET_HEADS, tm, RET_DV), lambda i: (0, i, 0)),
                  pl.BlockSpec((tm, d), lambda i: (i, 0)),
                  pl.BlockSpec((STRIPS, tm, MXU_COLS), lambda i: (SLAB_GATE, i, 0)),
                  pl.BlockSpec((STRIPS, tm, MXU_COLS), lambda i: (SLAB_GATE + 1, i, 0)),
                  pl.BlockSpec((tm, d), lambda i: (i, 0)),
                  _mod_spec(gt1, tm, rows_per_group, 1),
                  _mod_spec(sc2, tm, rows_per_group, 1),
                  _mod_spec(sh2, tm, rows_per_group, 1),
                  const((1, d)),
                  const(wr.shape), const(wd.shape), const(wo.shape),
                  const(rwh.shape), const(rwl.shape), const(rb.shape)],
        out_specs=out_specs,
        compiler_params=_cparams("arbitrary"),
        name="merge",
    )(ret_o, dif_o, p, p, x2d, gt1, sc2, sh2, g2, wr, wd, wo, rwh, rwl, rb)


def _route_kernel(cls_ref, hist_ref, tri_ref, pos_ref, tinfo_ref, carry_scr, off_scr, *, tile_rows):
    i = pl.program_id(0)

    def excl_cumsum(v):
        lane8 = lax.broadcasted_iota(jnp.int32, v.shape, 1)
        inc = v
        for sh in (1, 2, 4, 8, 16, 32, 64):
            inc = inc + jnp.where(lane8 >= sh, pltpu.roll(inc, sh, 1), 0.0)
        return inc - v

    @pl.when(i == 0)
    def _():
        cnt = hist_ref[...]
        ntile = jnp.floor((cnt + (tile_rows - 1)) * (1.0 / tile_rows))
        start = excl_cumsum(jnp.broadcast_to(ntile, (SUBLANES, LANES)))[0:1, :]
        off_scr[...] = start * tile_rows
        carry_scr[...] = jnp.zeros_like(carry_scr)
        nt = tinfo_ref.shape[0]
        jrow = lax.broadcasted_iota(jnp.int32, (nt, LANES), 0).astype(F32)
        lane = lax.broadcasted_iota(jnp.int32, (nt, LANES), 1)
        is_cls = lane < N_CLASSES
        tcls = jnp.sum(jnp.where(is_cls & (start <= jrow), 1.0, 0.0), axis=-1, keepdims=True) - 1.0
        pick = lane == tcls.astype(jnp.int32)
        cnt_j = jnp.sum(jnp.where(pick, cnt, 0.0), axis=-1, keepdims=True)
        start_j = jnp.sum(jnp.where(pick, start, 0.0), axis=-1, keepdims=True)
        valid = jnp.clip(cnt_j - (jrow[:, 0:1] - start_j) * tile_rows, 0.0, float(tile_rows)).astype(jnp.int32)
        tc = tcls.astype(jnp.int32)
        grp = sum((tc >= g * N_PAIRS).astype(jnp.int32) for g in range(1, N_GROUPS))
        pair = tc - grp * N_PAIRS
        lo = (pair >= 3).astype(jnp.int32) + (pair >= 5).astype(jnp.int32)
        hi = pair - jnp.right_shift(lo * (7 - lo), 1) + lo + 1
        ea = grp * EXPERTS_PER_GROUP + lo
        eb = grp * EXPERTS_PER_GROUP + hi
        tinfo_ref[...] = jnp.where(lane == 0, ea, jnp.where(lane == 1, eb, jnp.where(lane == 2, valid, 0)))

    cls = cls_ref[...]
    lane = lax.broadcasted_iota(jnp.int32, (cls.shape[0], LANES), 1)
    onehot = lane == cls
    earlier = _dot(tri_ref[...], jnp.where(onehot, 1.0, 0.0).astype(BF16))
    base = carry_scr[...] + off_scr[...]
    pos = jnp.sum(jnp.where(onehot, earlier + base, 0.0), axis=-1, keepdims=True)
    pos_ref[...] = pos.astype(jnp.int32)
    carry_scr[...] += jnp.sum(jnp.where(onehot, 1.0, 0.0), axis=0, keepdims=True)


def _route(cls, hist, tile_rows, n_tiles, tm):
    t = cls.shape[0]
    tri = jnp.tril(jnp.ones((tm, tm), BF16), -1)
    return pl.pallas_call(
        functools.partial(_route_kernel, tile_rows=tile_rows),
        out_shape=(jax.ShapeDtypeStruct((t, 1), jnp.int32),
                   jax.ShapeDtypeStruct((n_tiles, LANES), jnp.int32)),
        grid=(t // tm,),
        in_specs=[pl.BlockSpec((tm, 1), lambda i: (i, 0)),
                  pl.BlockSpec((1, LANES), lambda i: (0, 0)),
                  pl.BlockSpec((tm, tm), lambda i: (0, 0))],
        out_specs=(pl.BlockSpec((tm, 1), lambda i: (i, 0)),
                   pl.BlockSpec((n_tiles, LANES), lambda i: (0, 0))),
        scratch_shapes=[pltpu.VMEM((1, LANES), F32), pltpu.VMEM((1, LANES), F32)],
        compiler_params=_cparams("arbitrary"),
        name="route",
    )(cls, hist, tri)


def _start_and_wait_rows(row_copy, n_rows):
    def start(g, carry):
        for u in range(DMA_UNROLL):
            row_copy(g * DMA_UNROLL + u).start(priority=u % 2)
        return carry

    def wait(r, carry):
        row_copy(r).wait()
        return carry

    lax.fori_loop(0, n_rows // DMA_UNROLL, start, 0)
    lax.fori_loop(0, n_rows, wait, 0, unroll=DMA_UNROLL)


def _dispatch_kernel(pos_ref, hx_ref, sorted_ref, sem):
    tm = hx_ref.shape[0]

    def row_copy(r):
        return pltpu.make_async_copy(hx_ref.at[pl.ds(r, 1)], sorted_ref.at[pl.ds(pos_ref[0, r], 1)], sem)

    _start_and_wait_rows(row_copy, tm)


def _dispatch(pos, hx, n_slots, tm):
    t, w = hx.shape
    return pl.pallas_call(
        _dispatch_kernel,
        out_shape=jax.ShapeDtypeStruct((n_slots, w), F32),
        grid=(t // tm,),
        in_specs=[pl.BlockSpec((None, 1, tm), lambda i: (i, 0, 0), memory_space=pltpu.SMEM),
                  pl.BlockSpec((tm, w), lambda i: (i, 0))],
        out_specs=pl.BlockSpec(memory_space=pl.ANY),
        scratch_shapes=[pltpu.SemaphoreType.DMA],
        compiler_params=pltpu.CompilerParams(dimension_semantics=("arbitrary",), vmem_limit_bytes=VMEM_LIMIT,
                                             has_side_effects=True),
        name="dispatch",
    )(pos.reshape(t // tm, 1, tm), hx)


def _moe_routed_kernel(ea_ref, eb_ref, valid_ref, hx_ref, w1a_ref, w3a_ref, w2a_ref, w1b_ref, w3b_ref, w2b_ref, y_ref):
    j = pl.program_id(0)
    nvalid = valid_ref[j]

    @pl.when(nvalid == 0)
    def _():
        y_ref[...] = jnp.zeros_like(y_ref)

    @pl.when(nvalid > 0)
    def _():
        tm, d = y_ref.shape
        row = lax.broadcasted_iota(jnp.int32, (tm, 1), 0)
        real = row < nvalid
        h = jnp.where(real, hx_ref[:, 0:d], 0.0).astype(BF16)
        wts = jnp.where(real, hx_ref[:, d:d + LANES], 0.0)

        def expert(w1_ref, w3_ref, w2_ref):
            a = _dot(h, w1_ref[...])
            g = (a * jax.nn.sigmoid(a)) * _dot(h, w3_ref[...])
            return _dot(g.astype(BF16), w2_ref[...])

        y_ref[...] = (wts[:, 0:1] * expert(w1a_ref, w3a_ref, w2a_ref)
                      + wts[:, 1:2] * expert(w1b_ref, w3b_ref, w2b_ref))


def _moe_routed(hx_sorted, ea, eb, valid, w1, w3, w2, tile_rows):
    n_slots, w = hx_sorted.shape
    ne, d, de = w1.shape
    up = lambda sel: pl.BlockSpec((None, d, de), lambda j, ea, eb, valid: ((ea, eb)[sel][j], 0, 0))
    down = lambda sel: pl.BlockSpec((None, de, d), lambda j, ea, eb, valid: ((ea, eb)[sel][j], 0, 0))
    grid_spec = pltpu.PrefetchScalarGridSpec(
        num_scalar_prefetch=3,
        grid=(n_slots // tile_rows,),
        in_specs=[pl.BlockSpec((tile_rows, w), lambda j, ea, eb, valid: (j, 0)),
                  up(0), up(0), down(0), up(1), up(1), down(1)],
        out_specs=pl.BlockSpec((tile_rows, d), lambda j, ea, eb, valid: (j, 0)),
    )
    return pl.pallas_call(
        _moe_routed_kernel,
        out_shape=jax.ShapeDtypeStruct((n_slots, d), F32),
        grid_spec=grid_spec,
        compiler_params=_cparams("arbitrary"),
        name="moe_routed",
    )(ea, eb, valid, hx_sorted, w1, w3, w2, w1, w3, w2)


def _combine_kernel(pos_ref, ys_ref, x1_ref, gt2_ref, y_ref, buf, sem):
    tm = buf.shape[0]

    def row_copy(r):
        return pltpu.make_async_copy(ys_ref.at[pl.ds(pos_ref[0, r], 1)], buf.at[pl.ds(r, 1)], sem)

    _start_and_wait_rows(row_copy, tm)
    y_ref[...] = x1_ref[...] + gt2_ref[...] * buf[...]


def _combine(pos, ys, x1, gt2, tm, rows_per_group):
    t, d = x1.shape
    return pl.pallas_call(
        _combine_kernel,
        out_shape=jax.ShapeDtypeStruct((t, d), F32),
        grid=(t // tm,),
        in_specs=[pl.BlockSpec((None, 1, tm), lambda i: (i, 0, 0), memory_space=pltpu.SMEM),
                  pl.BlockSpec(memory_space=pl.ANY),
                  pl.BlockSpec((tm, d), lambda i: (i, 0)),
                  _mod_spec(gt2, tm, rows_per_group, 1)],
        out_specs=pl.BlockSpec((tm, d), lambda i: (i, 0)),
        scratch_shapes=[pltpu.VMEM((tm, d), F32), pltpu.SemaphoreType.DMA],
        compiler_params=_cparams("arbitrary"),
        name="combine",
    )(pos.reshape(t // tm, 1, tm), ys, x1, gt2)


def _moe_kernel(h_ref, cmb_ref, x1_ref, gt2_ref, w1_ref, w3_ref, w2_ref, y_ref, acc_scr):
    e = pl.program_id(1)

    @pl.when(e == 0)
    def _():
        acc_scr[...] = jnp.zeros_like(acc_scr)

    h = h_ref[...]
    a = _dot(h, w1_ref[...])
    g = (a * jax.nn.sigmoid(a)) * _dot(h, w3_ref[...])
    ff = _dot(g.astype(BF16), w2_ref[...])
    cmb = cmb_ref[...]
    lane = lax.broadcasted_iota(jnp.int32, cmb.shape, 1)
    w = jnp.sum(jnp.where(lane == e, cmb, 0.0), axis=-1, keepdims=True)
    acc_scr[...] += w * ff

    @pl.when(e == pl.num_programs(1) - 1)
    def _():
        y_ref[...] = x1_ref[...] + gt2_ref[...] * acc_scr[...]


def _moe(h2, cmb, x1, gt2, w1, w3, w2, tm, rows_per_group):
    t, d = x1.shape
    ne, _, de = w1.shape
    gt2_spec = (pl.BlockSpec((None, 1, d), lambda i, e: (i // (rows_per_group // tm), 0, 0))
                if gt2.ndim == 3 else pl.BlockSpec((tm, d), lambda i, e: (i, 0)))
    return pl.pallas_call(
        _moe_kernel,
        out_shape=jax.ShapeDtypeStruct((t, d), F32),
        grid=(t // tm, ne),
        in_specs=[pl.BlockSpec((tm, d), lambda i, e: (i, 0)),
                  pl.BlockSpec((tm, LANES), lambda i, e: (i, 0)),
                  pl.BlockSpec((tm, d), lambda i, e: (i, 0)),
                  gt2_spec,
                  pl.BlockSpec((None, d, de), lambda i, e: (e, 0, 0)),
                  pl.BlockSpec((None, d, de), lambda i, e: (e, 0, 0)),
                  pl.BlockSpec((None, de, d), lambda i, e: (e, 0, 0))],
        out_specs=pl.BlockSpec((tm, d), lambda i, e: (i, 0)),
        scratch_shapes=[pltpu.VMEM((tm, d), F32)],
        compiler_params=_cparams("arbitrary", "arbitrary"),
        name="moe",
    )(h2, cmb, x1, gt2, w1, w3, w2)


def _rope_tables(pos, gq, gk):
    posf = pos.astype(F32)[:, None]
    inv_r = ROPE_THETA ** (-jnp.arange(0, RET_DK, 2, dtype=F32) / RET_DK)
    ang_r = posf * inv_r[None, :]
    inv_d = ROPE_THETA ** (-jnp.arange(0, DIFF_DH, 2, dtype=F32) / DIFF_DH)
    lane = jnp.arange(LANES)
    half = DIFF_DH // 2
    ang_d = posf * inv_d[lane % half][None, :]
    first_half = ((lane % DIFF_DH) < half)[None, :]
    cos_d, sin_d = jnp.cos(ang_d), jnp.sin(ang_d)

    def diff_tables(g, scale):
        g128 = jnp.tile(g, LANES // DIFF_DH)
        return (cos_d * (g128 * scale)[None, :],
                jnp.where(first_half, -sin_d, 0.0) * (jnp.roll(g128, -half) * scale)[None, :],
                jnp.where(first_half, 0.0, sin_d) * (jnp.roll(g128, half) * scale)[None, :])

    return (jnp.cos(ang_r), jnp.sin(ang_r)) + diff_tables(gq, DIFF_DH ** -0.5 * LOG2E) + diff_tables(gk, 1.0)


def _tile_rows(n, target):
    tm = min(n, target)
    assert n % tm == 0
    return tm


def _layer(x, mod, pos, s0, kc, vc, lw, li, ret_chunk):
    nb, seq, d = x.shape
    t = nb * seq
    x2d = x.reshape(t, d)
    per_token = seq < 128
    if per_token:
        mods = [jnp.repeat(mod[:, k, :], seq, axis=0) for k in range(6)]
        tabs = tuple(jnp.tile(tb, (nb, 1)) for tb in _rope_tables(pos, lw['gq'], lw['gk']))
        rows_per_group = t
    else:
        mods = [mod[:, k:k + 1, :] for k in range(6)]
        tabs = _rope_tables(pos, lw['gq'], lw['gk'])
        rows_per_group = seq
    sh1, sc1, gt1, sh2, sc2, gt2 = mods
    tm = _tile_rows(rows_per_group, 512)

    p, qd, k2d, v2d = _inproj(x2d, sc1, sh1, lw['norm1_g'], lw['w_in'], tabs, lw['bd'],
                              _tile_rows(rows_per_group, 1024), rows_per_group)
    lam_init = 0.8 - 0.6 * math.exp(-0.3 * li)
    if kc is None:
        dif_o = _diffattn(qd, k2d, v2d, lw['lam'], lw['subln_g'], nb, seq, lam_init)
    else:
        dif_o = _diffattn_cached(qd, k2d, v2d, kc, vc, li, lw['lam'], lw['subln_g'], nb, seq, lam_init)
    ret_o, s_new = _retention(p, s0, nb, seq, ret_chunk)
    ret_o, k2d, v2d = lax.optimization_barrier((ret_o, k2d, v2d))
    merge_args = (ret_o, dif_o, p, x2d, gt1, sc2, sh2, lw['norm2_g'], lw['w_br_ret'], lw['w_br_diff'],
                  lw['w_out'], lw['rw_hi'], lw['rw_lo'], lw['rb'], tm, rows_per_group)
    if per_token:
        x1, h2, cmb = _merge(*merge_args, routed=False)
        y = _moe(h2, cmb, x1, gt2, lw['w1'], lw['w3'], lw['w2'], tm, rows_per_group)
    else:
        x1, hx, cls, hist = _merge(*merge_args, routed=True)
        n_tiles = SUBLANES * pl.cdiv(t // MOE_TILE_ROWS + N_CLASSES, SUBLANES)
        tm_rows = _tile_rows(rows_per_group, 1024)
        pos, tinfo = _route(cls, hist, MOE_TILE_ROWS, n_tiles, tm_rows)
        hx_sorted = _dispatch(pos, hx, n_tiles * MOE_TILE_ROWS, tm_rows)
        y_sorted = _moe_routed(hx_sorted, tinfo[:, 0], tinfo[:, 1], tinfo[:, 2],
                               lw['w1'], lw['w3'], lw['w2'], MOE_TILE_ROWS)
        y = _combine(pos, y_sorted, x1, gt2, tm_rows, rows_per_group)
    return (y.reshape(nb, seq, d), s_new, k2d.reshape(nb, seq, DIFF_HEADS, 2 * DIFF_DH),
            v2d.reshape(nb, seq, DIFF_HEADS, DIFF_DV))


def kernel(x_prompt, x_sample, c_prompt, c_sample, cache_diff_k, cache_diff_v, state_ret, w_ada, b_ada, norm1_g, w_in, diff_qnorm_g, diff_knorm_g, diff_lambda_q1, diff_lambda_k1, diff_lambda_q2, diff_lambda_k2, diff_subln_g, w_br_ret, w_br_diff, w_out, norm2_g, w_group, b_group, w_expert_router, b_expert, w1, w3, w2):
    nb, seq, d = x_prompt.shape
    nbs, seqs, _ = x_sample.shape
    depth = w_in.shape[0]
    past = cache_diff_k.shape[2]
    pos_p = jnp.arange(seq, dtype=jnp.int32)
    pos_s = past + jnp.arange(seqs, dtype=jnp.int32)
    c_all = jnp.concatenate([c_prompt, c_sample], axis=0)
    col = jnp.arange(MXU_COLS)
    bd = jnp.where((col[:, None] // DIFF_DH) == (col[None, :] // DIFF_DH), 1.0 / DIFF_DH, 0.0).astype(BF16)

    xp, xs = x_prompt, x_sample
    outs = [[] for _ in range(6)]
    for li in range(depth):
        rw = jnp.zeros((d, LANES), F32)
        rw = rw.at[:, :N_GROUPS].set(w_group[li]).at[:, N_GROUPS:N_GROUPS + N_EXPERTS].set(w_expert_router[li])
        rw_hi = rw.astype(BF16)
        rb = jnp.zeros((1, LANES), F32)
        rb = rb.at[0, :N_GROUPS].set(b_group[li]).at[0, N_GROUPS:N_GROUPS + N_EXPERTS].set(b_expert[li])
        lw = {
            'norm1_g': norm1_g[li].reshape(1, d), 'norm2_g': norm2_g[li].reshape(1, d),
            'w_in': jnp.stack([w_in[li, :, blk * COL_BLOCK:(blk + 1) * COL_BLOCK] for blk in IN_BLOCK_ORDER]
                              ).astype(BF16),
            'gq': diff_qnorm_g[li], 'gk': diff_knorm_g[li],
            'bd': bd,
            'lam': jnp.stack([diff_lambda_q1[li], diff_lambda_k1[li], diff_lambda_q2[li], diff_lambda_k2[li]]),
            'subln_g': diff_subln_g[li].reshape(1, DIFF_DV),
            'w_br_ret': w_br_ret[li].astype(BF16), 'w_br_diff': w_br_diff[li].astype(BF16),
            'w_out': w_out[li].astype(BF16),
            'rw_hi': rw_hi, 'rw_lo': (rw - rw_hi.astype(F32)).astype(BF16), 'rb': rb,
            'w1': w1[li].astype(BF16), 'w3': w3[li].astype(BF16), 'w2': w2[li].astype(BF16),
        }
        mod = _adaln(c_all, w_ada[li], b_ada[li]).reshape(nb + nbs, 6, d)
        xp, sp, kp, vp = _layer(xp, mod[:nb], pos_p, None, None, None, lw, li, ret_chunk=min(seq, 256))
        xs, ss, kss, vss = _layer(xs, mod[nb:], pos_s, state_ret[li], cache_diff_k, cache_diff_v,
                                  lw, li, ret_chunk=seqs)
        for lst, val in zip(outs, (kp, vp, sp, kss, vss, ss)):
            lst.append(val)
    kp, vp, sp, kss, vss, ss = (jnp.stack(lst) for lst in outs)
    return (xp, xs, kp, vp, sp.astype(x_prompt.dtype), kss, vss, ss.astype(state_ret.dtype))
```

```python
import functools
import math

import jax
import jax.numpy as jnp
from jax import lax
from jax.experimental import pallas as pl
from jax.experimental.pallas import tpu as pltpu

F32 = jnp.float32
BF16 = jnp.bfloat16

D_MODEL = 1024
CHUNK = 64
ROPE_THETA = 10000.0
EPS = 1e-6
RET_HEADS = 4
RET_DK = D_MODEL // RET_HEADS
RET_DV = 2 * RET_DK
DIFF_HEADS = 8
DIFF_DH = D_MODEL // (2 * DIFF_HEADS)
DIFF_DV = 2 * DIFF_DH
N_GROUPS = 4
EXPERTS_PER_GROUP = 4
N_EXPERTS = N_GROUPS * EXPERTS_PER_GROUP
D_EXPERT = 512
N_PAIRS = EXPERTS_PER_GROUP * (EXPERTS_PER_GROUP - 1) // 2
N_CLASSES = N_GROUPS * N_PAIRS
MOE_TILE_ROWS = 512
ATTN_ROW_PART = 256
MERGE_ROW_PARTS = 1
DMA_UNROLL = 8

LANES = 128
SUBLANES = 8
MXU_COLS = 256
COL_BLOCK = 1024
IN_BLOCK_ORDER = (0, 1, 2, 3, 4, 5, 6, 9, 10, 7, 8)
N_SLAB_BLOCKS = 8
STRIPS = 4
SLAB_RQ, SLAB_RK, SLAB_RV, SLAB_RG, SLAB_GATE = 0, 1, 2, 4, 6
VMEM_LIMIT = 56 * 1024 * 1024
LOG2E = math.log2(math.e)


def _cparams(*sem):
    return pltpu.CompilerParams(dimension_semantics=sem, vmem_limit_bytes=VMEM_LIMIT)


def _dot(a, b):
    return jnp.dot(a, b, preferred_element_type=F32)


def _dot_nt(a, b):
    return lax.dot_general(a, b, (((1,), (1,)), ((), ())), preferred_element_type=F32)


def _adaln_kernel(c_ref, w_ref, b_ref, o_ref):
    c = c_ref[...]
    a = c * jax.nn.sigmoid(c)
    o_ref[...] = jnp.dot(a, w_ref[...], preferred_element_type=F32,
                         precision=lax.Precision.HIGHEST) + b_ref[...]


def _adaln(c, w_ada, b_ada):
    rows, d = c.shape
    n = w_ada.shape[1]
    tn = 1536
    return pl.pallas_call(
        _adaln_kernel,
        out_shape=jax.ShapeDtypeStruct((rows, n), F32),
        grid=(n // tn,),
        in_specs=[pl.BlockSpec((rows, d), lambda j: (0, 0)),
                  pl.BlockSpec((d, tn), lambda j: (0, j)),
                  pl.BlockSpec((1, tn), lambda j: (0, j))],
        out_specs=pl.BlockSpec((rows, tn), lambda j: (0, j)),
        compiler_params=_cparams("arbitrary"),
        name="adaln",
    )(c, w_ada, b_ada.reshape(1, n))


def _rms(x):
    return x * lax.rsqrt(jnp.mean(x * x, axis=-1, keepdims=True) + EPS)


def _inproj_kernel(x_ref, sc_ref, sh_ref, g1_ref, w_ref, rc_ref, rs_ref, qc_ref, qsn_ref, qsp_ref,
                   kc_ref, ksn_ref, ksp_ref, bd_ref, p_ref, qd_ref, k_ref, v_ref, h_scr):
    j = pl.program_id(1)

    @pl.when(j == 0)
    def _():
        h = _rms(x_ref[...]) * g1_ref[...]
        h = h * (1.0 + sc_ref[...]) + sh_ref[...]
        h_scr[...] = h.astype(BF16)

    def strips():
        h = h_scr[...]
        starts = list(range(0, COL_BLOCK, MXU_COLS))
        ahead = _dot(h, w_ref[:, 0:MXU_COLS])
        for n, lo in enumerate(starts):
            acc = ahead
            if n + 1 < len(starts):
                ahead = _dot(h, w_ref[:, starts[n + 1]:starts[n + 1] + MXU_COLS])
            yield lo, acc

    def ret_rope(scale):
        c = rc_ref[...]
        s = rs_ref[...]
        for lo, acc in strips():
            x1 = acc[:, 0:LANES]
            x2 = acc[:, LANES:2 * LANES]
            p_ref[lo // MXU_COLS, :, 0:LANES] = ((x1 * c - x2 * s) * scale).astype(BF16)
            p_ref[lo // MXU_COLS, :, LANES:2 * LANES] = ((x1 * s + x2 * c) * scale).astype(BF16)

    def diff_norm():
        for lo, acc in list(strips()):
            k_ref[:, lo:lo + MXU_COLS] = acc * lax.rsqrt(_dot((acc * acc).astype(BF16), bd_ref[...]) + EPS)

    def diff_rope(lo, c_ref, sn_ref, sp_ref, out_ref):
        for off in range(lo, lo + MXU_COLS, LANES):
            yg = k_ref[:, off:off + LANES]
            o = (yg * c_ref[...] + pltpu.roll(yg, LANES - DIFF_DH // 2, 1) * sn_ref[...]
                 + pltpu.roll(yg, DIFF_DH // 2, 1) * sp_ref[...])
            out_ref[:, off:off + LANES] = o.astype(out_ref.dtype)

    def elementwise(fn, out_ref, after_strip=None):
        for lo, acc in strips():
            if out_ref is p_ref:
                p_ref[lo // MXU_COLS] = fn(acc).astype(BF16)
            else:
                out_ref[:, lo:lo + MXU_COLS] = fn(acc).astype(out_ref.dtype)
            if after_strip is not None:
                after_strip(lo)

    @pl.when(j == 0)
    def _():
        ret_rope(1.0)

    @pl.when(j == 1)
    def _():
        ret_rope(RET_DK ** -0.5)

    @pl.when((j == 2) | (j == 3))
    def _():
        elementwise(lambda a: a, p_ref)

    @pl.when((j == 4) | (j == 5))
    def _():
        elementwise(lambda a: a * jax.nn.sigmoid(a), p_ref)

    @pl.when((j == 6) | (j == 9))
    def _():
        diff_norm()

    @pl.when(j == 7)
    def _():
        elementwise(jax.nn.sigmoid, p_ref, lambda lo: diff_rope(lo, qc_ref, qsn_ref, qsp_ref, qd_ref))

    @pl.when(j == 8)
    def _():
        elementwise(jax.nn.sigmoid, p_ref)

    @pl.when(j == 10)
    def _():
        elementwise(lambda a: a, v_ref, lambda lo: diff_rope(lo, kc_ref, ksn_ref, ksp_ref, k_ref))


def _mod_spec(arr, tm, rows_per_group, ngrid):
    d = arr.shape[-1]
    if arr.ndim == 3:
        tiles_per_group = rows_per_group // tm
        if ngrid == 2:
            return pl.BlockSpec((None, 1, d), lambda i, j: (i // tiles_per_group, 0, 0))
        return pl.BlockSpec((None, 1, d), lambda i: (i // tiles_per_group, 0, 0))
    if ngrid == 2:
        return pl.BlockSpec((tm, d), lambda i, j: (i, 0))
    return pl.BlockSpec((tm, d), lambda i: (i, 0))


def _inproj(x2d, sc1, sh1, g1, w_blocks, tabs, bd, tm, rows_per_group):
    t, d = x2d.shape
    nrep = tabs[0].shape[0] // tm

    def slab_map(i, j):
        return (jnp.where(j < 6, j, jnp.where(j < 9, jnp.maximum(j - 1, 5), N_SLAB_BLOCKS - 1)), i, 0)

    tab_spec = pl.BlockSpec((tm, LANES), lambda i, j: (i % nrep, 0))
    row_spec = pl.BlockSpec((tm, d), lambda i, j: (i, 0))
    return pl.pallas_call(
        _inproj_kernel,
        out_shape=(jax.ShapeDtypeStruct((N_SLAB_BLOCKS * STRIPS, t, MXU_COLS), BF16),
                   jax.ShapeDtypeStruct((t, d), BF16),
                   jax.ShapeDtypeStruct((t, d), F32),
                   jax.ShapeDtypeStruct((t, d), F32)),
        grid=(t // tm, len(IN_BLOCK_ORDER)),
        in_specs=[row_spec,
                  _mod_spec(sc1, tm, rows_per_group, 2),
                  _mod_spec(sh1, tm, rows_per_group, 2),
                  pl.BlockSpec((1, d), lambda i, j: (0, 0)),
                  pl.BlockSpec((None, d, COL_BLOCK), lambda i, j: (j, 0, 0)),
                  *([tab_spec] * len(tabs)),
                  pl.BlockSpec((MXU_COLS, MXU_COLS), lambda i, j: (0, 0))],
        out_specs=(pl.BlockSpec((STRIPS, tm, MXU_COLS), slab_map), row_spec, row_spec, row_spec),
        scratch_shapes=[pltpu.VMEM((tm, d), BF16)],
        compiler_params=_cparams("arbitrary", "arbitrary"),
        name="inproj",
    )(x2d, sc1, sh1, g1, w_blocks, *tabs, bd)


def _retention_kernel(*refs, has_state, chunk):
    if has_state:
        q_ref, k_ref, v_ref, rg_ref, dm_ref, dv_ref, s0_ref, o_ref, sout_ref = refs
        s = s0_ref[...]
    else:
        q_ref, k_ref, v_ref, rg_ref, dm_ref, dv_ref, o_ref, sout_ref = refs
        s = jnp.zeros(sout_ref.shape, F32)
    dvec = dv_ref[...]
    q_decay = dvec[:, 0:1]
    k_decay = dvec[:, 1:2]
    chunk_decay = dvec[0:1, 2:3]

    def wide(ref, rows):
        return jnp.concatenate([ref[0, rows, :], ref[1, rows, :]], axis=1)

    for c in range(q_ref.shape[0] // chunk):
        rows = slice(c * chunk, (c + 1) * chunk)
        q = q_ref[rows, :]
        k = k_ref[rows, :]
        v = wide(v_ref, rows)
        scores = (_dot_nt(q, k) * dm_ref[...]).astype(BF16)
        o = _dot(scores, v) + _dot(q, s.astype(BF16)) * q_decay
        o = _rms(o) * wide(rg_ref, rows).astype(F32)
        o_ref[rows, :] = o.astype(BF16)
        kd_t = (k.astype(F32) * k_decay).T.astype(BF16)
        s = s * chunk_decay + _dot(kd_t, v)
    sout_ref[...] = s


def _retention(p, s0, nb, seq, chunk):
    t = p.shape[1]
    idx = jnp.arange(chunk, dtype=F32)
    log_gamma = jnp.log1p(-jnp.exp2(-5.0 - jnp.arange(RET_HEADS, dtype=F32)))
    rel = idx[:, None] - idx[None, :]
    dmat = jnp.where(rel >= 0, jnp.exp(jnp.maximum(rel, 0.0)[None] * log_gamma[:, None, None]), 0.0)
    q_decay = jnp.exp((idx + 1.0)[None, :] * log_gamma[:, None])
    k_decay = jnp.exp((chunk - 1.0 - idx)[None, :] * log_gamma[:, None])
    c_decay = jnp.broadcast_to(jnp.exp(chunk * log_gamma)[:, None], (RET_HEADS, chunk))
    dvec = jnp.zeros((RET_HEADS, chunk, LANES), F32)
    dvec = dvec.at[:, :, 0].set(q_decay).at[:, :, 1].set(k_decay).at[:, :, 2].set(c_decay)

    vstrips = RET_DV // MXU_COLS
    in_specs = [pl.BlockSpec((None, seq, RET_DK), lambda b, h: (SLAB_RQ * STRIPS + h, b, 0)),
                pl.BlockSpec((None, seq, RET_DK), lambda b, h: (SLAB_RK * STRIPS + h, b, 0)),
                pl.BlockSpec((vstrips, seq, MXU_COLS), lambda b, h: (SLAB_RV * STRIPS // vstrips + h, b, 0)),
                pl.BlockSpec((vstrips, seq, MXU_COLS), lambda b, h: (SLAB_RG * STRIPS // vstrips + h, b, 0)),
                pl.BlockSpec((None, chunk, chunk), lambda b, h: (h, 0, 0)),
                pl.BlockSpec((None, chunk, LANES), lambda b, h: (h, 0, 0))]
    args = [p, p, p, p, dmat, dvec]
    state_spec = pl.BlockSpec((None, None, RET_DK, RET_DV), lambda b, h: (b, h, 0, 0))
    if s0 is not None:
        in_specs.append(state_spec)
        args.append(s0)
    return pl.pallas_call(
        functools.partial(_retention_kernel, has_state=s0 is not None, chunk=chunk),
        out_shape=(jax.ShapeDtypeStruct((RET_HEADS, t, RET_DV), BF16),
                   jax.ShapeDtypeStruct((nb, RET_HEADS, RET_DK, RET_DV), F32)),
        grid=(nb, RET_HEADS),
        in_specs=in_specs,
        out_specs=(pl.BlockSpec((None, seq, RET_DV), lambda b, h: (h, b, 0)), state_spec),
        compiler_params=_cparams("arbitrary", "arbitrary"),
        name="retention",
    )(*args)


def _lambda(lam_ref, lam_init):
    l1 = jnp.sum(lam_ref[0:1, :] * lam_ref[1:2, :], axis=-1, keepdims=True)
    l2 = jnp.sum(lam_ref[2:3, :] * lam_ref[3:4, :], axis=-1, keepdims=True)
    return jnp.exp(l1) - jnp.exp(l2) + lam_init


def _split_maps(q):
    lane = lax.broadcasted_iota(jnp.int32, q.shape, 1)
    zero = jnp.zeros_like(q)
    return jnp.where(lane < DIFF_DH, q, zero), jnp.where(lane >= DIFF_DH, q, zero)


def _diffattn_kernel(q_ref, k_ref, v_ref, lam_ref, g_ref, o_ref, kb_scr, vb_scr, *, part, lam_init):
    kb_scr[...] = k_ref[...].astype(BF16)
    vb_scr[:, 0:DIFF_DV] = v_ref[...].astype(BF16)
    vb_scr[:, DIFF_DV:2 * DIFF_DV] = jnp.ones((k_ref.shape[0], DIFF_DV), BF16)
    lam = _lambda(lam_ref, lam_init)
    row = lax.broadcasted_iota(jnp.int32, (part, part), 0)
    col = lax.broadcasted_iota(jnp.int32, (part, part), 1)
    visible = (col // CHUNK) <= (row // CHUNK)

    def scores(q0):
        qmaps = _split_maps(q_ref[q0:q0 + part, :])
        kblk = kb_scr[0:q0 + part, :]
        out = []
        for mp in range(2):
            sc = _dot_nt(qmaps[mp], kblk)
            diag = jnp.where(visible, sc[:, q0:q0 + part], -1e30)
            out.append(jnp.concatenate([sc[:, 0:q0], diag], axis=1) if q0 else diag)
        return out

    starts = list(range(0, k_ref.shape[0], part))
    pending = scores(starts[0])
    for n, q0 in enumerate(starts):
        scs = pending
        if n + 1 < len(starts):
            pending = scores(starts[n + 1])
        vblk = vb_scr[0:q0 + part, :]
        att = []
        for mp in range(2):
            sc = scs[mp]
            pv = _dot(jnp.exp2(sc - jnp.max(sc, axis=-1, keepdims=True)).astype(BF16), vblk)
            att.append(pv[:, 0:DIFF_DV] / pv[:, DIFF_DV:2 * DIFF_DV])
        o = att[0] - lam * att[1]
        o = _rms(o) * g_ref[...] * (1.0 - lam_init)
        o_ref[q0:q0 + part, :] = o.astype(BF16)


def _diffattn(qd, k2d, v2d, lam_vecs, subln_g, nb, seq, lam_init):
    t = qd.shape[0]
    part = min(seq, ATTN_ROW_PART)
    assert seq % part == 0 and part % CHUNK == 0
    return pl.pallas_call(
        functools.partial(_diffattn_kernel, part=part, lam_init=lam_init),
        out_shape=jax.ShapeDtypeStruct((t, DIFF_HEADS * DIFF_DV), BF16),
        grid=(nb, DIFF_HEADS),
        in_specs=[pl.BlockSpec((seq, DIFF_DV), lambda b, h: (b, h)),
                  pl.BlockSpec((seq, DIFF_DV), lambda b, h: (b, h)),
                  pl.BlockSpec((seq, DIFF_DV), lambda b, h: (b, h)),
                  pl.BlockSpec((4, DIFF_DH), lambda b, h: (0, 0)),
                  pl.BlockSpec((1, DIFF_DV), lambda b, h: (0, 0))],
        out_specs=pl.BlockSpec((seq, DIFF_DV), lambda b, h: (b, h)),
        scratch_shapes=[pltpu.VMEM((seq, DIFF_DV), BF16), pltpu.VMEM((seq, 2 * DIFF_DV), BF16)],
        compiler_params=_cparams("arbitrary", "arbitrary"),
        name="diffattn",
    )(qd, k2d, v2d, lam_vecs, subln_g)


def _diffattn_cached_kernel(q_ref, kc_ref, vc_ref, kn_ref, vn_ref, lam_ref, g_ref, o_ref, k_scr, v_scr,
                            *, past, new, lam_init):
    total = k_scr.shape[0]
    pad = jnp.zeros((total - past, DIFF_DV), BF16)
    qpos = past + lax.broadcasted_iota(jnp.int32, (new, total), 0)
    kpos = lax.broadcasted_iota(jnp.int32, (new, total), 1)
    visible = ((kpos // CHUNK) <= (qpos // CHUNK)) & (kpos < past + new)
    lam = _lambda(lam_ref, lam_init)

    for h in range(DIFF_HEADS):
        cols = slice(h * DIFF_DV, (h + 1) * DIFF_DV)
        k_scr[0:past, :] = kc_ref[:, h, :].astype(BF16)
        v_scr[0:past, :] = vc_ref[:, h, :].astype(BF16)
        k_scr[past:total, :] = pad
        v_scr[past:total, :] = pad
        k_scr[past:past + new, :] = kn_ref[:, cols].astype(BF16)
        v_scr[past:past + new, :] = vn_ref[:, cols].astype(BF16)
        kall = k_scr[...]
        vall = v_scr[...]

        def softmax_av(qm):
            sc = jnp.where(visible, _dot_nt(qm, kall), -1e30)
            m = jnp.max(sc, axis=-1, keepdims=True)
            p = jnp.exp2(sc - m)
            return _dot(p.astype(BF16), vall) / jnp.sum(p, axis=-1, keepdims=True)

        qa, qb = _split_maps(q_ref[:, cols])
        o = softmax_av(qa) - lam * softmax_av(qb)
        o = _rms(o) * g_ref[...] * (1.0 - lam_init)
        o_ref[:, cols] = o.astype(BF16)


def _diffattn_cached(qd, k2d, v2d, kc, vc, li, lam_vecs, subln_g, nb, new, lam_init):
    t = qd.shape[0]
    past = kc.shape[2]
    width = DIFF_HEADS * DIFF_DV
    total = past + LANES * pl.cdiv(new, LANES)
    cache_spec = pl.BlockSpec((None, None, past, DIFF_HEADS, DIFF_DV), lambda b: (li, b, 0, 0, 0))
    return pl.pallas_call(
        functools.partial(_diffattn_cached_kernel, past=past, new=new, lam_init=lam_init),
        out_shape=jax.ShapeDtypeStruct((t, width), BF16),
        grid=(nb,),
        in_specs=[pl.BlockSpec((new, width), lambda b: (b, 0)),
                  cache_spec, cache_spec,
                  pl.BlockSpec((new, width), lambda b: (b, 0)),
                  pl.BlockSpec((new, width), lambda b: (b, 0)),
                  pl.BlockSpec((4, DIFF_DH), lambda b: (0, 0)),
                  pl.BlockSpec((1, DIFF_DV), lambda b: (0, 0))],
        out_specs=pl.BlockSpec((new, width), lambda b: (b, 0)),
        scratch_shapes=[pltpu.VMEM((total, DIFF_DV), BF16), pltpu.VMEM((total, DIFF_DV), BF16)],
        compiler_params=_cparams("arbitrary"),
        name="diffattn_cached",
    )(qd, kc, vc, k2d, v2d, lam_vecs, subln_g)


def _merge_kernel(ret_ref, dif_ref, gr_ref, gd_ref, x_ref, gt1_ref, sc2_ref, sh2_ref, g2_ref,
                  wr_ref, wd_ref, wo_ref, rwh_ref, rwl_ref, rb_ref, x1_ref, *out_refs, routed):
    tm, d = x_ref.shape
    n_parts = MERGE_ROW_PARTS if tm % (MERGE_ROW_PARTS * LANES) == 0 else 1
    part = tm // n_parts

    def mod_rows(ref, rows):
        return ref[rows, :] if ref.shape[0] > 1 else ref[...]

    hist = jnp.zeros((1, LANES), F32)
    for r in range(n_parts):
        rows = slice(r * part, (r + 1) * part)
        def lanes(ref):
            return jnp.concatenate([ref[n, rows, :] for n in range(ref.shape[0])], axis=1)

        a = _dot(lanes(ret_ref), wr_ref[...])
        b = _dot(dif_ref[rows, :], wd_ref[...])
        merged = lanes(gr_ref).astype(F32) * a + lanes(gd_ref).astype(F32) * b
        x1 = x_ref[rows, :] + mod_rows(gt1_ref, rows) * _dot(merged.astype(BF16), wo_ref[...])
        x1_ref[rows, :] = x1
        h2 = _rms(x1) * g2_ref[...]
        h2 = h2 * (1.0 + mod_rows(sc2_ref, rows)) + mod_rows(sh2_ref, rows)

        hi = h2.astype(BF16)
        lo = (h2 - hi.astype(F32)).astype(BF16)
        logits = _dot(hi, rwh_ref[...]) + _dot(lo, rwh_ref[...]) + _dot(hi, rwl_ref[...]) + rb_ref[...]

        lane = lax.broadcasted_iota(jnp.int32, logits.shape, 1)
        big = jnp.int32(4 * LANES)

        def masked_argmax(mask):
            vmax = jnp.max(jnp.where(mask, logits, -jnp.inf), axis=-1, keepdims=True)
            idx = jnp.min(jnp.where(mask & (logits == vmax), lane, big), axis=-1, keepdims=True)
            return vmax, idx

        is_group = lane < N_GROUPS
        gmax, gidx = masked_argmax(is_group)
        gprob = 1.0 / jnp.sum(jnp.where(is_group, jnp.exp(logits - gmax), 0.0), axis=-1, keepdims=True)
        first = N_GROUPS + gidx * EXPERTS_PER_GROUP
        in_group = (lane >= first) & (lane < first + EXPERTS_PER_GROUP)
        v1, i1 = masked_argmax(in_group)
        v2, i2 = masked_argmax(in_group & (lane != i1))
        e2 = jnp.exp(v2 - v1)
        w1 = gprob / (1.0 + e2)
        w2 = gprob * e2 / (1.0 + e2)
        if not routed:
            h2_ref, cmb_ref = out_refs
            h2_ref[rows, :] = h2.astype(BF16)
            cmb_ref[rows, :] = (jnp.where(lane + N_GROUPS == i1, w1, 0.0)
                                + jnp.where(lane + N_GROUPS == i2, w2, 0.0))
            continue

        hx_ref, cls_ref, hist_ref = out_refs
        first_is_lo = i1 < i2
        lo_e = jnp.minimum(i1, i2) - first
        hi_e = jnp.maximum(i1, i2) - first
        cls = gidx * N_PAIRS + jnp.right_shift(lo_e * (7 - lo_e), 1) + hi_e - lo_e - 1
        hx_ref[rows, 0:d] = h2
        hx_ref[rows, d:d + LANES] = jnp.where(lane == 0, jnp.where(first_is_lo, w1, w2),
                                              jnp.where(lane == 1, jnp.where(first_is_lo, w2, w1), 0.0))
        cls_ref[rows, :] = cls
        hist = hist + jnp.sum(jnp.where(lane == cls, 1.0, 0.0), axis=0, keepdims=True)

    if routed:
        hist_ref = out_refs[2]

        @pl.when(pl.program_id(0) == 0)
        def _():
            hist_ref[...] = jnp.zeros_like(hist_ref)

        hist_ref[...] += hist


def _merge(ret_o, dif_o, p, x2d, gt1, sc2, sh2, g2, wr, wd, wo, rwh, rwl, rb, tm, rows_per_group, routed):
    t, d = x2d.shape
    const = lambda shape: pl.BlockSpec(shape, lambda i: (0, 0), pipeline_mode=pl.Buffered(1))
    if routed:
        out_shape = (jax.ShapeDtypeStruct((t, d), F32),
                     jax.ShapeDtypeStruct((t, d + LANES), F32),
                     jax.ShapeDtypeStruct((t, 1), jnp.int32),
                     jax.ShapeDtypeStruct((1, LANES), F32))
        out_specs = (pl.BlockSpec((tm, d), lambda i: (i, 0)),
                     pl.BlockSpec((tm, d + LANES), lambda i: (i, 0)),
                     pl.BlockSpec((tm, 1), lambda i: (i, 0)),
                     pl.BlockSpec((1, LANES), lambda i: (0, 0)))
    else:
        out_shape = (jax.ShapeDtypeStruct((t, d), F32),
                     jax.ShapeDtypeStruct((t, d), BF16),
                     jax.ShapeDtypeStruct((t, LANES), F32))
        out_specs = (pl.BlockSpec((tm, d), lambda i: (i, 0)),
                     pl.BlockSpec((tm, d), lambda i: (i, 0)),
                     pl.BlockSpec((tm, LANES), lambda i: (i, 0)))
    return pl.pallas_call(
        functools.partial(_merge_kernel, routed=routed),
        out_shape=out_shape,
        grid=(t // tm,),
        in_specs=[pl.BlockSpec((RET_HEADS, tm, RET_DV), lambda i: (0, i, 0)),
                  pl.BlockSpec((tm, d), lambda i: (i, 0)),
                  pl.BlockSpec((STRIPS, tm, MXU_COLS), lambda i: (SLAB_GATE, i, 0)),
                  pl.BlockSpec((STRIPS, tm, MXU_COLS), lambda i: (SLAB_GATE + 1, i, 0)),
                  pl.BlockSpec((tm, d), lambda i: (i, 0)),
                  _mod_spec(gt1, tm, rows_per_group, 1),
                  _mod_spec(sc2, tm, rows_per_group, 1),
                  _mod_spec(sh2, tm, rows_per_group, 1),
                  const((1, d)),
                  const(wr.shape), const(wd.shape), const(wo.shape),
                  const(rwh.shape), const(rwl.shape), const(rb.shape)],
        out_specs=out_specs,
        compiler_params=_cparams("arbitrary"),
        name="merge",
    )(ret_o, dif_o, p, p, x2d, gt1, sc2, sh2, g2, wr, wd, wo, rwh, rwl, rb)


def _route_kernel(cls_ref, hist_ref, tri_ref, pos_ref, tinfo_ref, carry_scr, off_scr, *, tile_rows):
    i = pl.program_id(0)

    def excl_cumsum(v):
        lane8 = lax.broadcasted_iota(jnp.int32, v.shape, 1)
        inc = v
        for sh in (1, 2, 4, 8, 16, 32, 64):
            inc = inc + jnp.where(lane8 >= sh, pltpu.roll(inc, sh, 1), 0.0)
        return inc - v

    @pl.when(i == 0)
    def _():
        cnt = hist_ref[...]
        ntile = jnp.floor((cnt + (tile_rows - 1)) * (1.0 / tile_rows))
        start = excl_cumsum(jnp.broadcast_to(ntile, (SUBLANES, LANES)))[0:1, :]
        off_scr[...] = start * tile_rows
        carry_scr[...] = jnp.zeros_like(carry_scr)
        nt = tinfo_ref.shape[0]
        jrow = lax.broadcasted_iota(jnp.int32, (nt, LANES), 0).astype(F32)
        lane = lax.broadcasted_iota(jnp.int32, (nt, LANES), 1)
        is_cls = lane < N_CLASSES
        tcls = jnp.sum(jnp.where(is_cls & (start <= jrow), 1.0, 0.0), axis=-1, keepdims=True) - 1.0
        pick = lane == tcls.astype(jnp.int32)
        cnt_j = jnp.sum(jnp.where(pick, cnt, 0.0), axis=-1, keepdims=True)
        start_j = jnp.sum(jnp.where(pick, start, 0.0), axis=-1, keepdims=True)
        valid = jnp.clip(cnt_j - (jrow[:, 0:1] - start_j) * tile_rows, 0.0, float(tile_rows)).astype(jnp.int32)
        tc = tcls.astype(jnp.int32)
        grp = sum((tc >= g * N_PAIRS).astype(jnp.int32) for g in range(1, N_GROUPS))
        pair = tc - grp * N_PAIRS
        lo = (pair >= 3).astype(jnp.int32) + (pair >= 5).astype(jnp.int32)
        hi = pair - jnp.right_shift(lo * (7 - lo), 1) + lo + 1
        ea = grp * EXPERTS_PER_GROUP + lo
        eb = grp * EXPERTS_PER_GROUP + hi
        tinfo_ref[...] = jnp.where(lane == 0, ea, jnp.where(lane == 1, eb, jnp.where(lane == 2, valid, 0)))

    cls = cls_ref[...]
    lane = lax.broadcasted_iota(jnp.int32, (cls.shape[0], LANES), 1)
    onehot = lane == cls
    earlier = _dot(tri_ref[...], jnp.where(onehot, 1.0, 0.0).astype(BF16))
    base = carry_scr[...] + off_scr[...]
    pos = jnp.sum(jnp.where(onehot, earlier + base, 0.0), axis=-1, keepdims=True)
    pos_ref[...] = pos.astype(jnp.int32)
    carry_scr[...] += jnp.sum(jnp.where(onehot, 1.0, 0.0), axis=0, keepdims=True)


def _route(cls, hist, tile_rows, n_tiles, tm):
    t = cls.shape[0]
    tri = jnp.tril(jnp.ones((tm, tm), BF16), -1)
    return pl.pallas_call(
        functools.partial(_route_kernel, tile_rows=tile_rows),
        out_shape=(jax.ShapeDtypeStruct((t, 1), jnp.int32),
                   jax.ShapeDtypeStruct((n_tiles, LANES), jnp.int32)),
        grid=(t // tm,),
        in_specs=[pl.BlockSpec((tm, 1), lambda i: (i, 0)),
                  pl.BlockSpec((1, LANES), lambda i: (0, 0)),
                  pl.BlockSpec((tm, tm), lambda i: (0, 0))],
        out_specs=(pl.BlockSpec((tm, 1), lambda i: (i, 0)),
                   pl.BlockSpec((n_tiles, LANES), lambda i: (0, 0))),
        scratch_shapes=[pltpu.VMEM((1, LANES), F32), pltpu.VMEM((1, LANES), F32)],
        compiler_params=_cparams("arbitrary"),
        name="route",
    )(cls, hist, tri)


def _start_and_wait_rows(row_copy, n_rows):
    def start(g, carry):
        for u in range(DMA_UNROLL):
            row_copy(g * DMA_UNROLL + u).start(priority=u % 2)
        return carry

    def wait(r, carry):
        row_copy(r).wait()
        return carry

    lax.fori_loop(0, n_rows // DMA_UNROLL, start, 0)
    lax.fori_loop(0, n_rows, wait, 0, unroll=DMA_UNROLL)


def _dispatch_kernel(pos_ref, hx_ref, sorted_ref, sem):
    tm = hx_ref.shape[0]

    def row_copy(r):
        return pltpu.make_async_copy(hx_ref.at[pl.ds(r, 1)], sorted_ref.at[pl.ds(pos_ref[0, r], 1)], sem)

    _start_and_wait_rows(row_copy, tm)


def _dispatch(pos, hx, n_slots, tm):
    t, w = hx.shape
    return pl.pallas_call(
        _dispatch_kernel,
        out_shape=jax.ShapeDtypeStruct((n_slots, w), F32),
        grid=(t // tm,),
        in_specs=[pl.BlockSpec((None, 1, tm), lambda i: (i, 0, 0), memory_space=pltpu.SMEM),
                  pl.BlockSpec((tm, w), lambda i: (i, 0))],
        out_specs=pl.BlockSpec(memory_space=pl.ANY),
        scratch_shapes=[pltpu.SemaphoreType.DMA],
        compiler_params=pltpu.CompilerParams(dimension_semantics=("arbitrary",), vmem_limit_bytes=VMEM_LIMIT,
                                             has_side_effects=True),
        name="dispatch",
    )(pos.reshape(t // tm, 1, tm), hx)


def _moe_routed_kernel(ea_ref, eb_ref, valid_ref, hx_ref, w1a_ref, w3a_ref, w2a_ref, w1b_ref, w3b_ref, w2b_ref, y_ref):
    j = pl.program_id(0)
    nvalid = valid_ref[j]

    @pl.when(nvalid == 0)
    def _():
        y_ref[...] = jnp.zeros_like(y_ref)

    @pl.when(nvalid > 0)
    def _():
        tm, d = y_ref.shape
        row = lax.broadcasted_iota(jnp.int32, (tm, 1), 0)
        real = row < nvalid
        h = jnp.where(real, hx_ref[:, 0:d], 0.0).astype(BF16)
        wts = jnp.where(real, hx_ref[:, d:d + LANES], 0.0)

        def expert(w1_ref, w3_ref, w2_ref):
            a = _dot(h, w1_ref[...])
            g = (a * jax.nn.sigmoid(a)) * _dot(h, w3_ref[...])
            return _dot(g.astype(BF16), w2_ref[...])

        y_ref[...] = (wts[:, 0:1] * expert(w1a_ref, w3a_ref, w2a_ref)
                      + wts[:, 1:2] * expert(w1b_ref, w3b_ref, w2b_ref))


def _moe_routed(hx_sorted, ea, eb, valid, w1, w3, w2, tile_rows):
    n_slots, w = hx_sorted.shape
    ne, d, de = w1.shape
    up = lambda sel: pl.BlockSpec((None, d, de), lambda j, ea, eb, valid: ((ea, eb)[sel][j], 0, 0))
    down = lambda sel: pl.BlockSpec((None, de, d), lambda j, ea, eb, valid: ((ea, eb)[sel][j], 0, 0))
    grid_spec = pltpu.PrefetchScalarGridSpec(
        num_scalar_prefetch=3,
        grid=(n_slots // tile_rows,),
        in_specs=[pl.BlockSpec((tile_rows, w), lambda j, ea, eb, valid: (j, 0)),
                  up(0), up(0), down(0), up(1), up(1), down(1)],
        out_specs=pl.BlockSpec((tile_rows, d), lambda j, ea, eb, valid: (j, 0)),
    )
    return pl.pallas_call(
        _moe_routed_kernel,
        out_shape=jax.ShapeDtypeStruct((n_slots, d), F32),
        grid_spec=grid_spec,
        compiler_params=_cparams("arbitrary"),
        name="moe_routed",
    )(ea, eb, valid, hx_sorted, w1, w3, w2, w1, w3, w2)


def _combine_kernel(pos_ref, ys_ref, x1_ref, gt2_ref, y_ref, buf, sem):
    tm = buf.shape[0]

    def row_copy(r):
        return pltpu.make_async_copy(ys_ref.at[pl.ds(pos_ref[0, r], 1)], buf.at[pl.ds(r, 1)], sem)

    _start_and_wait_rows(row_copy, tm)
    y_ref[...] = x1_ref[...] + gt2_ref[...] * buf[...]


def _combine(pos, ys, x1, gt2, tm, rows_per_group):
    t, d = x1.shape
    return pl.pallas_call(
        _combine_kernel,
        out_shape=jax.ShapeDtypeStruct((t, d), F32),
        grid=(t // tm,),
        in_specs=[pl.BlockSpec((None, 1, tm), lambda i: (i, 0, 0), memory_space=pltpu.SMEM),
                  pl.BlockSpec(memory_space=pl.ANY),
                  pl.BlockSpec((tm, d), lambda i: (i, 0)),
                  _mod_spec(gt2, tm, rows_per_group, 1)],
        out_specs=pl.BlockSpec((tm, d), lambda i: (i, 0)),
        scratch_shapes=[pltpu.VMEM((tm, d), F32), pltpu.SemaphoreType.DMA],
        compiler_params=_cparams("arbitrary"),
        name="combine",
    )(pos.reshape(t // tm, 1, tm), ys, x1, gt2)


def _moe_kernel(h_ref, cmb_ref, x1_ref, gt2_ref, w1_ref, w3_ref, w2_ref, y_ref, acc_scr):
    e = pl.program_id(1)

    @pl.when(e == 0)
    def _():
        acc_scr[...] = jnp.zeros_like(acc_scr)

    h = h_ref[...]
    a = _dot(h, w1_ref[...])
    g = (a * jax.nn.sigmoid(a)) * _dot(h, w3_ref[...])
    ff = _dot(g.astype(BF16), w2_ref[...])
    cmb = cmb_ref[...]
    lane = lax.broadcasted_iota(jnp.int32, cmb.shape, 1)
    w = jnp.sum(jnp.where(lane == e, cmb, 0.0), axis=-1, keepdims=True)
    acc_scr[...] += w * ff

    @pl.when(e == pl.num_programs(1) - 1)
    def _():
        y_ref[...] = x1_ref[...] + gt2_ref[...] * acc_scr[...]


def _moe(h2, cmb, x1, gt2, w1, w3, w2, tm, rows_per_group):
    t, d = x1.shape
    ne, _, de = w1.shape
    gt2_spec = (pl.BlockSpec((None, 1, d), lambda i, e: (i // (rows_per_group // tm), 0, 0))
                if gt2.ndim == 3 else pl.BlockSpec((tm, d), lambda i, e: (i, 0)))
    return pl.pallas_call(
        _moe_kernel,
        out_shape=jax.ShapeDtypeStruct((t, d), F32),
        grid=(t // tm, ne),
        in_specs=[pl.BlockSpec((tm, d), lambda i, e: (i, 0)),
                  pl.BlockSpec((tm, LANES), lambda i, e: (i, 0)),
                  pl.BlockSpec((tm, d), lambda i, e: (i, 0)),
                  gt2_spec,
                  pl.BlockSpec((None, d, de), lambda i, e: (e, 0, 0)),
                  pl.BlockSpec((None, d, de), lambda i, e: (e, 0, 0)),
                  pl.BlockSpec((None, de, d), lambda i, e: (e, 0, 0))],
        out_specs=pl.BlockSpec((tm, d), lambda i, e: (i, 0)),
        scratch_shapes=[pltpu.VMEM((tm, d), F32)],
        compiler_params=_cparams("arbitrary", "arbitrary"),
        name="moe",
    )(h2, cmb, x1, gt2, w1, w3, w2)


def _rope_tables(pos, gq, gk):
    posf = pos.astype(F32)[:, None]
    inv_r = ROPE_THETA ** (-jnp.arange(0, RET_DK, 2, dtype=F32) / RET_DK)
    ang_r = posf * inv_r[None, :]
    inv_d = ROPE_THETA ** (-jnp.arange(0, DIFF_DH, 2, dtype=F32) / DIFF_DH)
    lane = jnp.arange(LANES)
    half = DIFF_DH // 2
    ang_d = posf * inv_d[lane % half][None, :]
    first_half = ((lane % DIFF_DH) < half)[None, :]
    cos_d, sin_d = jnp.cos(ang_d), jnp.sin(ang_d)

    def diff_tables(g, scale):
        g128 = jnp.tile(g, LANES // DIFF_DH)
        return (cos_d * (g128 * scale)[None, :],
                jnp.where(first_half, -sin_d, 0.0) * (jnp.roll(g128, -half) * scale)[None, :],
                jnp.where(first_half, 0.0, sin_d) * (jnp.roll(g128, half) * scale)[None, :])

    return (jnp.cos(ang_r), jnp.sin(ang_r)) + diff_tables(gq, DIFF_DH ** -0.5 * LOG2E) + diff_tables(gk, 1.0)


def _tile_rows(n, target):
    tm = min(n, target)
    assert n % tm == 0
    return tm


def _layer(x, mod, pos, s0, kc, vc, lw, li, ret_chunk):
    nb, seq, d = x.shape
    t = nb * seq
    x2d = x.reshape(t, d)
    per_token = seq < 128
    if per_token:
        mods = [jnp.repeat(mod[:, k, :], seq, axis=0) for k in range(6)]
        tabs = tuple(jnp.tile(tb, (nb, 1)) for tb in _rope_tables(pos, lw['gq'], lw['gk']))
        rows_per_group = t
    else:
        mods = [mod[:, k:k + 1, :] for k in range(6)]
        tabs = _rope_tables(pos, lw['gq'], lw['gk'])
        rows_per_group = seq
    sh1, sc1, gt1, sh2, sc2, gt2 = mods
    tm = _tile_rows(rows_per_group, 512)

    p, qd, k2d, v2d = _inproj(x2d, sc1, sh1, lw['norm1_g'], lw['w_in'], tabs, lw['bd'],
                              _tile_rows(rows_per_group, 1024), rows_per_group)
    lam_init = 0.8 - 0.6 * math.exp(-0.3 * li)
    if kc is None:
        dif_o = _diffattn(qd, k2d, v2d, lw['lam'], lw['subln_g'], nb, seq, lam_init)
    else:
        dif_o = _diffattn_cached(qd, k2d, v2d, kc, vc, li, lw['lam'], lw['subln_g'], nb, seq, lam_init)
    ret_o, s_new = _retention(p, s0, nb, seq, ret_chunk)
    ret_o, k2d, v2d = lax.optimization_barrier((ret_o, k2d, v2d))
    merge_args = (ret_o, dif_o, p, x2d, gt1, sc2, sh2, lw['norm2_g'], lw['w_br_ret'], lw['w_br_diff'],
                  lw['w_out'], lw['rw_hi'], lw['rw_lo'], lw['rb'], tm, rows_per_group)
    if per_token:
        x1, h2, cmb = _merge(*merge_args, routed=False)
        y = _moe(h2, cmb, x1, gt2, lw['w1'], lw['w3'], lw['w2'], tm, rows_per_group)
    else:
        x1, hx, cls, hist = _merge(*merge_args, routed=True)
        n_tiles = SUBLANES * pl.cdiv(t // MOE_TILE_ROWS + N_CLASSES, SUBLANES)
        tm_rows = _tile_rows(rows_per_group, 2048)
        pos, tinfo = _route(cls, hist, MOE_TILE_ROWS, n_tiles, tm_rows)
        hx_sorted = _dispatch(pos, hx, n_tiles * MOE_TILE_ROWS, tm_rows)
        y_sorted = _moe_routed(hx_sorted, tinfo[:, 0], tinfo[:, 1], tinfo[:, 2],
                               lw['w1'], lw['w3'], lw['w2'], MOE_TILE_ROWS)
        y = _combine(pos, y_sorted, x1, gt2, tm_rows, rows_per_group)
    return (y.reshape(nb, seq, d), s_new, k2d.reshape(nb, seq, DIFF_HEADS, 2 * DIFF_DH),
            v2d.reshape(nb, seq, DIFF_HEADS, DIFF_DV))


def kernel(x_prompt, x_sample, c_prompt, c_sample, cache_diff_k, cache_diff_v, state_ret, w_ada, b_ada, norm1_g, w_in, diff_qnorm_g, diff_knorm_g, diff_lambda_q1, diff_lambda_k1, diff_lambda_q2, diff_lambda_k2, diff_subln_g, w_br_ret, w_br_diff, w_out, norm2_g, w_group, b_group, w_expert_router, b_expert, w1, w3, w2):
    nb, seq, d = x_prompt.shape
    nbs, seqs, _ = x_sample.shape
    depth = w_in.shape[0]
    past = cache_diff_k.shape[2]
    pos_p = jnp.arange(seq, dtype=jnp.int32)
    pos_s = past + jnp.arange(seqs, dtype=jnp.int32)
    c_all = jnp.concatenate([c_prompt, c_sample], axis=0)
    col = jnp.arange(MXU_COLS)
    bd = jnp.where((col[:, None] // DIFF_DH) == (col[None, :] // DIFF_DH), 1.0 / DIFF_DH, 0.0).astype(BF16)

    xp, xs = x_prompt, x_sample
    outs = [[] for _ in range(6)]
    for li in range(depth):
        rw = jnp.zeros((d, LANES), F32)
        rw = rw.at[:, :N_GROUPS].set(w_group[li]).at[:, N_GROUPS:N_GROUPS + N_EXPERTS].set(w_expert_router[li])
        rw_hi = rw.astype(BF16)
        rb = jnp.zeros((1, LANES), F32)
        rb = rb.at[0, :N_GROUPS].set(b_group[li]).at[0, N_GROUPS:N_GROUPS + N_EXPERTS].set(b_expert[li])
        lw = {
            'norm1_g': norm1_g[li].reshape(1, d), 'norm2_g': norm2_g[li].reshape(1, d),
            'w_in': jnp.stack([w_in[li, :, blk * COL_BLOCK:(blk + 1) * COL_BLOCK] for blk in IN_BLOCK_ORDER]
                              ).astype(BF16),
            'gq': diff_qnorm_g[li], 'gk': diff_knorm_g[li],
            'bd': bd,
            'lam': jnp.stack([diff_lambda_q1[li], diff_lambda_k1[li], diff_lambda_q2[li], diff_lambda_k2[li]]),
            'subln_g': diff_subln_g[li].reshape(1, DIFF_DV),
            'w_br_ret': w_br_ret[li].astype(BF16), 'w_br_diff': w_br_diff[li].astype(BF16),
            'w_out': w_out[li].astype(BF16),
            'rw_hi': rw_hi, 'rw_lo': (rw - rw_hi.astype(F32)).astype(BF16), 'rb': rb,
            'w1': w1[li].astype(BF16), 'w3': w3[li].astype(BF16), 'w2': w2[li].astype(BF16),
        }
        mod = _adaln(c_all, w_ada[li], b_ada[li]).reshape(nb + nbs, 6, d)
        xp, sp, kp, vp = _layer(xp, mod[:nb], pos_p, None, None, None, lw, li, ret_chunk=min(seq, 256))
        xs, ss, kss, vss = _layer(xs, mod[nb:], pos_s, state_ret[li], cache_diff_k, cache_diff_v,
                                  lw, li, ret_chunk=seqs)
        for lst, val in zip(outs, (kp, vp, sp, kss, vss, ss)):
            lst.append(val)
    kp, vp, sp, kss, vss, ss = (jnp.stack(lst) for lst in outs)
    return (xp, xs, kp, vp, sp.astype(x_prompt.dtype), kss, vss, ss.astype(state_ret.dtype))
```

```python
import functools
import math

import jax
import jax.numpy as jnp
from jax import lax
from jax.experimental import pallas as pl
from jax.experimental.pallas import tpu as pltpu

F32 = jnp.float32
BF16 = jnp.bfloat16

D_MODEL = 1024
CHUNK = 64
ROPE_THETA = 10000.0
EPS = 1e-6
RET_HEADS = 4
RET_DK = D_MODEL // RET_HEADS
RET_DV = 2 * RET_DK
DIFF_HEADS = 8
DIFF_DH = D_MODEL // (2 * DIFF_HEADS)
DIFF_DV = 2 * DIFF_DH
N_GROUPS = 4
EXPERTS_PER_GROUP = 4
N_EXPERTS = N_GROUPS * EXPERTS_PER_GROUP
N_PAIRS = EXPERTS_PER_GROUP * (EXPERTS_PER_GROUP - 1) // 2
N_CLASSES = N_GROUPS * N_PAIRS
MOE_TILE_ROWS = 512
ATTN_ROW_PART = 256
MERGE_ROW_PARTS = 1
DMA_UNROLL = 8

LANES = 128
SUBLANES = 8
MXU_COLS = 256
COL_BLOCK = 1024
IN_BLOCK_ORDER = (0, 1, 2, 3, 4, 5, 6, 9, 10, 7, 8)
N_SLAB_BLOCKS = 8
STRIPS = 4
SLAB_RQ, SLAB_RK, SLAB_RV, SLAB_RG, SLAB_GATE = 0, 1, 2, 4, 6
VMEM_LIMIT = 56 * 1024 * 1024
LOG2E = math.log2(math.e)


def _cparams(*sem):
    return pltpu.CompilerParams(dimension_semantics=sem, vmem_limit_bytes=VMEM_LIMIT)


def _dot(a, b):
    return jnp.dot(a, b, preferred_element_type=F32)


def _dot_nt(a, b):
    return lax.dot_general(a, b, (((1,), (1,)), ((), ())), preferred_element_type=F32)


def _adaln_kernel(c_ref, w_ref, b_ref, o_ref):
    c = c_ref[...]
    a = c * jax.nn.sigmoid(c)
    o_ref[...] = jnp.dot(a, w_ref[...], preferred_element_type=F32,
                         precision=lax.Precision.HIGHEST) + b_ref[...]


def _adaln(c, w_ada, b_ada):
    rows, d = c.shape
    n = w_ada.shape[1]
    tn = 1536
    return pl.pallas_call(
        _adaln_kernel,
        out_shape=jax.ShapeDtypeStruct((rows, n), F32),
        grid=(n // tn,),
        in_specs=[pl.BlockSpec((rows, d), lambda j: (0, 0)),
                  pl.BlockSpec((d, tn), lambda j: (0, j)),
                  pl.BlockSpec((1, tn), lambda j: (0, j))],
        out_specs=pl.BlockSpec((rows, tn), lambda j: (0, j)),
        compiler_params=_cparams("arbitrary"),
        name="adaln",
    )(c, w_ada, b_ada.reshape(1, n))


def _rms(x):
    return x * lax.rsqrt(jnp.mean(x * x, axis=-1, keepdims=True) + EPS)


def _inproj_kernel(x_ref, sc_ref, sh_ref, g1_ref, w_ref, rc_ref, rs_ref, qc_ref, qsn_ref, qsp_ref,
                   kc_ref, ksn_ref, ksp_ref, bd_ref, p_ref, qd_ref, k_ref, v_ref, h_scr):
    j = pl.program_id(1)

    @pl.when(j == 0)
    def _():
        h = _rms(x_ref[...]) * g1_ref[...]
        h = h * (1.0 + sc_ref[...]) + sh_ref[...]
        h_scr[...] = h.astype(BF16)

    def strips():
        h = h_scr[...]
        starts = list(range(0, COL_BLOCK, MXU_COLS))
        ahead = _dot(h, w_ref[:, 0:MXU_COLS])
        for n, lo in enumerate(starts):
            acc = ahead
            if n + 1 < len(starts):
                ahead = _dot(h, w_ref[:, starts[n + 1]:starts[n + 1] + MXU_COLS])
            yield lo, acc

    def ret_rope(scale):
        c = rc_ref[...]
        s = rs_ref[...]
        for lo, acc in strips():
            x1 = acc[:, 0:LANES]
            x2 = acc[:, LANES:2 * LANES]
            p_ref[lo // MXU_COLS, :, 0:LANES] = ((x1 * c - x2 * s) * scale).astype(BF16)
            p_ref[lo // MXU_COLS, :, LANES:2 * LANES] = ((x1 * s + x2 * c) * scale).astype(BF16)

    def diff_norm():
        for lo, acc in list(strips()):
            k_ref[:, lo:lo + MXU_COLS] = acc * lax.rsqrt(_dot((acc * acc).astype(BF16), bd_ref[...]) + EPS)

    def diff_rope(lo, c_ref, sn_ref, sp_ref, out_ref):
        for off in range(lo, lo + MXU_COLS, LANES):
            yg = k_ref[:, off:off + LANES]
            o = (yg * c_ref[...] + pltpu.roll(yg, LANES - DIFF_DH // 2, 1) * sn_ref[...]
                 + pltpu.roll(yg, DIFF_DH // 2, 1) * sp_ref[...])
            out_ref[:, off:off + LANES] = o.astype(out_ref.dtype)

    def elementwise(fn, out_ref, after_strip=None):
        for lo, acc in strips():
            if out_ref is p_ref:
                p_ref[lo // MXU_COLS] = fn(acc).astype(BF16)
            else:
                out_ref[:, lo:lo + MXU_COLS] = fn(acc).astype(out_ref.dtype)
            if after_strip is not None:
                after_strip(lo)

    @pl.when(j == 0)
    def _():
        ret_rope(1.0)

    @pl.when(j == 1)
    def _():
        ret_rope(RET_DK ** -0.5)

    @pl.when((j == 2) | (j == 3))
    def _():
        elementwise(lambda a: a, p_ref)

    @pl.when((j == 4) | (j == 5))
    def _():
        elementwise(lambda a: a * jax.nn.sigmoid(a), p_ref)

    @pl.when((j == 6) | (j == 9))
    def _():
        diff_norm()

    @pl.when(j == 7)
    def _():
        elementwise(jax.nn.sigmoid, p_ref, lambda lo: diff_rope(lo, qc_ref, qsn_ref, qsp_ref, qd_ref))

    @pl.when(j == 8)
    def _():
        elementwise(jax.nn.sigmoid, p_ref)

    @pl.when(j == 10)
    def _():
        elementwise(lambda a: a, v_ref, lambda lo: diff_rope(lo, kc_ref, ksn_ref, ksp_ref, k_ref))


def _mod_spec(arr, tm, rows_per_group, ngrid):
    d = arr.shape[-1]
    if arr.ndim == 3:
        tiles_per_group = rows_per_group // tm
        if ngrid == 2:
            return pl.BlockSpec((None, 1, d), lambda i, j: (i // tiles_per_group, 0, 0))
        return pl.BlockSpec((None, 1, d), lambda i: (i // tiles_per_group, 0, 0))
    if ngrid == 2:
        return pl.BlockSpec((tm, d), lambda i, j: (i, 0))
    return pl.BlockSpec((tm, d), lambda i: (i, 0))


def _inproj(x2d, sc1, sh1, g1, w_blocks, tabs, bd, tm, rows_per_group):
    t, d = x2d.shape
    nrep = tabs[0].shape[0] // tm

    def slab_map(i, j):
        return (jnp.where(j < 6, j, jnp.where(j < 9, jnp.maximum(j - 1, 5), N_SLAB_BLOCKS - 1)), i, 0)

    tab_spec = pl.BlockSpec((tm, LANES), lambda i, j: (i % nrep, 0))
    row_spec = pl.BlockSpec((tm, d), lambda i, j: (i, 0))
    return pl.pallas_call(
        _inproj_kernel,
        out_shape=(jax.ShapeDtypeStruct((N_SLAB_BLOCKS * STRIPS, t, MXU_COLS), BF16),
                   jax.ShapeDtypeStruct((t, d), BF16),
                   jax.ShapeDtypeStruct((t, d), F32),
                   jax.ShapeDtypeStruct((t, d), F32)),
        grid=(t // tm, len(IN_BLOCK_ORDER)),
        in_specs=[row_spec,
                  _mod_spec(sc1, tm, rows_per_group, 2),
                  _mod_spec(sh1, tm, rows_per_group, 2),
                  pl.BlockSpec((1, d), lambda i, j: (0, 0)),
                  pl.BlockSpec((None, d, COL_BLOCK), lambda i, j: (j, 0, 0)),
                  *([tab_spec] * len(tabs)),
                  pl.BlockSpec((MXU_COLS, MXU_COLS), lambda i, j: (0, 0))],
        out_specs=(pl.BlockSpec((STRIPS, tm, MXU_COLS), slab_map), row_spec, row_spec, row_spec),
        scratch_shapes=[pltpu.VMEM((tm, d), BF16)],
        compiler_params=_cparams("arbitrary", "arbitrary"),
        name="inproj",
    )(x2d, sc1, sh1, g1, w_blocks, *tabs, bd)


def _retention_kernel(*refs, has_state, chunk):
    if has_state:
        q_ref, k_ref, v_ref, rg_ref, dm_ref, dv_ref, s0_ref, o_ref, sout_ref = refs
        s = s0_ref[...]
    else:
        q_ref, k_ref, v_ref, rg_ref, dm_ref, dv_ref, o_ref, sout_ref = refs
        s = jnp.zeros(sout_ref.shape, F32)
    dvec = dv_ref[...]
    q_decay = dvec[:, 0:1]
    k_decay = dvec[:, 1:2]
    chunk_decay = dvec[0:1, 2:3]

    def wide(ref, rows):
        return jnp.concatenate([ref[0, rows, :], ref[1, rows, :]], axis=1)

    for c in range(q_ref.shape[0] // chunk):
        rows = slice(c * chunk, (c + 1) * chunk)
        q = q_ref[rows, :]
        k = k_ref[rows, :]
        v = wide(v_ref, rows)
        scores = (_dot_nt(q, k) * dm_ref[...]).astype(BF16)
        o = _dot(scores, v) + _dot(q, s.astype(BF16)) * q_decay
        o = _rms(o) * wide(rg_ref, rows).astype(F32)
        o_ref[rows, :] = o.astype(BF16)
        kd_t = (k.astype(F32) * k_decay).T.astype(BF16)
        s = s * chunk_decay + _dot(kd_t, v)
    sout_ref[...] = s


def _retention(p, s0, nb, seq, chunk):
    t = p.shape[1]
    idx = jnp.arange(chunk, dtype=F32)
    log_gamma = jnp.log1p(-jnp.exp2(-5.0 - jnp.arange(RET_HEADS, dtype=F32)))
    rel = idx[:, None] - idx[None, :]
    dmat = jnp.where(rel >= 0, jnp.exp(jnp.maximum(rel, 0.0)[None] * log_gamma[:, None, None]), 0.0)
    q_decay = jnp.exp((idx + 1.0)[None, :] * log_gamma[:, None])
    k_decay = jnp.exp((chunk - 1.0 - idx)[None, :] * log_gamma[:, None])
    c_decay = jnp.broadcast_to(jnp.exp(chunk * log_gamma)[:, None], (RET_HEADS, chunk))
    dvec = jnp.zeros((RET_HEADS, chunk, LANES), F32)
    dvec = dvec.at[:, :, 0].set(q_decay).at[:, :, 1].set(k_decay).at[:, :, 2].set(c_decay)

    vstrips = RET_DV // MXU_COLS
    in_specs = [pl.BlockSpec((None, seq, RET_DK), lambda b, h: (SLAB_RQ * STRIPS + h, b, 0)),
                pl.BlockSpec((None, seq, RET_DK), lambda b, h: (SLAB_RK * STRIPS + h, b, 0)),
                pl.BlockSpec((vstrips, seq, MXU_COLS), lambda b, h: (SLAB_RV * STRIPS // vstrips + h, b, 0)),
                pl.BlockSpec((vstrips, seq, MXU_COLS), lambda b, h: (SLAB_RG * STRIPS // vstrips + h, b, 0)),
                pl.BlockSpec((None, chunk, chunk), lambda b, h: (h, 0, 0)),
                pl.BlockSpec((None, chunk, LANES), lambda b, h: (h, 0, 0))]
    args = [p, p, p, p, dmat, dvec]
    state_spec = pl.BlockSpec((None, None, RET_DK, RET_DV), lambda b, h: (b, h, 0, 0))
    if s0 is not None:
        in_specs.append(state_spec)
        args.append(s0)
    return pl.pallas_call(
        functools.partial(_retention_kernel, has_state=s0 is not None, chunk=chunk),
        out_shape=(jax.ShapeDtypeStruct((RET_HEADS, t, RET_DV), BF16),
                   jax.ShapeDtypeStruct((nb, RET_HEADS, RET_DK, RET_DV), F32)),
        grid=(nb, RET_HEADS),
        in_specs=in_specs,
        out_specs=(pl.BlockSpec((None, seq, RET_DV), lambda b, h: (h, b, 0)), state_spec),
        compiler_params=_cparams("arbitrary", "arbitrary"),
        name="retention",
    )(*args)


def _lambda(lam_ref, lam_init):
    l1 = jnp.sum(lam_ref[0:1, :] * lam_ref[1:2, :], axis=-1, keepdims=True)
    l2 = jnp.sum(lam_ref[2:3, :] * lam_ref[3:4, :], axis=-1, keepdims=True)
    return jnp.exp(l1) - jnp.exp(l2) + lam_init


def _split_maps(q):
    lane = lax.broadcasted_iota(jnp.int32, q.shape, 1)
    zero = jnp.zeros_like(q)
    return jnp.where(lane < DIFF_DH, q, zero), jnp.where(lane >= DIFF_DH, q, zero)


def _diffattn_kernel(q_ref, k_ref, v_ref, lam_ref, g_ref, o_ref, kb_scr, vb_scr, *, part, lam_init):
    kb_scr[...] = k_ref[...].astype(BF16)
    vb_scr[:, 0:DIFF_DV] = v_ref[...].astype(BF16)
    vb_scr[:, DIFF_DV:2 * DIFF_DV] = jnp.ones((k_ref.shape[0], DIFF_DV), BF16)
    lam = _lambda(lam_ref, lam_init)
    row = lax.broadcasted_iota(jnp.int32, (part, part), 0)
    col = lax.broadcasted_iota(jnp.int32, (part, part), 1)
    visible = (col // CHUNK) <= (row // CHUNK)

    def scores(q0):
        qmaps = _split_maps(q_ref[q0:q0 + part, :])
        kblk = kb_scr[0:q0 + part, :]
        out = []
        for mp in range(2):
            sc = _dot_nt(qmaps[mp], kblk)
            diag = jnp.where(visible, sc[:, q0:q0 + part], -1e30)
            out.append(jnp.concatenate([sc[:, 0:q0], diag], axis=1) if q0 else diag)
        return out

    starts = list(range(0, k_ref.shape[0], part))
    pending = scores(starts[0])
    for n, q0 in enumerate(starts):
        scs = pending
        if n + 1 < len(starts):
            pending = scores(starts[n + 1])
        vblk = vb_scr[0:q0 + part, :]
        att = []
        for mp in range(2):
            sc = scs[mp]
            pv = _dot(jnp.exp2(sc - jnp.max(sc, axis=-1, keepdims=True)).astype(BF16), vblk)
            att.append(pv[:, 0:DIFF_DV] / pv[:, DIFF_DV:2 * DIFF_DV])
        o = att[0] - lam * att[1]
        o = _rms(o) * g_ref[...] * (1.0 - lam_init)
        o_ref[q0:q0 + part, :] = o.astype(BF16)


def _diffattn(qd, k2d, v2d, lam_vecs, subln_g, nb, seq, lam_init):
    t = qd.shape[0]
    part = min(seq, ATTN_ROW_PART)
    assert seq % part == 0 and part % CHUNK == 0
    return pl.pallas_call(
        functools.partial(_diffattn_kernel, part=part, lam_init=lam_init),
        out_shape=jax.ShapeDtypeStruct((t, DIFF_HEADS * DIFF_DV), BF16),
        grid=(nb, DIFF_HEADS),
        in_specs=[pl.BlockSpec((seq, DIFF_DV), lambda b, h: (b, h)),
                  pl.BlockSpec((seq, DIFF_DV), lambda b, h: (b, h)),
                  pl.BlockSpec((seq, DIFF_DV), lambda b, h: (b, h)),
                  pl.BlockSpec((4, DIFF_DH), lambda b, h: (0, 0)),
                  pl.BlockSpec((1, DIFF_DV), lambda b, h: (0, 0))],
        out_specs=pl.BlockSpec((seq, DIFF_DV), lambda b, h: (b, h)),
        scratch_shapes=[pltpu.VMEM((seq, DIFF_DV), BF16), pltpu.VMEM((seq, 2 * DIFF_DV), BF16)],
        compiler_params=_cparams("arbitrary", "arbitrary"),
        name="diffattn",
    )(qd, k2d, v2d, lam_vecs, subln_g)


def _diffattn_cached_kernel(q_ref, kc_ref, vc_ref, kn_ref, vn_ref, lam_ref, g_ref, o_ref, k_scr, v_scr,
                            *, past, new, lam_init):
    total = k_scr.shape[0]
    pad = jnp.zeros((total - past, DIFF_DV), BF16)
    qpos = past + lax.broadcasted_iota(jnp.int32, (new, total), 0)
    kpos = lax.broadcasted_iota(jnp.int32, (new, total), 1)
    visible = ((kpos // CHUNK) <= (qpos // CHUNK)) & (kpos < past + new)
    lam = _lambda(lam_ref, lam_init)

    for h in range(DIFF_HEADS):
        cols = slice(h * DIFF_DV, (h + 1) * DIFF_DV)
        k_scr[0:past, :] = kc_ref[:, h, :].astype(BF16)
        v_scr[0:past, :] = vc_ref[:, h, :].astype(BF16)
        k_scr[past:total, :] = pad
        v_scr[past:total, :] = pad
        k_scr[past:past + new, :] = kn_ref[:, cols].astype(BF16)
        v_scr[past:past + new, :] = vn_ref[:, cols].astype(BF16)
        kall = k_scr[...]
        vall = v_scr[...]

        def softmax_av(qm):
            sc = jnp.where(visible, _dot_nt(qm, kall), -1e30)
            m = jnp.max(sc, axis=-1, keepdims=True)
            p = jnp.exp2(sc - m)
            return _dot(p.astype(BF16), vall) / jnp.sum(p, axis=-1, keepdims=True)

        qa, qb = _split_maps(q_ref[:, cols])
        o = softmax_av(qa) - lam * softmax_av(qb)
        o = _rms(o) * g_ref[...] * (1.0 - lam_init)
        o_ref[:, cols] = o.astype(BF16)


def _diffattn_cached(qd, k2d, v2d, kc, vc, li, lam_vecs, subln_g, nb, new, lam_init):
    t = qd.shape[0]
    past = kc.shape[2]
    width = DIFF_HEADS * DIFF_DV
    total = past + LANES * pl.cdiv(new, LANES)
    cache_spec = pl.BlockSpec((None, None, past, DIFF_HEADS, DIFF_DV), lambda b: (li, b, 0, 0, 0))
    return pl.pallas_call(
        functools.partial(_diffattn_cached_kernel, past=past, new=new, lam_init=lam_init),
        out_shape=jax.ShapeDtypeStruct((t, width), BF16),
        grid=(nb,),
        in_specs=[pl.BlockSpec((new, width), lambda b: (b, 0)),
                  cache_spec, cache_spec,
                  pl.BlockSpec((new, width), lambda b: (b, 0)),
                  pl.BlockSpec((new, width), lambda b: (b, 0)),
                  pl.BlockSpec((4, DIFF_DH), lambda b: (0, 0)),
                  pl.BlockSpec((1, DIFF_DV), lambda b: (0, 0))],
        out_specs=pl.BlockSpec((new, width), lambda b: (b, 0)),
        scratch_shapes=[pltpu.VMEM((total, DIFF_DV), BF16), pltpu.VMEM((total, DIFF_DV), BF16)],
        compiler_params=_cparams("arbitrary"),
        name="diffattn_cached",
    )(qd, kc, vc, k2d, v2d, lam_vecs, subln_g)


def _merge_kernel(ret_ref, dif_ref, gr_ref, gd_ref, x_ref, gt1_ref, sc2_ref, sh2_ref, g2_ref,
                  wr_ref, wd_ref, wo_ref, rwh_ref, rwl_ref, rb_ref, x1_ref, *out_refs, routed):
    tm, d = x_ref.shape
    n_parts = MERGE_ROW_PARTS if tm % (MERGE_ROW_PARTS * LANES) == 0 else 1
    part = tm // n_parts

    def mod_rows(ref, rows):
        return ref[rows, :] if ref.shape[0] > 1 else ref[...]

    hist = jnp.zeros((1, LANES), F32)
    for r in range(n_parts):
        rows = slice(r * part, (r + 1) * part)
        def lanes(ref):
            return jnp.concatenate([ref[n, rows, :] for n in range(ref.shape[0])], axis=1)

        a = _dot(lanes(ret_ref), wr_ref[...])
        b = _dot(dif_ref[rows, :], wd_ref[...])
        merged = lanes(gr_ref).astype(F32) * a + lanes(gd_ref).astype(F32) * b
        x1 = x_ref[rows, :] + mod_rows(gt1_ref, rows) * _dot(merged.astype(BF16), wo_ref[...])
        x1_ref[rows, :] = x1
        h2 = _rms(x1) * g2_ref[...]
        h2 = h2 * (1.0 + mod_rows(sc2_ref, rows)) + mod_rows(sh2_ref, rows)

        hi = h2.astype(BF16)
        lo = (h2 - hi.astype(F32)).astype(BF16)
        logits = _dot(hi, rwh_ref[...]) + _dot(lo, rwh_ref[...]) + _dot(hi, rwl_ref[...]) + rb_ref[...]

        lane = lax.broadcasted_iota(jnp.int32, logits.shape, 1)
        big = jnp.int32(4 * LANES)

        def masked_argmax(mask):
            vmax = jnp.max(jnp.where(mask, logits, -jnp.inf), axis=-1, keepdims=True)
            idx = jnp.min(jnp.where(mask & (logits == vmax), lane, big), axis=-1, keepdims=True)
            return vmax, idx

        is_group = lane < N_GROUPS
        gmax, gidx = masked_argmax(is_group)
        gprob = 1.0 / jnp.sum(jnp.where(is_group, jnp.exp(logits - gmax), 0.0), axis=-1, keepdims=True)
        first = N_GROUPS + gidx * EXPERTS_PER_GROUP
        in_group = (lane >= first) & (lane < first + EXPERTS_PER_GROUP)
        v1, i1 = masked_argmax(in_group)
        v2, i2 = masked_argmax(in_group & (lane != i1))
        e2 = jnp.exp(v2 - v1)
        w1 = gprob / (1.0 + e2)
        w2 = gprob * e2 / (1.0 + e2)
        if not routed:
            h2_ref, cmb_ref = out_refs
            h2_ref[rows, :] = h2.astype(BF16)
            cmb_ref[rows, :] = (jnp.where(lane + N_GROUPS == i1, w1, 0.0)
                                + jnp.where(lane + N_GROUPS == i2, w2, 0.0))
            continue

        hx_ref, cls_ref, hist_ref = out_refs
        first_is_lo = i1 < i2
        lo_e = jnp.minimum(i1, i2) - first
        hi_e = jnp.maximum(i1, i2) - first
        cls = gidx * N_PAIRS + jnp.right_shift(lo_e * (7 - lo_e), 1) + hi_e - lo_e - 1
        hx_ref[rows, 0:d] = h2
        hx_ref[rows, d:d + LANES] = jnp.where(lane == 0, jnp.where(first_is_lo, w1, w2),
                                              jnp.where(lane == 1, jnp.where(first_is_lo, w2, w1), 0.0))
        cls_ref[rows, :] = cls
        hist = hist + jnp.sum(jnp.where(lane == cls, 1.0, 0.0), axis=0, keepdims=True)

    if routed:
        hist_ref = out_refs[2]

        @pl.when(pl.program_id(0) == 0)
        def _():
            hist_ref[...] = jnp.zeros_like(hist_ref)

        hist_ref[...] += hist


def _merge(ret_o, dif_o, p, x2d, gt1, sc2, sh2, g2, wr, wd, wo, rwh, rwl, rb, tm, rows_per_group, routed):
    t, d = x2d.shape
    const = lambda shape: pl.BlockSpec(shape, lambda i: (0, 0), pipeline_mode=pl.Buffered(1))
    if routed:
        out_shape = (jax.ShapeDtypeStruct((t, d), F32),
                     jax.ShapeDtypeStruct((t, d + LANES), F32),
                     jax.ShapeDtypeStruct((t, 1), jnp.int32),
                     jax.ShapeDtypeStruct((1, LANES), F32))
        out_specs = (pl.BlockSpec((tm, d), lambda i: (i, 0)),
                     pl.BlockSpec((tm, d + LANES), lambda i: (i, 0)),
                     pl.BlockSpec((tm, 1), lambda i: (i, 0)),
                     pl.BlockSpec((1, LANES), lambda i: (0, 0)))
    else:
        out_shape = (jax.ShapeDtypeStruct((t, d), F32),
                     jax.ShapeDtypeStruct((t, d), BF16),
                     jax.ShapeDtypeStruct((t, LANES), F32))
        out_specs = (pl.BlockSpec((tm, d), lambda i: (i, 0)),
                     pl.BlockSpec((tm, d), lambda i: (i, 0)),
                     pl.BlockSpec((tm, LANES), lambda i: (i, 0)))
    return pl.pallas_call(
        functools.partial(_merge_kernel, routed=routed),
        out_shape=out_shape,
        grid=(t // tm,),
        in_specs=[pl.BlockSpec((RET_HEADS, tm, RET_DV), lambda i: (0, i, 0)),
                  pl.BlockSpec((tm, d), lambda i: (i, 0)),
                  pl.BlockSpec((STRIPS, tm, MXU_COLS), lambda i: (SLAB_GATE, i, 0)),
                  pl.BlockSpec((STRIPS, tm, MXU_COLS), lambda i: (SLAB_GATE + 1, i, 0)),
                  pl.BlockSpec((tm, d), lambda i: (i, 0)),
                  _mod_spec(gt1, tm, rows_per_group, 1),
                  _mod_spec(sc2, tm, rows_per_group, 1),
                  _mod_spec(sh2, tm, rows_per_group, 1),
                  const((1, d)),
                  const(wr.shape), const(wd.shape), const(wo.shape),
                  const(rwh.shape), const(rwl.shape), const(rb.shape)],
        out_specs=out_specs,
        compiler_params=_cparams("arbitrary"),
        name="merge",
    )(ret_o, dif_o, p, p, x2d, gt1, sc2, sh2, g2, wr, wd, wo, rwh, rwl, rb)


def _route_kernel(cls_ref, hist_ref, tri_ref, pos_ref, tinfo_ref, carry_scr, off_scr, *, tile_rows):
    i = pl.program_id(0)

    def excl_cumsum(v):
        lane8 = lax.broadcasted_iota(jnp.int32, v.shape, 1)
        inc = v
        for sh in (1, 2, 4, 8, 16, 32, 64):
            inc = inc + jnp.where(lane8 >= sh, pltpu.roll(inc, sh, 1), 0.0)
        return inc - v

    @pl.when(i == 0)
    def _():
        cnt = hist_ref[...]
        ntile = jnp.floor((cnt + (tile_rows - 1)) * (1.0 / tile_rows))
        start = excl_cumsum(jnp.broadcast_to(ntile, (SUBLANES, LANES)))[0:1, :]
        off_scr[...] = start * tile_rows
        carry_scr[...] = jnp.zeros_like(carry_scr)
        nt = tinfo_ref.shape[0]
        jrow = lax.broadcasted_iota(jnp.int32, (nt, LANES), 0).astype(F32)
        lane = lax.broadcasted_iota(jnp.int32, (nt, LANES), 1)
        is_cls = lane < N_CLASSES
        tcls = jnp.sum(jnp.where(is_cls & (start <= jrow), 1.0, 0.0), axis=-1, keepdims=True) - 1.0
        pick = lane == tcls.astype(jnp.int32)
        cnt_j = jnp.sum(jnp.where(pick, cnt, 0.0), axis=-1, keepdims=True)
        start_j = jnp.sum(jnp.where(pick, start, 0.0), axis=-1, keepdims=True)
        valid = jnp.clip(cnt_j - (jrow[:, 0:1] - start_j) * tile_rows, 0.0, float(tile_rows)).astype(jnp.int32)
        tc = tcls.astype(jnp.int32)
        grp = sum((tc >= g * N_PAIRS).astype(jnp.int32) for g in range(1, N_GROUPS))
        pair = tc - grp * N_PAIRS
        lo = (pair >= 3).astype(jnp.int32) + (pair >= 5).astype(jnp.int32)
        hi = pair - jnp.right_shift(lo * (7 - lo), 1) + lo + 1
        ea = grp * EXPERTS_PER_GROUP + lo
        eb = grp * EXPERTS_PER_GROUP + hi
        tinfo_ref[...] = jnp.where(lane == 0, ea, jnp.where(lane == 1, eb, jnp.where(lane == 2, valid, 0)))

    cls = cls_ref[...]
    lane = lax.broadcasted_iota(jnp.int32, (cls.shape[0], LANES), 1)
    onehot = lane == cls
    earlier = _dot(tri_ref[...], jnp.where(onehot, 1.0, 0.0).astype(BF16))
    base = carry_scr[...] + off_scr[...]
    pos = jnp.sum(jnp.where(onehot, earlier + base, 0.0), axis=-1, keepdims=True)
    pos_ref[...] = pos.astype(jnp.int32)
    carry_scr[...] += jnp.sum(jnp.where(onehot, 1.0, 0.0), axis=0, keepdims=True)


def _route(cls, hist, tile_rows, n_tiles, tm):
    t = cls.shape[0]
    tri = jnp.tril(jnp.ones((tm, tm), BF16), -1)
    return pl.pallas_call(
        functools.partial(_route_kernel, tile_rows=tile_rows),
        out_shape=(jax.ShapeDtypeStruct((t, 1), jnp.int32),
                   jax.ShapeDtypeStruct((n_tiles, LANES), jnp.int32)),
        grid=(t // tm,),
        in_specs=[pl.BlockSpec((tm, 1), lambda i: (i, 0)),
                  pl.BlockSpec((1, LANES), lambda i: (0, 0)),
                  pl.BlockSpec((tm, tm), lambda i: (0, 0))],
        out_specs=(pl.BlockSpec((tm, 1), lambda i: (i, 0)),
                   pl.BlockSpec((n_tiles, LANES), lambda i: (0, 0))),
        scratch_shapes=[pltpu.VMEM((1, LANES), F32), pltpu.VMEM((1, LANES), F32)],
        compiler_params=_cparams("arbitrary"),
        name="route",
    )(cls, hist, tri)


def _start_and_wait_rows(row_copy, n_rows):
    def start(g, carry):
        for u in range(DMA_UNROLL):
            row_copy(g * DMA_UNROLL + u).start(priority=u % 2)
        return carry

    def wait(r, carry):
        row_copy(r).wait()
        return carry

    lax.fori_loop(0, n_rows // DMA_UNROLL, start, 0)
    lax.fori_loop(0, n_rows, wait, 0, unroll=DMA_UNROLL)


def _dispatch_kernel(pos_ref, hx_ref, sorted_ref, sem):
    tm = hx_ref.shape[0]

    def row_copy(r):
        return pltpu.make_async_copy(hx_ref.at[pl.ds(r, 1)], sorted_ref.at[pl.ds(pos_ref[0, r], 1)], sem)

    _start_and_wait_rows(row_copy, tm)


def _dispatch(pos, hx, n_slots, tm):
    t, w = hx.shape
    return pl.pallas_call(
        _dispatch_kernel,
        out_shape=jax.ShapeDtypeStruct((n_slots, w), F32),
        grid=(t // tm,),
        in_specs=[pl.BlockSpec((None, 1, tm), lambda i: (i, 0, 0), memory_space=pltpu.SMEM),
                  pl.BlockSpec((tm, w), lambda i: (i, 0))],
        out_specs=pl.BlockSpec(memory_space=pl.ANY),
        scratch_shapes=[pltpu.SemaphoreType.DMA],
        compiler_params=pltpu.CompilerParams(dimension_semantics=("arbitrary",), vmem_limit_bytes=VMEM_LIMIT,
                                             has_side_effects=True),
        name="dispatch",
    )(pos.reshape(t // tm, 1, tm), hx)


def _moe_routed_kernel(ea_ref, eb_ref, valid_ref, hx_ref, w1a_ref, w3a_ref, w2a_ref, w1b_ref, w3b_ref, w2b_ref, y_ref):
    j = pl.program_id(0)
    nvalid = valid_ref[j]

    @pl.when(nvalid == 0)
    def _():
        y_ref[...] = jnp.zeros_like(y_ref)

    @pl.when(nvalid > 0)
    def _():
        tm, d = y_ref.shape
        row = lax.broadcasted_iota(jnp.int32, (tm, 1), 0)
        real = row < nvalid
        h = jnp.where(real, hx_ref[:, 0:d], 0.0).astype(BF16)
        wts = jnp.where(real, hx_ref[:, d:d + LANES], 0.0)

        def expert(w1_ref, w3_ref, w2_ref):
            a = _dot(h, w1_ref[...])
            g = (a * jax.nn.sigmoid(a)) * _dot(h, w3_ref[...])
            return _dot(g.astype(BF16), w2_ref[...])

        y_ref[...] = (wts[:, 0:1] * expert(w1a_ref, w3a_ref, w2a_ref)
                      + wts[:, 1:2] * expert(w1b_ref, w3b_ref, w2b_ref))


def _moe_routed(hx_sorted, ea, eb, valid, w1, w3, w2, tile_rows):
    n_slots, w = hx_sorted.shape
    ne, d, de = w1.shape
    up = lambda sel: pl.BlockSpec((None, d, de), lambda j, ea, eb, valid: ((ea, eb)[sel][j], 0, 0))
    down = lambda sel: pl.BlockSpec((None, de, d), lambda j, ea, eb, valid: ((ea, eb)[sel][j], 0, 0))
    grid_spec = pltpu.PrefetchScalarGridSpec(
        num_scalar_prefetch=3,
        grid=(n_slots // tile_rows,),
        in_specs=[pl.BlockSpec((tile_rows, w), lambda j, ea, eb, valid: (j, 0)),
                  up(0), up(0), down(0), up(1), up(1), down(1)],
        out_specs=pl.BlockSpec((tile_rows, d), lambda j, ea, eb, valid: (j, 0)),
    )
    return pl.pallas_call(
        _moe_routed_kernel,
        out_shape=jax.ShapeDtypeStruct((n_slots, d), F32),
        grid_spec=grid_spec,
        compiler_params=_cparams("arbitrary"),
        name="moe_routed",
    )(ea, eb, valid, hx_sorted, w1, w3, w2, w1, w3, w2)


def _combine_kernel(pos_ref, ys_ref, x1_ref, gt2_ref, y_ref, buf, sem):
    tm = buf.shape[0]

    def row_copy(r):
        return pltpu.make_async_copy(ys_ref.at[pl.ds(pos_ref[0, r], 1)], buf.at[pl.ds(r, 1)], sem)

    _start_and_wait_rows(row_copy, tm)
    y_ref[...] = x1_ref[...] + gt2_ref[...] * buf[...]


def _combine(pos, ys, x1, gt2, tm, rows_per_group):
    t, d = x1.shape
    return pl.pallas_call(
        _combine_kernel,
        out_shape=jax.ShapeDtypeStruct((t, d), F32),
        grid=(t // tm,),
        in_specs=[pl.BlockSpec((None, 1, tm), lambda i: (i, 0, 0), memory_space=pltpu.SMEM),
                  pl.BlockSpec(memory_space=pl.ANY),
                  pl.BlockSpec((tm, d), lambda i: (i, 0)),
                  _mod_spec(gt2, tm, rows_per_group, 1)],
        out_specs=pl.BlockSpec((tm, d), lambda i: (i, 0)),
        scratch_shapes=[pltpu.VMEM((tm, d), F32), pltpu.SemaphoreType.DMA],
        compiler_params=_cparams("arbitrary"),
        name="combine",
    )(pos.reshape(t // tm, 1, tm), ys, x1, gt2)


def _moe_kernel(h_ref, cmb_ref, x1_ref, gt2_ref, w1_ref, w3_ref, w2_ref, y_ref, acc_scr):
    e = pl.program_id(1)

    @pl.when(e == 0)
    def _():
        acc_scr[...] = jnp.zeros_like(acc_scr)

    h = h_ref[...]
    a = _dot(h, w1_ref[...])
    g = (a * jax.nn.sigmoid(a)) * _dot(h, w3_ref[...])
    ff = _dot(g.astype(BF16), w2_ref[...])
    cmb = cmb_ref[...]
    lane = lax.broadcasted_iota(jnp.int32, cmb.shape, 1)
    w = jnp.sum(jnp.where(lane == e, cmb, 0.0), axis=-1, keepdims=True)
    acc_scr[...] += w * ff

    @pl.when(e == pl.num_programs(1) - 1)
    def _():
        y_ref[...] = x1_ref[...] + gt2_ref[...] * acc_scr[...]


def _moe(h2, cmb, x1, gt2, w1, w3, w2, tm, rows_per_group):
    t, d = x1.shape
    ne, _, de = w1.shape
    gt2_spec = (pl.BlockSpec((None, 1, d), lambda i, e: (i // (rows_per_group // tm), 0, 0))
                if gt2.ndim == 3 else pl.BlockSpec((tm, d), lambda i, e: (i, 0)))
    return pl.pallas_call(
        _moe_kernel,
        out_shape=jax.ShapeDtypeStruct((t, d), F32),
        grid=(t // tm, ne),
        in_specs=[pl.BlockSpec((tm, d), lambda i, e: (i, 0)),
                  pl.BlockSpec((tm, LANES), lambda i, e: (i, 0)),
                  pl.BlockSpec((tm, d), lambda i, e: (i, 0)),
                  gt2_spec,
                  pl.BlockSpec((None, d, de), lambda i, e: (e, 0, 0)),
                  pl.BlockSpec((None, d, de), lambda i, e: (e, 0, 0)),
                  pl.BlockSpec((None, de, d), lambda i, e: (e, 0, 0))],
        out_specs=pl.BlockSpec((tm, d), lambda i, e: (i, 0)),
        scratch_shapes=[pltpu.VMEM((tm, d), F32)],
        compiler_params=_cparams("arbitrary", "arbitrary"),
        name="moe",
    )(h2, cmb, x1, gt2, w1, w3, w2)


def _rope_tables(pos, gq, gk):
    posf = pos.astype(F32)[:, None]
    inv_r = ROPE_THETA ** (-jnp.arange(0, RET_DK, 2, dtype=F32) / RET_DK)
    ang_r = posf * inv_r[None, :]
    inv_d = ROPE_THETA ** (-jnp.arange(0, DIFF_DH, 2, dtype=F32) / DIFF_DH)
    lane = jnp.arange(LANES)
    half = DIFF_DH // 2
    ang_d = posf * inv_d[lane % half][None, :]
    first_half = ((lane % DIFF_DH) < half)[None, :]
    cos_d, sin_d = jnp.cos(ang_d), jnp.sin(ang_d)

    def diff_tables(g, scale):
        g128 = jnp.tile(g, LANES // DIFF_DH)
        return (cos_d * (g128 * scale)[None, :],
                jnp.where(first_half, -sin_d, 0.0) * (jnp.roll(g128, -half) * scale)[None, :],
                jnp.where(first_half, 0.0, sin_d) * (jnp.roll(g128, half) * scale)[None, :])

    return (jnp.cos(ang_r), jnp.sin(ang_r)) + diff_tables(gq, DIFF_DH ** -0.5 * LOG2E) + diff_tables(gk, 1.0)


def _tile_rows(n, target):
    tm = min(n, target)
    assert n % tm == 0
    return tm


def _layer(x, mod, pos, s0, kc, vc, lw, li, ret_chunk):
    nb, seq, d = x.shape
    t = nb * seq
    x2d = x.reshape(t, d)
    per_token = seq < 128
    if per_token:
        mods = [jnp.repeat(mod[:, k, :], seq, axis=0) for k in range(6)]
        tabs = tuple(jnp.tile(tb, (nb, 1)) for tb in _rope_tables(pos, lw['gq'], lw['gk']))
        rows_per_group = t
    else:
        mods = [mod[:, k:k + 1, :] for k in range(6)]
        tabs = _rope_tables(pos, lw['gq'], lw['gk'])
        rows_per_group = seq
    sh1, sc1, gt1, sh2, sc2, gt2 = mods
    tm = _tile_rows(rows_per_group, 512)

    p, qd, k2d, v2d = _inproj(x2d, sc1, sh1, lw['norm1_g'], lw['w_in'], tabs, lw['bd'],
                              _tile_rows(rows_per_group, 1024), rows_per_group)
    lam_init = 0.8 - 0.6 * math.exp(-0.3 * li)
    if kc is None:
        dif_o = _diffattn(qd, k2d, v2d, lw['lam'], lw['subln_g'], nb, seq, lam_init)
    else:
        dif_o = _diffattn_cached(qd, k2d, v2d, kc, vc, li, lw['lam'], lw['subln_g'], nb, seq, lam_init)
    ret_o, s_new = _retention(p, s0, nb, seq, ret_chunk)
    ret_o, k2d, v2d = lax.optimization_barrier((ret_o, k2d, v2d))
    merge_args = (ret_o, dif_o, p, x2d, gt1, sc2, sh2, lw['norm2_g'], lw['w_br_ret'], lw['w_br_diff'],
                  lw['w_out'], lw['rw_hi'], lw['rw_lo'], lw['rb'], tm, rows_per_group)
    if per_token:
        x1, h2, cmb = _merge(*merge_args, routed=False)
        y = _moe(h2, cmb, x1, gt2, lw['w1'], lw['w3'], lw['w2'], tm, rows_per_group)
    else:
        x1, hx, cls, hist = _merge(*merge_args, routed=True)
        n_tiles = SUBLANES * pl.cdiv(t // MOE_TILE_ROWS + N_CLASSES, SUBLANES)
        tm_rows = _tile_rows(rows_per_group, 1024)
        pos, tinfo = _route(cls, hist, MOE_TILE_ROWS, n_tiles, tm_rows)
        hx_sorted = _dispatch(pos, hx, n_tiles * MOE_TILE_ROWS, tm_rows)
        y_sorted = _moe_routed(hx_sorted, tinfo[:, 0], tinfo[:, 1], tinfo[:, 2],
                               lw['w1'], lw['w3'], lw['w2'], MOE_TILE_ROWS)
        y = _combine(pos, y_sorted, x1, gt2, tm_rows, rows_per_group)
    return (y.reshape(nb, seq, d), s_new, k2d.reshape(nb, seq, DIFF_HEADS, 2 * DIFF_DH),
            v2d.reshape(nb, seq, DIFF_HEADS, DIFF_DV))


def kernel(x_prompt, x_sample, c_prompt, c_sample, cache_diff_k, cache_diff_v, state_ret, w_ada, b_ada, norm1_g, w_in, diff_qnorm_g, diff_knorm_g, diff_lambda_q1, diff_lambda_k1, diff_lambda_q2, diff_lambda_k2, diff_subln_g, w_br_ret, w_br_diff, w_out, norm2_g, w_group, b_group, w_expert_router, b_expert, w1, w3, w2):
    nb, seq, d = x_prompt.shape
    nbs, seqs, _ = x_sample.shape
    depth = w_in.shape[0]
    past = cache_diff_k.shape[2]
    pos_p = jnp.arange(seq, dtype=jnp.int32)
    pos_s = past + jnp.arange(seqs, dtype=jnp.int32)
    c_all = jnp.concatenate([c_prompt, c_sample], axis=0)
    col = jnp.arange(MXU_COLS)
    bd = jnp.where((col[:, None] // DIFF_DH) == (col[None, :] // DIFF_DH), 1.0 / DIFF_DH, 0.0).astype(BF16)

    xp, xs = x_prompt, x_sample
    outs = [[] for _ in range(6)]
    for li in range(depth):
        rw = jnp.zeros((d, LANES), F32)
        rw = rw.at[:, :N_GROUPS].set(w_group[li]).at[:, N_GROUPS:N_GROUPS + N_EXPERTS].set(w_expert_router[li])
        rw_hi = rw.astype(BF16)
        rb = jnp.zeros((1, LANES), F32)
        rb = rb.at[0, :N_GROUPS].set(b_group[li]).at[0, N_GROUPS:N_GROUPS + N_EXPERTS].set(b_expert[li])
        lw = {
            'norm1_g': norm1_g[li].reshape(1, d), 'norm2_g': norm2_g[li].reshape(1, d),
            'w_in': jnp.stack([w_in[li, :, blk * COL_BLOCK:(blk + 1) * COL_BLOCK] for blk in IN_BLOCK_ORDER]
                              ).astype(BF16),
            'gq': diff_qnorm_g[li], 'gk': diff_knorm_g[li],
            'bd': bd,
            'lam': jnp.stack([diff_lambda_q1[li], diff_lambda_k1[li], diff_lambda_q2[li], diff_lambda_k2[li]]),
            'subln_g': diff_subln_g[li].reshape(1, DIFF_DV),
            'w_br_ret': w_br_ret[li].astype(BF16), 'w_br_diff': w_br_diff[li].astype(BF16),
            'w_out': w_out[li].astype(BF16),
            'rw_hi': rw_hi, 'rw_lo': (rw - rw_hi.astype(F32)).astype(BF16), 'rb': rb,
            'w1': w1[li].astype(BF16), 'w3': w3[li].astype(BF16), 'w2': w2[li].astype(BF16),
        }
        mod = _adaln(c_all, w_ada[li], b_ada[li]).reshape(nb + nbs, 6, d)
        xp, sp, kp, vp = _layer(xp, mod[:nb], pos_p, None, None, None, lw, li, ret_chunk=min(seq, 256))
        xs, ss, kss, vss = _layer(xs, mod[nb:], pos_s, state_ret[li], cache_diff_k, cache_diff_v,
                                  lw, li, ret_chunk=seqs)
        for lst, val in zip(outs, (kp, vp, sp, kss, vss, ss)):
            lst.append(val)
    kp, vp, sp, kss, vss, ss = (jnp.stack(lst) for lst in outs)
    return (xp, xs, kp, vp, sp.astype(x_prompt.dtype), kss, vss, ss.astype(state_ret.dtype))
```
